```python
import math
import jax
import jax.numpy as jnp
from jax import lax
import numpy as np


D_MODEL = 2048
BATCH = 2
SEQ = 8192
DEPTH = 2

GRID_W = 64
CTX_LEN = 256
HEAD_DIM = 128
N_HEADS = D_MODEL // HEAD_DIM
A_HEADS = (5 * N_HEADS) // 16
C_HEADS = A_HEADS
B_HEADS = N_HEADS - A_HEADS - C_HEADS
A_W = A_HEADS * HEAD_DIM
B_W = B_HEADS * HEAD_DIM
C_W = C_HEADS * HEAD_DIM
MIX_W = A_W + B_W + C_W
PROJ_SIZES = (A_W, A_W, A_W, A_W, A_W, 3 * B_W, B_W, B_HEADS, B_HEADS, B_HEADS, B_HEADS, C_W, C_W, C_W, C_W)
PROJ_DIM = 5 * A_W + 4 * B_W + 4 * B_HEADS + 4 * C_W
FFN_DIM = ((8 * D_MODEL // 3 + 255) // 256) * 256
CHUNK = 64
SHORT_CONV_W = 3
FFN_CONV_W = 3
ROPE_BASE = 10000.0
NORM_EPS = 1e-6
N_MOD = 6
GATE_CLAMP = 60.0

kernel_name = 'hybrid_hgrn2_gdn_retention_flow_block'


def rmsnorm(x, w):
    xf = x.astype(jnp.float32)
    y = xf * lax.rsqrt(jnp.mean(xf * xf, axis=-1, keepdims=True) + NORM_EPS)
    return (y * w.astype(jnp.float32)).astype(x.dtype)


def to_heads(a, n_heads):
    b, t, _ = a.shape
    return a.reshape(b, t, n_heads, -1).transpose(0, 2, 1, 3)


def from_heads(a):
    b, h, t, d = a.shape
    return a.transpose(0, 2, 1, 3).reshape(b, t, h * d)


def head_rmsnorm(o):
    return o * lax.rsqrt(jnp.mean(o * o, axis=-1, keepdims=True) + NORM_EPS)


def head_layernorm(o):
    oc = o - jnp.mean(o, axis=-1, keepdims=True)
    return oc * lax.rsqrt(jnp.mean(oc * oc, axis=-1, keepdims=True) + NORM_EPS)


def l2norm(a):
    return a * lax.rsqrt(jnp.sum(a * a, axis=-1, keepdims=True) + NORM_EPS)


def masked_exp(a, mask):
    return jnp.where(mask, jnp.exp(jnp.where(mask, a, 0.0)), 0.0)


def dir_order(a, n_ctx):
    return jnp.concatenate([jnp.flip(a[:, :, :n_ctx], axis=2), jnp.flip(a[:, :, n_ctx:], axis=2)], axis=2)


def rotary(a, pos):
    half = a.shape[-1] // 2
    inv = ROPE_BASE ** (-jnp.arange(half, dtype=jnp.float32) / half)
    ang = pos[:, None] * inv[None, :]
    cos, sin = jnp.cos(ang), jnp.sin(ang)
    a1, a2 = a[..., :half], a[..., half:]
    return jnp.concatenate([a1 * cos - a2 * sin, a1 * sin + a2 * cos], axis=-1)


def short_conv(a, w):
    return lax.conv_general_dilated(a, w[:, None, :], (1,), 'SAME',
                                    dimension_numbers=('NWC', 'WIO', 'NWC'),
                                    feature_group_count=a.shape[-1])


def dwconv2d(a, w, rows):
    b, t, f = a.shape
    img = a.reshape(b, rows, t // rows, f)
    out = lax.conv_general_dilated(img, w[:, :, None, :], (1, 1), 'SAME',
                                   dimension_numbers=('NHWC', 'HWIO', 'NHWC'),
                                   feature_group_count=f)
    return out.reshape(b, t, f)


def hgrn_lower_bounds(logits):
    p = jax.nn.softmax(logits.astype(jnp.float32), axis=0)
    return jnp.cumsum(p, axis=0) - p[0]


def retention_log_gamma():
    h = jnp.arange(C_HEADS, dtype=jnp.float32)
    return jnp.log1p(-jnp.exp2(-5.0 - h))


def chunk_gla(q, k, v, logf):
    b, h, t, dk = q.shape
    dv = v.shape[-1]
    n = t // CHUNK
    qc = q.reshape(b, h, n, CHUNK, dk)
    kc = k.reshape(b, h, n, CHUNK, dk)
    vc = v.reshape(b, h, n, CHUNK, dv)
    bc = jnp.cumsum(logf.reshape(b, h, n, CHUNK, dk), axis=3)
    incl = jnp.tril(jnp.ones((CHUNK, CHUNK), dtype=bool))[:, :, None]

    def step(state, inp):
        q_i, k_i, v_i, b_i = inp
        diff = b_i[:, :, :, None, :] - b_i[:, :, None, :, :]
        dec = masked_exp(diff, incl)
        scores = jnp.einsum('bhtd,bhsd,bhtsd->bhts', q_i, k_i, dec)
        b_last = b_i[:, :, -1:, :]
        out = (jnp.einsum('bhts,bhsv->bhtv', scores, v_i)
               + jnp.einsum('bhtd,bhdv->bhtv', q_i * jnp.exp(b_i), state))
        state = (jnp.exp(b_last[:, :, 0, :, None]) * state
                 + jnp.einsum('bhsd,bhsv->bhdv', k_i * jnp.exp(b_last - b_i), v_i))
        return state, out

    xs = (jnp.moveaxis(qc, 2, 0), jnp.moveaxis(kc, 2, 0), jnp.moveaxis(vc, 2, 0), jnp.moveaxis(bc, 2, 0))
    s0 = jnp.zeros((b, h, dk, dv), q.dtype)
    _, o = lax.scan(step, s0, xs)
    return jnp.moveaxis(o, 0, 2).reshape(b, h, t, dv)


def chunk_gated_delta(q, k, v, g, beta):
    b, h, t, dk = q.shape
    dv = v.shape[-1]
    n = t // CHUNK
    qc = q.reshape(b, h, n, CHUNK, dk)
    kc = k.reshape(b, h, n, CHUNK, dk)
    vc = v.reshape(b, h, n, CHUNK, dv)
    bc = jnp.cumsum(g.reshape(b, h, n, CHUNK), axis=-1)
    betac = beta.reshape(b, h, n, CHUNK)[..., None]
    incl = jnp.tril(jnp.ones((CHUNK, CHUNK), dtype=bool))
    strict = jnp.tril(jnp.ones((CHUNK, CHUNK), dtype=bool), -1)
    lmat = masked_exp(bc[..., :, None] - bc[..., None, :], incl)
    kbeta = kc * betac
    amat = jnp.where(strict, jnp.einsum('bhnid,bhnjd->bhnij', kbeta, kc) * lmat, 0.0)
    rhs = jnp.concatenate([vc * betac, kbeta * jnp.exp(bc)[..., None]], axis=-1)
    sol = lax.linalg.triangular_solve(amat, rhs, left_side=True, lower=True, unit_diagonal=True)
    u, w = sol[..., :dv], sol[..., dv:]
    qk = jnp.einsum('bhnid,bhnjd->bhnij', qc, kc) * lmat
    q_dec = qc * jnp.exp(bc)[..., None]
    k_dec = kc * jnp.exp(bc[..., -1:] - bc)[..., None]
    last = jnp.exp(bc[..., -1])

    def step(state, inp):
        u_i, w_i, qk_i, qd_i, kd_i, last_i = inp
        v_new = u_i - jnp.einsum('bhcd,bhdv->bhcv', w_i, state)
        out = (jnp.einsum('bhcd,bhdv->bhcv', qd_i, state)
               + jnp.einsum('bhij,bhjv->bhiv', qk_i, v_new))
        state = last_i[..., None, None] * state + jnp.einsum('bhcd,bhcv->bhdv', kd_i, v_new)
        return state, out

    xs = tuple(jnp.moveaxis(a, 2, 0) for a in (u, w, qk, q_dec, k_dec, last))
    s0 = jnp.zeros((b, h, dk, dv), q.dtype)
    _, o = lax.scan(step, s0, xs)
    return jnp.moveaxis(o, 0, 2).reshape(b, h, t, dv)


def chunk_retention(q, k, v, log_gamma):
    b, h, t, dk = q.shape
    dv = v.shape[-1]
    n = t // CHUNK
    qc = q.reshape(b, h, n, CHUNK, dk)
    kc = k.reshape(b, h, n, CHUNK, dk)
    vc = v.reshape(b, h, n, CHUNK, dv)
    idx = jnp.arange(CHUNK, dtype=jnp.float32)
    rel = idx[:, None] - idx[None, :]
    dmat = masked_exp(log_gamma[:, None, None] * rel, (rel >= 0)[None])
    scores = jnp.einsum('bhnid,bhnjd->bhnij', qc, kc) * dmat[None, :, None]
    intra = jnp.einsum('bhnij,bhnjv->bhniv', scores, vc)
    q_dec = qc * jnp.exp(log_gamma[:, None] * (idx + 1.0))[None, :, None, :, None]
    k_dec = kc * jnp.exp(log_gamma[:, None] * (CHUNK - 1.0 - idx))[None, :, None, :, None]
    chunk_dec = jnp.exp(log_gamma * CHUNK)[None, :, None, None]

    def step(state, inp):
        qd_i, kd_i, v_i = inp
        out = jnp.einsum('bhcd,bhdv->bhcv', qd_i, state)
        state = chunk_dec * state + jnp.einsum('bhcd,bhcv->bhdv', kd_i, v_i)
        return state, out

    xs = (jnp.moveaxis(q_dec, 2, 0), jnp.moveaxis(k_dec, 2, 0), jnp.moveaxis(vc, 2, 0))
    s0 = jnp.zeros((b, h, dk, dv), q.dtype)
    _, inter = lax.scan(step, s0, xs)
    return (intra + jnp.moveaxis(inter, 0, 2)).reshape(b, h, t, dv)


def token_mixer(hc, hl, w_in, qkv_conv_w, lb_f, lb_b, hgrn_norm_w, a_log_f, a_log_b,
                dt_f, dt_b, gdn_norm_w, ret_norm_w, keep_ctx):
    f32 = jnp.float32
    n_ctx = hc.shape[1]
    z = jnp.concatenate([hc, hl], axis=1) @ w_in
    t = z.shape[1]
    split_at = [int(s) for s in np.cumsum(PROJ_SIZES)[:-1]]
    (a_q, a_i, a_ff, a_fb, a_g, b_qkv, b_g, b_af, b_ab, b_bf, b_bb,
     c_q, c_k, c_v, c_g) = jnp.split(z, split_at, axis=-1)
    start = 0 if keep_ctx else n_ctx
    flip = lambda a: dir_order(a, n_ctx)
    scale = HEAD_DIM ** -0.5

    aq = to_heads(jax.nn.silu(a_q.astype(f32)), A_HEADS) * scale
    ai = to_heads(a_i.astype(f32), A_HEADS)

    def hgrn_gates(zf, lb):
        zf = to_heads(zf.astype(f32), A_HEADS)
        lb = lb.astype(f32).reshape(A_HEADS, 1, HEAD_DIM)
        logf = jax.nn.log_sigmoid(zf) + jnp.log1p(lb * jnp.exp(-jnp.maximum(zf, -GATE_CLAMP)))
        return (1.0 - lb) * jax.nn.sigmoid(-zf), logf

    ka_f, lf_f = hgrn_gates(a_ff, lb_f)
    ka_b, lf_b = hgrn_gates(a_fb, lb_b)
    oa = chunk_gla(aq, ka_f, ai, lf_f) + flip(chunk_gla(flip(aq), flip(ka_b), flip(ai), flip(lf_b)))
    oa = from_heads(head_rmsnorm(oa[:, :, start:]) * hgrn_norm_w.astype(f32)) * jax.nn.silu(a_g[:, start:].astype(f32))

    qkv = jnp.concatenate([short_conv(b_qkv[:, :n_ctx], qkv_conv_w),
                           short_conv(b_qkv[:, n_ctx:], qkv_conv_w)], axis=1)
    bq, bk, bv = jnp.split(jax.nn.silu(qkv.astype(f32)), 3, axis=-1)
    bq = l2norm(to_heads(bq, B_HEADS)) * scale
    bk = l2norm(to_heads(bk, B_HEADS))
    bv = to_heads(bv, B_HEADS)

    def gdn_gates(za, zb, a_log, dt_bias):
        g = (-jnp.exp(a_log.astype(f32))[None, :, None]
             * jax.nn.softplus(za.astype(f32) + dt_bias.astype(f32)).transpose(0, 2, 1))
        return g, jax.nn.sigmoid(zb.astype(f32)).transpose(0, 2, 1)

    g_f, beta_f = gdn_gates(b_af, b_bf, a_log_f, dt_f)
    g_b, beta_b = gdn_gates(b_ab, b_bb, a_log_b, dt_b)
    ob = (chunk_gated_delta(bq, bk, bv, g_f, beta_f)
          + flip(chunk_gated_delta(flip(bq), flip(bk), flip(bv), flip(g_b), flip(beta_b))))
    ob = from_heads(head_rmsnorm(ob[:, :, start:]) * gdn_norm_w.astype(f32)) * jax.nn.silu(b_g[:, start:].astype(f32))

    rq = to_heads(c_q.astype(f32), C_HEADS) * scale
    rk = to_heads(c_k.astype(f32), C_HEADS)
    rv = to_heads(c_v.astype(f32), C_HEADS)
    pos = jnp.arange(t, dtype=f32)
    log_gamma = retention_log_gamma()

    def retention(q, k, v, lg):
        return chunk_retention(rotary(q, pos), rotary(k, pos), v, lg)

    oc = retention(rq, rk, rv, log_gamma) + flip(retention(flip(rq), flip(rk), flip(rv), log_gamma[::-1]))
    oc = from_heads(head_layernorm(oc[:, :, start:])) * ret_norm_w.astype(f32) * jax.nn.silu(c_g[:, start:].astype(f32))

    return jnp.concatenate([oa, ob, oc], axis=-1).astype(hl.dtype)


def conv_ffn(h, w_up, conv_w, conv_b, w_down, rows):
    u, v = jnp.split(h @ w_up, 2, axis=-1)
    u = dwconv2d(u, conv_w, rows) + conv_b
    return (jax.nn.silu(u) * v) @ w_down


def setup_inputs(seed: int = 0) -> dict:
    key = jax.random.key(seed)
    ks = jax.random.split(key, 32)
    f32 = jnp.float32
    nrm = lambda k, shape, s: jax.random.normal(k, shape, f32) * s
    gain = lambda k, shape: 1.0 + 0.05 * jax.random.normal(k, shape, f32)
    dt_lo, dt_hi = math.log(1e-3), math.log(1e-1)
    dt_f = jnp.exp(jax.random.uniform(ks[15], (DEPTH, B_HEADS), f32, dt_lo, dt_hi))
    dt_b = jnp.exp(jax.random.uniform(ks[16], (DEPTH, B_HEADS), f32, dt_lo, dt_hi))
    return {
        'x': nrm(ks[0], (BATCH, SEQ, D_MODEL), 1.0),
        'c': nrm(ks[1], (BATCH, D_MODEL), 1.0),
        'ctx': nrm(ks[2], (BATCH, CTX_LEN, D_MODEL), 1.0),
        'c_ctx': nrm(ks[3], (D_MODEL,), 1.0),
        'w_mod': nrm(ks[4], (DEPTH, D_MODEL, N_MOD * D_MODEL), D_MODEL ** -0.5),
        'b_mod': nrm(ks[5], (DEPTH, N_MOD * D_MODEL), 0.01),
        'norm1_w': gain(ks[6], (DEPTH, D_MODEL)),
        'norm2_w': gain(ks[7], (DEPTH, D_MODEL)),
        'w_in': nrm(ks[8], (DEPTH, D_MODEL, PROJ_DIM), D_MODEL ** -0.5),
        'qkv_conv_w': nrm(ks[9], (DEPTH, SHORT_CONV_W, 3 * B_W), SHORT_CONV_W ** -0.5),
        'hgrn_lb_logits_fwd': nrm(ks[10], (DEPTH, A_W), 1.0),
        'hgrn_lb_logits_bwd': nrm(ks[11], (DEPTH, A_W), 1.0),
        'hgrn_norm_w': gain(ks[12], (DEPTH, HEAD_DIM)),
        'gdn_a_log_fwd': jnp.log(jax.random.uniform(ks[13], (DEPTH, B_HEADS), f32, 1.0, 16.0)),
        'gdn_a_log_bwd': jnp.log(jax.random.uniform(ks[14], (DEPTH, B_HEADS), f32, 1.0, 16.0)),
        'gdn_dt_bias_fwd': dt_f + jnp.log(-jnp.expm1(-dt_f)),
        'gdn_dt_bias_bwd': dt_b + jnp.log(-jnp.expm1(-dt_b)),
        'gdn_norm_w': gain(ks[17], (DEPTH, HEAD_DIM)),
        'ret_norm_w': gain(ks[18], (DEPTH, C_W)),
        'w_out': nrm(ks[19], (DEPTH, MIX_W, D_MODEL), MIX_W ** -0.5),
        'ffn_w_up': nrm(ks[20], (DEPTH, D_MODEL, 2 * FFN_DIM), D_MODEL ** -0.5),
        'ffn_conv_w': nrm(ks[21], (DEPTH, FFN_CONV_W, FFN_CONV_W, FFN_DIM), 1.0 / FFN_CONV_W),
        'ffn_conv_b': nrm(ks[22], (DEPTH, FFN_DIM), 0.01),
        'ffn_w_down': nrm(ks[23], (DEPTH, FFN_DIM, D_MODEL), FFN_DIM ** -0.5),
        'final_norm_w': gain(ks[24], (D_MODEL,)),
    }


def reference(x, c, ctx, c_ctx, w_mod, b_mod, norm1_w, norm2_w, w_in, qkv_conv_w,
              hgrn_lb_logits_fwd, hgrn_lb_logits_bwd, hgrn_norm_w, gdn_a_log_fwd, gdn_a_log_bwd,
              gdn_dt_bias_fwd, gdn_dt_bias_bwd, gdn_norm_w, ret_norm_w, w_out, ffn_w_up,
              ffn_conv_w, ffn_conv_b, ffn_w_down, final_norm_w):
    n_lat = x.shape[1]
    rows = n_lat // GRID_W
    n_ctx = ctx.shape[1]
    lb_fwd = hgrn_lower_bounds(hgrn_lb_logits_fwd)
    lb_bwd = hgrn_lower_bounds(hgrn_lb_logits_bwd)
    xc = ctx
    for l in range(DEPTH):
        keep_ctx = l < DEPTH - 1
        mod = jax.nn.silu(c) @ w_mod[l] + b_mod[l]
        cmod = jax.nn.silu(c_ctx) @ w_mod[l] + b_mod[l]
        sh1, sc1, g1, sh2, sc2, g2 = jnp.split(mod[:, None, :], N_MOD, axis=-1)
        csh1, csc1, cg1, csh2, csc2, cg2 = jnp.split(cmod[None, None, :], N_MOD, axis=-1)

        hl = rmsnorm(x, norm1_w[l]) * (1.0 + sc1) + sh1
        hc = rmsnorm(xc, norm1_w[l]) * (1.0 + csc1) + csh1
        merged = token_mixer(hc, hl, w_in[l], qkv_conv_w[l], lb_fwd[l], lb_bwd[l], hgrn_norm_w[l],
                             gdn_a_log_fwd[l], gdn_a_log_bwd[l], gdn_dt_bias_fwd[l], gdn_dt_bias_bwd[l],
                             gdn_norm_w[l], ret_norm_w[l], keep_ctx)
        y = merged @ w_out[l]
        x = x + g1 * y[:, -n_lat:]
        if keep_ctx:
            xc = xc + cg1 * y[:, :n_ctx]

        hl = rmsnorm(x, norm2_w[l]) * (1.0 + sc2) + sh2
        x = x + g2 * conv_ffn(hl, ffn_w_up[l], ffn_conv_w[l], ffn_conv_b[l], ffn_w_down[l], rows)
        if keep_ctx:
            hc = rmsnorm(xc, norm2_w[l]) * (1.0 + csc2) + csh2
            xc = xc + cg2 * conv_ffn(hc, ffn_w_up[l], ffn_conv_w[l][1:2], ffn_conv_b[l], ffn_w_down[l], 1)
    return rmsnorm(x, final_norm_w)
```

```python
import functools
import math

import numpy as np
import jax
import jax.numpy as jnp
from jax import lax
from jax.experimental import pallas as pl
from jax.experimental.pallas import tpu as pltpu

F32 = jnp.float32
BF16 = jnp.bfloat16

HEAD_DIM = 128
N_HEADS = 16
A_HEADS = 5
C_HEADS = 5
B_HEADS = 6
A_W = A_HEADS * HEAD_DIM
B_W = B_HEADS * HEAD_DIM
C_W = C_HEADS * HEAD_DIM
N_GATES = 4 * B_HEADS
CHUNK = 64
GRID_W = 64
ROPE_BASE = 10000.0
NORM_EPS = 1e-6
N_MOD = 6
GATE_CLAMP = 60.0
SCALE = HEAD_DIM ** -0.5
LANES = 128
MOD_ROWS = 8
ROW_TILE = 256
VMEM_LIMIT = 56 * 1024 * 1024

ZB_AQ, ZB_AI, ZB_AFF, ZB_AFB, ZB_AG = 0, 5, 10, 15, 20
ZB_BQ, ZB_BK, ZB_BV, ZB_BG = 25, 31, 37, 43
ZB_CQ, ZB_CK, ZB_CV, ZB_CG = 49, 54, 59, 64
Z_BLOCKS = 69


def _cparams(n_axes):
    return pltpu.CompilerParams(dimension_semantics=("arbitrary",) * n_axes,
                                vmem_limit_bytes=VMEM_LIMIT)


def _dot(a, b):
    return jnp.dot(a, b, preferred_element_type=F32)


def _dot_nt(a, b):
    return lax.dot_general(a, b, (((1,), (1,)), ((), ())), preferred_element_type=F32)


def _dot_tn(a, b):
    return lax.dot_general(a, b, (((0,), (0,)), ((), ())), preferred_element_type=F32)


def _bf(a):
    return a.astype(BF16)


def _split2(a):
    hi = a.astype(BF16)
    lo = (a - hi.astype(F32)).astype(BF16)
    return hi, lo


def _dot3(a, b):
    ah, al = _split2(a)
    bh, bl = _split2(b)
    return _dot(ah, bh) + (_dot(al, bh) + _dot(ah, bl))


def _silu(a):
    return a * jax.nn.sigmoid(a)


def _softplus(a):
    return jnp.maximum(a, 0.0) + jnp.log1p(jnp.exp(-jnp.abs(a)))


def _log_sigmoid(a):
    return jnp.minimum(a, 0.0) - jnp.log1p(jnp.exp(-jnp.abs(a)))


def _mod_kernel(a_ref, w_ref, b_ref, o_ref):
    a = _silu(a_ref[...])
    o_ref[0] = _dot3(a, w_ref[0]) + b_ref[0]


def _modulation(rows, w_mod, b_mod):
    depth, d, n = w_mod.shape
    tn = 1024
    return pl.pallas_call(
        _mod_kernel,
        grid=(depth, n // tn),
        in_specs=[pl.BlockSpec((MOD_ROWS, d), lambda l, j: (0, 0)),
                  pl.BlockSpec((1, d, tn), lambda l, j: (l, 0, j)),
                  pl.BlockSpec((1, 1, tn), lambda l, j: (l, 0, j))],
        out_specs=pl.BlockSpec((1, MOD_ROWS, tn), lambda l, j: (l, 0, j)),
        out_shape=jax.ShapeDtypeStruct((depth, MOD_ROWS, n), F32),
        compiler_params=_cparams(2),
        name="modulation",
    )(rows, w_mod, b_mod.reshape(depth, 1, n))


def _norm_mod_kernel(x_ref, w_ref, sc_ref, sh_ref, o_ref):
    x = x_ref[0]
    y = x * lax.rsqrt(jnp.mean(x * x, axis=-1, keepdims=True) + NORM_EPS) * w_ref[...]
    o_ref[0] = (y * (1.0 + sc_ref[0]) + sh_ref[0]).astype(o_ref.dtype)


def _mod_row(b, i, n_ctx_tiles, n_batch, k):
    return (jnp.where(i < n_ctx_tiles, n_batch, b) * N_MOD + k, 0, 0)


def _norm_mod(x, w, mod, k_scale, k_shift, n_ctx):
    bsz, t, d = x.shape
    nct = n_ctx // ROW_TILE
    return pl.pallas_call(
        _norm_mod_kernel,
        grid=(bsz, t // ROW_TILE),
        in_specs=[pl.BlockSpec((1, ROW_TILE, d), lambda b, i: (b, i, 0)),
                  pl.BlockSpec((1, d), lambda b, i: (0, 0)),
                  pl.BlockSpec((1, 1, d), lambda b, i: _mod_row(b, i, nct, bsz, k_scale)),
                  pl.BlockSpec((1, 1, d), lambda b, i: _mod_row(b, i, nct, bsz, k_shift))],
        out_specs=pl.BlockSpec((1, ROW_TILE, d), lambda b, i: (b, i, 0)),
        out_shape=jax.ShapeDtypeStruct((bsz, t, d), BF16),
        compiler_params=_cparams(2),
        name="norm_mod",
    )(x, w.reshape(1, d), mod, mod)


def _final_norm_kernel(x_ref, w_ref, o_ref):
    x = x_ref[0]
    o_ref[0] = x * lax.rsqrt(jnp.mean(x * x, axis=-1, keepdims=True) + NORM_EPS) * w_ref[...]


def _final_norm(x, w, n_ctx):
    bsz, t, d = x.shape
    nct = n_ctx // ROW_TILE
    return pl.pallas_call(
        _final_norm_kernel,
        grid=(bsz, (t - n_ctx) // ROW_TILE),
        in_specs=[pl.BlockSpec((1, ROW_TILE, d), lambda b, i: (b, i + nct, 0)),
                  pl.BlockSpec((1, d), lambda b, i: (0, 0))],
        out_specs=pl.BlockSpec((1, ROW_TILE, d), lambda b, i: (b, i, 0)),
        out_shape=jax.ShapeDtypeStruct((bsz, t - n_ctx, d), F32),
        compiler_params=_cparams(2),
        name="final_norm",
    )(x, w.reshape(1, d))


def _mm_kernel(a_ref, w_ref, o_ref):
    o_ref[...] = _dot(a_ref[...], w_ref[...]).astype(o_ref.dtype)


def _matmul(a, w, tm, tn, out_dtype, name):
    m, k = a.shape
    n = w.shape[1]
    return pl.pallas_call(
        _mm_kernel,
        grid=(n // tn, m // tm),
        in_specs=[pl.BlockSpec((tm, k), lambda j, i: (i, 0)),
                  pl.BlockSpec((k, tn), lambda j, i: (0, j))],
        out_specs=pl.BlockSpec((tm, tn), lambda j, i: (i, j)),
        out_shape=jax.ShapeDtypeStruct((m, n), out_dtype),
        compiler_params=_cparams(2),
        name=name,
    )(a, w)


def _mm_res_kernel(a_ref, w_ref, x_ref, g_ref, o_ref):
    o_ref[0] = x_ref[0] + g_ref[0] * _dot(a_ref[0], w_ref[...])


def _matmul_gated_residual(a, w, x, mod, k_gate, n_ctx, tn, name):
    bsz, t, k = a.shape
    d = w.shape[1]
    nct = n_ctx // ROW_TILE
    return pl.pallas_call(
        _mm_res_kernel,
        grid=(d // tn, bsz, t // ROW_TILE),
        in_specs=[pl.BlockSpec((1, ROW_TILE, k), lambda j, b, i: (b, i, 0)),
                  pl.BlockSpec((k, tn), lambda j, b, i: (0, j)),
                  pl.BlockSpec((1, ROW_TILE, tn), lambda j, b, i: (b, i, j)),
                  pl.BlockSpec((1, 1, tn),
                               lambda j, b, i: _mod_row(b, i, nct, bsz, k_gate)[:2] + (j,))],
        out_specs=pl.BlockSpec((1, ROW_TILE, tn), lambda j, b, i: (b, i, j)),
        out_shape=jax.ShapeDtypeStruct(x.shape, F32),
        input_output_aliases={2: 0},
        compiler_params=_cparams(3),
        name=name,
    )(a, w, x, mod)


def _ffn_conv_kernel(up_ref, um_ref, un_ref, v_ref, w_ref, b_ref, o_ref, *, n_ctx_tiles, n_tiles):
    i = pl.program_id(1)
    is_ctx = i < n_ctx_tiles
    has_up = jnp.logical_and(jnp.logical_not(is_ctx), i > n_ctx_tiles)
    has_dn = jnp.logical_and(jnp.logical_not(is_ctx), i < n_tiles - 1)
    row_on = jnp.where(is_ctx, 0.0, 1.0)
    main = um_ref[0].astype(F32)
    prev = up_ref[0].astype(F32) * jnp.where(has_up, 1.0, 0.0)
    nxt = un_ref[0].astype(F32) * jnp.where(has_dn, 1.0, 0.0)
    up = jnp.concatenate([prev, main[:ROW_TILE - GRID_W]], axis=0)
    dn = jnp.concatenate([main[GRID_W:], nxt], axis=0)
    w = w_ref[...]
    wu = w[0:3] * row_on
    wc = w[3:6]
    wd = w[6:9] * row_on

    def col(kw):
        return up * wu[kw:kw + 1] + main * wc[kw:kw + 1] + dn * wd[kw:kw + 1]

    r = lax.broadcasted_iota(jnp.int32, (ROW_TILE, 1), 0)
    c = jnp.where(is_ctx, r, jnp.bitwise_and(r, GRID_W - 1))
    last_c = jnp.where(is_ctx, ROW_TILE - 1, GRID_W - 1)
    left = jnp.where(c != 0, pltpu.roll(col(0), 1, 0), 0.0)
    right = jnp.where(c != last_c, pltpu.roll(col(2), ROW_TILE - 1, 0), 0.0)
    u = left + col(1) + right + b_ref[...]
    o_ref[0] = (_silu(u) * v_ref[0].astype(F32)).astype(o_ref.dtype)


def _ffn_conv(uv, conv_w, conv_b, n_ctx, tc):
    bsz, t, f2 = uv.shape
    f = f2 // 2
    assert ROW_TILE % GRID_W == 0 and n_ctx == ROW_TILE or n_ctx % ROW_TILE == 0
    nct = n_ctx // ROW_TILE
    nt = t // ROW_TILE
    per = ROW_TILE // GRID_W
    nrow = t // GRID_W
    kern = functools.partial(_ffn_conv_kernel, n_ctx_tiles=nct, n_tiles=nt)
    return pl.pallas_call(
        kern,
        grid=(bsz, nt, f // tc),
        in_specs=[pl.BlockSpec((1, GRID_W, tc), lambda b, i, j: (b, jnp.maximum(i * per - 1, 0), j)),
                  pl.BlockSpec((1, ROW_TILE, tc), lambda b, i, j: (b, i, j)),
                  pl.BlockSpec((1, GRID_W, tc), lambda b, i, j: (b, jnp.minimum(i * per + per, nrow - 1), j)),
                  pl.BlockSpec((1, ROW_TILE, tc), lambda b, i, j: (b, i, j + f // tc)),
                  pl.BlockSpec((9, tc), lambda b, i, j: (0, j)),
                  pl.BlockSpec((1, tc), lambda b, i, j: (0, j))],
        out_specs=pl.BlockSpec((1, ROW_TILE, tc), lambda b, i, j: (b, i, j)),
        out_shape=jax.ShapeDtypeStruct((bsz, t, f), BF16),
        compiler_params=_cparams(3),
        name="ffn_conv",
    )(uv, uv, uv, uv, conv_w.reshape(9, f), conv_b.reshape(1, f))


def _scan_chunks(body_bwd, body_fwd, init, n_ctx_chunks, n_chunks):
    st = lax.fori_loop(0, n_ctx_chunks, lambda i, s: body_bwd(n_ctx_chunks - 1 - i, s), init)
    lax.fori_loop(0, n_chunks - n_ctx_chunks, lambda i, s: body_bwd(n_chunks - 1 - i, s), st)
    lax.fori_loop(0, n_chunks, body_fwd, init)


def _rows(c):
    return pl.ds(pl.multiple_of(c * CHUNK, CHUNK), CHUNK)


def _level_constants():
    idx = np.arange(CHUNK)
    x = idx[:, None] ^ idx[None, :]
    lvl = np.where(x > 0, np.floor(np.log2(np.maximum(x, 1))), 6).astype(np.int32)
    lvl_f = np.where(idx[:, None] >= idx[None, :], lvl, 7).astype(np.int32)
    lvl_b = lvl_f.T.copy()
    def stack(fwd):
        mats = []
        tri = (idx[None, :] <= idx[:, None]) if fwd else (idx[None, :] >= idx[:, None])
        mats.append(tri)
        for m in (32, 16, 8, 4, 2, 1):
            ref = (idx // (2 * m)) * (2 * m) + (m - 1 if fwd else m)
            mats.append(tri[ref])
        return np.concatenate(mats, axis=0).astype(np.float32)
    return lvl_f, lvl_b, stack(True), stack(False)


def _gla_chunk(q, k, v, logf, st, sel, lvl, fwd):
    hi = logf.astype(BF16)
    r1 = logf - hi.astype(F32)
    mid = r1.astype(BF16)
    lo = (r1 - mid.astype(F32)).astype(BF16)
    cs = _dot(sel, jnp.concatenate([hi, mid, lo], axis=1))
    cs = cs[:, :LANES] + (cs[:, LANES:2 * LANES] + cs[:, 2 * LANES:])
    b = cs[:CHUNK]
    scores = jnp.where(lvl == 6, _dot_nt(_bf(q), _bf(k)), 0.0)
    for n in range(6):
        wgt = jnp.exp(-jnp.abs(b - cs[(n + 1) * CHUNK:(n + 2) * CHUNK]))
        scores = jnp.where(lvl == 5 - n, _dot_nt(_bf(q * wgt), _bf(k * wgt)), scores)
    b_tot = b[CHUNK - 1:CHUNK] if fwd else b[0:1]
    out = _dot(_bf(scores), _bf(v)) + _dot_nt(_bf(q * jnp.exp(b)), _bf(st))
    st = st * jnp.exp(b_tot) + _dot_tn(_bf(v), _bf(k * jnp.exp(b_tot - b)))
    return out, st


def _hgrn_kernel(zq_ref, zi_ref, zff_ref, zfb_ref, zg_ref, lbf_ref, lbb_ref, nw_ref,
                 self_ref, selb_ref, lvlf_ref, lvlb_ref, o_ref, *, n_ctx_chunks, n_chunks):
    def chunk(c, st, fwd):
        rows = _rows(c)
        zf_ref, lb_ref, sel_ref, lvl_ref = ((zff_ref, lbf_ref, self_ref, lvlf_ref) if fwd
                                            else (zfb_ref, lbb_ref, selb_ref, lvlb_ref))
        q = _silu(zq_ref[0, rows, :].astype(F32)) * SCALE
        v = zi_ref[0, rows, :].astype(F32)
        zf = zf_ref[0, rows, :].astype(F32)
        lb = lb_ref[...]
        logf = _log_sigmoid(zf) + jnp.log1p(lb * jnp.exp(-jnp.maximum(zf, -GATE_CLAMP)))
        k = (1.0 - lb) * jax.nn.sigmoid(-zf)
        out, st = _gla_chunk(q, k, v, logf, st, sel_ref[...], lvl_ref[...], fwd)
        if fwd:
            o = out + o_ref[0, rows, :].astype(F32)
            o = o * lax.rsqrt(jnp.mean(o * o, axis=-1, keepdims=True) + NORM_EPS) * nw_ref[...]
            o_ref[0, rows, :] = (o * _silu(zg_ref[0, rows, :].astype(F32))).astype(o_ref.dtype)
        else:
            o_ref[0, rows, :] = out.astype(o_ref.dtype)
        return st

    _scan_chunks(lambda c, s: chunk(c, s, False), lambda c, s: chunk(c, s, True),
                 jnp.zeros((HEAD_DIM, HEAD_DIM), F32), n_ctx_chunks, n_chunks)


def _zspec(t, base):
    return pl.BlockSpec((1, t, LANES), lambda b, h: (b, 0, base + h))


def _const_spec(shape):
    return pl.BlockSpec(shape, lambda b, h: (0,) * len(shape))


def _hgrn(z, merged_shape, lb_f, lb_b, norm_w, n_ctx):
    bsz, t, _ = z.shape
    lvl_f, lvl_b, sel_f, sel_b = _level_constants()
    kern = functools.partial(_hgrn_kernel, n_ctx_chunks=n_ctx // CHUNK, n_chunks=t // CHUNK)
    head_vec = pl.BlockSpec((1, LANES), lambda b, h: (0, h))
    return pl.pallas_call(
        kern,
        grid=(bsz, A_HEADS),
        in_specs=[_zspec(t, ZB_AQ), _zspec(t, ZB_AI), _zspec(t, ZB_AFF), _zspec(t, ZB_AFB), _zspec(t, ZB_AG),
                  head_vec, head_vec, _const_spec((1, LANES)),
                  _const_spec(sel_f.shape), _const_spec(sel_b.shape),
                  _const_spec(lvl_f.shape), _const_spec(lvl_b.shape)],
        out_specs=pl.BlockSpec((1, t, LANES), lambda b, h: (b, 0, h)),
        out_shape=jax.ShapeDtypeStruct((bsz, t, A_W), BF16),
        compiler_params=_cparams(2),
        name="hgrn",
    )(z, z, z, z, z, lb_f.reshape(1, A_W), lb_b.reshape(1, A_W), norm_w.reshape(1, LANES),
      jnp.asarray(sel_f, BF16), jnp.asarray(sel_b, BF16), jnp.asarray(lvl_f), jnp.asarray(lvl_b))


def _gdn_kernel(zq_ref, zk_ref, zv_ref, zg_ref, wq_ref, wk_ref, wv_ref, gc_ref, gr_ref, par_ref, nw_ref,
                lvlf_ref, lvlb_ref, o_ref, q_scr, k_scr, v_scr, *, n_ctx_chunks, n_chunks):
    t_total = n_chunks * CHUNK
    ri = lax.broadcasted_iota(jnp.int32, (CHUNK, CHUNK), 0)
    ci = lax.broadcasted_iota(jnp.int32, (CHUNK, CHUNK), 1)
    row_id = lax.broadcasted_iota(jnp.int32, (CHUNK, 1), 0)
    eye = jnp.where(ri == ci, 1.0, 0.0)

    def prep(c, carry):
        rows = _rows(c)
        has_prev = jnp.logical_and(c != 0, c != n_ctx_chunks)
        has_next = jnp.logical_and(c != n_ctx_chunks - 1, c != n_chunks - 1)
        p0 = pl.multiple_of(jnp.maximum(c * CHUNK - 16, 0), 16)
        n0 = pl.multiple_of(jnp.minimum(c * CHUNK + CHUNK, t_total - 16), 16)

        def conv(z_ref, w_ref):
            x = z_ref[0, rows, :].astype(F32)
            xp = z_ref[0, pl.ds(p0, 16), :].astype(F32)[15:16] * jnp.where(has_prev, 1.0, 0.0)
            xn = z_ref[0, pl.ds(n0, 16), :].astype(F32)[0:1] * jnp.where(has_next, 1.0, 0.0)
            xm1 = jnp.where(row_id == 0, xp, pltpu.roll(x, 1, 0))
            xp1 = jnp.where(row_id == CHUNK - 1, xn, pltpu.roll(x, CHUNK - 1, 0))
            w = w_ref[...]
            return _silu(xm1 * w[0:1] + x * w[1:2] + xp1 * w[2:3])

        def l2n(a):
            return a * lax.rsqrt(jnp.sum(a * a, axis=-1, keepdims=True) + NORM_EPS)

        q_scr[rows, :] = l2n(conv(zq_ref, wq_ref)) * SCALE
        k_scr[rows, :] = l2n(conv(zk_ref, wk_ref))
        v_scr[rows, :] = conv(zv_ref, wv_ref)
        return carry

    lax.fori_loop(0, n_chunks, prep, 0)

    def chunk(c, st, fwd):
        rows = _rows(c)
        ia, ib = (0, 2) if fwd else (1, 3)
        par = par_ref[0]
        neg_a = -jnp.exp(par[2 * (1 - int(fwd)):2 * (1 - int(fwd)) + 1, 0:1])
        dt = par[2 * (1 - int(fwd)) + 1:2 * (1 - int(fwd)) + 2, 0:1]
        q = q_scr[rows, :]
        k = k_scr[rows, :]
        v = v_scr[rows, :]
        g_col = neg_a * _softplus(gc_ref[0, 0, rows, ia:ia + 1] + dt)
        g_row = neg_a * _softplus(gr_ref[0, 0, ia, pl.ds(c, 1), :] + dt)
        beta = jax.nn.sigmoid(gc_ref[0, 0, rows, ib:ib + 1])
        incl = (ci <= ri) if fwd else (ci >= ri)
        incl_t = (ri <= ci) if fwd else (ri >= ci)
        bc_col = jnp.sum(jnp.where(incl, g_row, 0.0), axis=1, keepdims=True)
        bc_row = jnp.sum(jnp.where(incl_t, g_col, 0.0), axis=0, keepdims=True)
        lmat = jnp.where(incl, jnp.exp(jnp.minimum(bc_col - bc_row, 0.0)), 0.0)
        kb = _bf(k)
        amat = jnp.where(ri == ci, 0.0, beta * _dot_nt(kb, kb) * lmat)
        lvl = (lvlf_ref if fwd else lvlb_ref)[...]
        x = eye
        for lev in range(6):
            x = x - _dot3(x, _dot3(jnp.where(lvl == lev, amat, 0.0), x))
        e_col = jnp.exp(bc_col)
        kbeta = k * beta
        sol = _dot3(x, jnp.concatenate([v * beta, kbeta * e_col], axis=1))
        u = sol[:, :HEAD_DIM]
        w = sol[:, HEAD_DIM:]
        qk = _dot_nt(_bf(q), kb) * lmat
        bc_tot = bc_col[CHUNK - 1:CHUNK] if fwd else bc_col[0:1]
        stb = _bf(st)
        v_new = u - _dot(_bf(w), stb)
        out = _dot(_bf(q * e_col), stb) + _dot(_bf(qk), _bf(v_new))
        st = jnp.exp(bc_tot) * st + _dot_tn(_bf(k * jnp.exp(bc_tot - bc_col)), _bf(v_new))
        if fwd:
            o = out + o_ref[0, rows, :].astype(F32)
            o = o * lax.rsqrt(jnp.mean(o * o, axis=-1, keepdims=True) + NORM_EPS) * nw_ref[...]
            o_ref[0, rows, :] = (o * _silu(zg_ref[0, rows, :].astype(F32))).astype(o_ref.dtype)
        else:
            o_ref[0, rows, :] = out.astype(o_ref.dtype)
        return st

    _scan_chunks(lambda c, s: chunk(c, s, False), lambda c, s: chunk(c, s, True),
                 jnp.zeros((HEAD_DIM, HEAD_DIM), F32), n_ctx_chunks, n_chunks)


def _gdn(z, zgate, conv_w, a_log_f, a_log_b, dt_f, dt_b, norm_w, n_ctx):
    bsz, t, _ = z.shape
    n = t // CHUNK
    g = zgate[:, :, :N_GATES].reshape(bsz, t, 4, B_HEADS)
    gate_cols = g.transpose(0, 3, 1, 2)
    gate_rows = g.reshape(bsz, n, CHUNK, 4, B_HEADS).transpose(0, 4, 3, 1, 2)
    par = jnp.stack([a_log_f, dt_f, a_log_b, dt_b] + [jnp.zeros_like(dt_f)] * 4, axis=1)
    par = jnp.broadcast_to(par[:, :, None], (B_HEADS, 8, LANES)).astype(F32)
    lvl_f, lvl_b, _, _ = _level_constants()
    kern = functools.partial(_gdn_kernel, n_ctx_chunks=n_ctx // CHUNK, n_chunks=n)

    def wspec(base):
        return pl.BlockSpec((3, LANES), lambda b, h: (0, base + h))

    return pl.pallas_call(
        kern,
        grid=(bsz, B_HEADS),
        in_specs=[_zspec(t, ZB_BQ), _zspec(t, ZB_BK), _zspec(t, ZB_BV), _zspec(t, ZB_BG),
                  wspec(0), wspec(B_HEADS), wspec(2 * B_HEADS),
                  pl.BlockSpec((1, 1, t, 4), lambda b, h: (b, h, 0, 0)),
                  pl.BlockSpec((1, 1, 4, n, CHUNK), lambda b, h: (b, h, 0, 0, 0)),
                  pl.BlockSpec((1, 8, LANES), lambda b, h: (h, 0, 0)),
                  _const_spec((1, LANES)), _const_spec(lvl_f.shape), _const_spec(lvl_b.shape)],
        out_specs=pl.BlockSpec((1, t, LANES), lambda b, h: (b, 0, h)),
        out_shape=jax.ShapeDtypeStruct((bsz, t, B_W), BF16),
        scratch_shapes=[pltpu.VMEM((t, HEAD_DIM), F32)] * 3,
        compiler_params=_cparams(2),
        name="gdn",
    )(z, z, z, z, conv_w, conv_w, conv_w, gate_cols, gate_rows, par, norm_w.reshape(1, LANES),
      jnp.asarray(lvl_f), jnp.asarray(lvl_b))


def _ret_kernel(zq_ref, zk_ref, zv_ref, zg_ref, base_ref, loc_ref, dec_ref, dmat_ref, cdec_ref, nw_ref,
                o_ref, *, n_ctx_chunks, n_chunks):
    def chunk(c, st, fwd):
        rows = _rows(c)
        d = 0 if fwd else 1
        cb = base_ref[2 * d, pl.ds(c, 1), :]
        sb = base_ref[2 * d + 1, pl.ds(c, 1), :]
        cl, sl, cls, sls = (loc_ref[4 * d + n] for n in range(4))
        cos = cb * cl - sb * sl
        sin = sb * cls + cb * sls

        def rot(a):
            return a * cos + pltpu.roll(a, HEAD_DIM // 2, 1) * sin

        q = rot(zq_ref[0, rows, :].astype(F32) * SCALE)
        k = rot(zk_ref[0, rows, :].astype(F32))
        v = _bf(zv_ref[0, rows, :])
        scores = _dot_nt(_bf(q), _bf(k)) * dmat_ref[0, d]
        out = _dot(_bf(scores), v) + _dot_nt(_bf(q * dec_ref[0, 2 * d]), _bf(st))
        st = st * cdec_ref[0, d:d + 1, :] + _dot_tn(v, _bf(k * dec_ref[0, 2 * d + 1]))
        if fwd:
            o = out + o_ref[0, rows, :].astype(F32)
            oc = o - jnp.mean(o, axis=-1, keepdims=True)
            o = oc * lax.rsqrt(jnp.mean(oc * oc, axis=-1, keepdims=True) + NORM_EPS) * nw_ref[...]
            o_ref[0, rows, :] = (o * _silu(zg_ref[0, rows, :].astype(F32))).astype(o_ref.dtype)
        else:
            o_ref[0, rows, :] = out.astype(o_ref.dtype)
        return st

    _scan_chunks(lambda c, s: chunk(c, s, False), lambda c, s: chunk(c, s, True),
                 jnp.zeros((HEAD_DIM, HEAD_DIM), F32), n_ctx_chunks, n_chunks)


def _retention_tables(n_ctx, t):
    half = HEAD_DIM // 2
    n = t // CHUNK
    nc = n_ctx // CHUNK
    inv = ROPE_BASE ** (-jnp.arange(half, dtype=F32) / half)
    inv = jnp.concatenate([inv, inv])
    sgn = jnp.concatenate([-jnp.ones((half,), F32), jnp.ones((half,), F32)])
    c_idx = np.arange(n)
    base_f = (c_idx * CHUNK).astype(np.float32)
    base_b = np.where(c_idx < nc, n_ctx - CHUNK * (c_idx + 1), t - CHUNK * (c_idx - nc + 1)).astype(np.float32)
    loc_f = np.arange(CHUNK, dtype=np.float32)
    loc_b = loc_f[::-1].copy()

    def ang(p):
        return jnp.asarray(p)[:, None] * inv[None, :]

    base = jnp.stack([jnp.cos(ang(base_f)), jnp.sin(ang(base_f)), jnp.cos(ang(base_b)), jnp.sin(ang(base_b))])
    loc = jnp.stack([jnp.cos(ang(loc_f)), jnp.sin(ang(loc_f)), jnp.cos(ang(loc_f)) * sgn, jnp.sin(ang(loc_f)) * sgn,
                     jnp.cos(ang(loc_b)), jnp.sin(ang(loc_b)), jnp.cos(ang(loc_b)) * sgn, jnp.sin(ang(loc_b)) * sgn])
    lg_f = jnp.log1p(-jnp.exp2(-5.0 - jnp.arange(C_HEADS, dtype=F32)))
    lg_b = lg_f[::-1]
    idx = jnp.arange(CHUNK, dtype=F32)
    rel = idx[:, None] - idx[None, :]

    def masked_exp(a, mask):
        return jnp.where(mask, jnp.exp(jnp.where(mask, a, 0.0)), 0.0)

    dmat_f = masked_exp(lg_f[:, None, None] * rel, (rel >= 0)[None])
    dmat_b = masked_exp(-lg_b[:, None, None] * rel, (rel <= 0)[None])
    dq_f = jnp.exp(lg_f[:, None] * (idx + 1.0))
    dk_f = jnp.exp(lg_f[:, None] * (CHUNK - 1.0 - idx))
    dq_b = jnp.exp(lg_b[:, None] * (CHUNK - idx))
    dk_b = jnp.exp(lg_b[:, None] * idx)
    dec = jnp.stack([dq_f, dk_f, dq_b, dk_b], axis=1)
    dec = jnp.broadcast_to(dec[..., None], (C_HEADS, 4, CHUNK, LANES))
    cdec = jnp.stack([jnp.exp(lg_f * CHUNK), jnp.exp(lg_b * CHUNK)] + [jnp.zeros((C_HEADS,), F32)] * 6, axis=1)
    cdec = jnp.broadcast_to(cdec[..., None], (C_HEADS, 8, LANES))
    return base, loc, dec, jnp.stack([dmat_f, dmat_b], axis=1), cdec


def _retention(z, norm_w, n_ctx):
    bsz, t, _ = z.shape
    n = t // CHUNK
    base, loc, dec, dmat, cdec = _retention_tables(n_ctx, t)
    kern = functools.partial(_ret_kernel, n_ctx_chunks=n_ctx // CHUNK, n_chunks=n)
    return pl.pallas_call(
        kern,
        grid=(bsz, C_HEADS),
        in_specs=[_zspec(t, ZB_CQ), _zspec(t, ZB_CK), _zspec(t, ZB_CV), _zspec(t, ZB_CG),
                  _const_spec(base.shape), _const_spec(loc.shape),
                  pl.BlockSpec((1, 4, CHUNK, LANES), lambda b, h: (h, 0, 0, 0)),
                  pl.BlockSpec((1, 2, CHUNK, CHUNK), lambda b, h: (h, 0, 0, 0)),
                  pl.BlockSpec((1, 8, LANES), lambda b, h: (h, 0, 0)),
                  pl.BlockSpec((1, LANES), lambda b, h: (0, h))],
        out_specs=pl.BlockSpec((1, t, LANES), lambda b, h: (b, 0, h)),
        out_shape=jax.ShapeDtypeStruct((bsz, t, C_W), BF16),
        compiler_params=_cparams(2),
        name="retention",
    )(z, z, z, z, base, loc, dec, dmat, cdec, norm_w.reshape(1, C_W))


def _reorder_in_proj(w):
    g0 = 5 * A_W + 4 * B_W
    main = jnp.concatenate([w[:, :g0], w[:, g0 + N_GATES:]], axis=1).astype(BF16)
    gates = jnp.pad(w[:, g0:g0 + N_GATES], ((0, 0), (0, LANES - N_GATES))).astype(BF16)
    return main, gates


def _lower_bounds(logits):
    p = jax.nn.softmax(logits.astype(F32), axis=0)
    return jnp.cumsum(p, axis=0) - p[0]


def kernel(x, c, ctx, c_ctx, w_mod, b_mod, norm1_w, norm2_w, w_in, qkv_conv_w, hgrn_lb_logits_fwd, hgrn_lb_logits_bwd, hgrn_norm_w, gdn_a_log_fwd, gdn_a_log_bwd, gdn_dt_bias_fwd, gdn_dt_bias_bwd, gdn_norm_w, ret_norm_w, w_out, ffn_w_up, ffn_conv_w, ffn_conv_b, ffn_w_down, final_norm_w):
    bsz, n_lat, d = x.shape
    n_ctx = ctx.shape[1]
    depth = w_mod.shape[0]
    t = n_ctx + n_lat
    assert n_ctx % ROW_TILE == 0 and n_lat % ROW_TILE == 0 and bsz < MOD_ROWS
    lb_fwd = _lower_bounds(hgrn_lb_logits_fwd)
    lb_bwd = _lower_bounds(hgrn_lb_logits_bwd)

    rows = jnp.concatenate([c, c_ctx[None, :], jnp.zeros((MOD_ROWS - bsz - 1, d), F32)], axis=0)
    mods = _modulation(rows, w_mod, b_mod)
    xs = jnp.concatenate([ctx, x], axis=1)

    tm = 512
    m_rows = bsz * t
    assert m_rows % tm == 0
    for l in range(depth):
        mod = mods[l].reshape(MOD_ROWS * N_MOD, 1, d)
        w_main, w_gate = _reorder_in_proj(w_in[l])
        h = _norm_mod(xs, norm1_w[l], mod, 1, 0, n_ctx).reshape(m_rows, d)
        z = _matmul(h, w_main, tm, 23 * LANES, BF16, "in_proj").reshape(bsz, t, Z_BLOCKS * LANES)
        zgate = _matmul(h, w_gate, tm, LANES, F32, "in_proj_gates").reshape(bsz, t, LANES)
        oa = _hgrn(z, None, lb_fwd[l], lb_bwd[l], hgrn_norm_w[l], n_ctx)
        ob = _gdn(z, zgate, qkv_conv_w[l], gdn_a_log_fwd[l], gdn_a_log_bwd[l],
                  gdn_dt_bias_fwd[l], gdn_dt_bias_bwd[l], gdn_norm_w[l], n_ctx)
        oc = _retention(z, ret_norm_w[l], n_ctx)
        merged = jnp.concatenate([oa, ob, oc], axis=-1)
        xs = _matmul_gated_residual(merged, w_out[l].astype(BF16), xs, mod, 2, n_ctx, d, "out_proj")

        h = _norm_mod(xs, norm2_w[l], mod, 4, 3, n_ctx).reshape(m_rows, d)
        f2 = ffn_w_up.shape[2]
        uv = _matmul(h, ffn_w_up[l].astype(BF16), tm, f2 // 4, BF16, "ffn_up").reshape(bsz, t, f2)
        act = _ffn_conv(uv, ffn_conv_w[l], ffn_conv_b[l], n_ctx, 512)
        xs = _matmul_gated_residual(act, ffn_w_down[l].astype(BF16), xs, mod, 5, n_ctx, 512, "ffn_down")
    return _final_norm(xs, final_norm_w, n_ctx)
```

```python
import functools
import math

import numpy as np
import jax
import jax.numpy as jnp
from jax import lax
from jax.experimental import pallas as pl
from jax.experimental.pallas import tpu as pltpu

F32 = jnp.float32
BF16 = jnp.bfloat16

HEAD_DIM = 128
N_HEADS = 16
A_HEADS = 5
C_HEADS = 5
B_HEADS = 6
A_W = A_HEADS * HEAD_DIM
B_W = B_HEADS * HEAD_DIM
C_W = C_HEADS * HEAD_DIM
N_GATES = 4 * B_HEADS
CHUNK = 64
GRID_W = 64
ROPE_BASE = 10000.0
NORM_EPS = 1e-6
N_MOD = 6
GATE_CLAMP = 60.0
SCALE = HEAD_DIM ** -0.5
LANES = 128
MOD_ROWS = 8
ROW_TILE = 256
VMEM_LIMIT = 56 * 1024 * 1024
SCAN_GROUP = 4

ZB_AQ, ZB_AI, ZB_AFF, ZB_AFB, ZB_AG = 0, 5, 10, 15, 20
ZB_BQ, ZB_BK, ZB_BV, ZB_BG = 25, 31, 37, 43
ZB_CQ, ZB_CK, ZB_CV, ZB_CG = 49, 54, 59, 64
Z_BLOCKS = 69


def _cparams(n_axes):
    return pltpu.CompilerParams(dimension_semantics=("arbitrary",) * n_axes,
                                vmem_limit_bytes=VMEM_LIMIT)


def _dot(a, b):
    return jnp.dot(a, b, preferred_element_type=F32)


def _dot_nt(a, b):
    return lax.dot_general(a, b, (((1,), (1,)), ((), ())), preferred_element_type=F32)


def _dot_tn(a, b):
    return lax.dot_general(a, b, (((0,), (0,)), ((), ())), preferred_element_type=F32)


def _bf(a):
    return a.astype(BF16)


def _split2(a):
    hi = a.astype(BF16)
    lo = (a - hi.astype(F32)).astype(BF16)
    return hi, lo


def _dot3(a, b):
    ah, al = _split2(a)
    bh, bl = _split2(b)
    return _dot(ah, bh) + (_dot(al, bh) + _dot(ah, bl))


def _silu(a):
    return a * jax.nn.sigmoid(a)


def _softplus(a):
    return jnp.maximum(a, 0.0) + jnp.log1p(jnp.exp(-jnp.abs(a)))


def _log_sigmoid(a):
    return jnp.minimum(a, 0.0) - jnp.log1p(jnp.exp(-jnp.abs(a)))


def _mod_kernel(a_ref, w_ref, b_ref, o_ref):
    a = _silu(a_ref[...])
    o_ref[0] = _dot3(a, w_ref[0]) + b_ref[0]


def _modulation(rows, w_mod, b_mod):
    depth, d, n = w_mod.shape
    tn = 1024
    return pl.pallas_call(
        _mod_kernel,
        grid=(depth, n // tn),
        in_specs=[pl.BlockSpec((MOD_ROWS, d), lambda l, j: (0, 0)),
                  pl.BlockSpec((1, d, tn), lambda l, j: (l, 0, j)),
                  pl.BlockSpec((1, 1, tn), lambda l, j: (l, 0, j))],
        out_specs=pl.BlockSpec((1, MOD_ROWS, tn), lambda l, j: (l, 0, j)),
        out_shape=jax.ShapeDtypeStruct((depth, MOD_ROWS, n), F32),
        compiler_params=_cparams(2),
        name="modulation",
    )(rows, w_mod, b_mod.reshape(depth, 1, n))


def _norm_mod_kernel(x_ref, w_ref, sc_ref, sh_ref, o_ref):
    x = x_ref[0]
    y = x * lax.rsqrt(jnp.mean(x * x, axis=-1, keepdims=True) + NORM_EPS) * w_ref[...]
    o_ref[0] = (y * (1.0 + sc_ref[0]) + sh_ref[0]).astype(o_ref.dtype)


def _mod_row(b, i, n_ctx_tiles, n_batch, k):
    return (jnp.where(i < n_ctx_tiles, n_batch, b) * N_MOD + k, 0, 0)


def _norm_mod(x, w, mod, k_scale, k_shift, n_ctx):
    bsz, t, d = x.shape
    nct = n_ctx // ROW_TILE
    return pl.pallas_call(
        _norm_mod_kernel,
        grid=(bsz, t // ROW_TILE),
        in_specs=[pl.BlockSpec((1, ROW_TILE, d), lambda b, i: (b, i, 0)),
                  pl.BlockSpec((1, d), lambda b, i: (0, 0)),
                  pl.BlockSpec((1, 1, d), lambda b, i: _mod_row(b, i, nct, bsz, k_scale)),
                  pl.BlockSpec((1, 1, d), lambda b, i: _mod_row(b, i, nct, bsz, k_shift))],
        out_specs=pl.BlockSpec((1, ROW_TILE, d), lambda b, i: (b, i, 0)),
        out_shape=jax.ShapeDtypeStruct((bsz, t, d), BF16),
        compiler_params=_cparams(2),
        name="norm_mod",
    )(x, w.reshape(1, d), mod, mod)


def _final_norm_kernel(x_ref, w_ref, o_ref):
    x = x_ref[0]
    o_ref[0] = x * lax.rsqrt(jnp.mean(x * x, axis=-1, keepdims=True) + NORM_EPS) * w_ref[...]


def _final_norm(x, w, n_ctx):
    bsz, t, d = x.shape
    nct = n_ctx // ROW_TILE
    return pl.pallas_call(
        _final_norm_kernel,
        grid=(bsz, (t - n_ctx) // ROW_TILE),
        in_specs=[pl.BlockSpec((1, ROW_TILE, d), lambda b, i: (b, i + nct, 0)),
                  pl.BlockSpec((1, d), lambda b, i: (0, 0))],
        out_specs=pl.BlockSpec((1, ROW_TILE, d), lambda b, i: (b, i, 0)),
        out_shape=jax.ShapeDtypeStruct((bsz, t - n_ctx, d), F32),
        compiler_params=_cparams(2),
        name="final_norm",
    )(x, w.reshape(1, d))


def _mm_kernel(a_ref, w_ref, o_ref):
    o_ref[...] = _dot(a_ref[...], w_ref[...]).astype(o_ref.dtype)


def _matmul(a, w, tm, tn, out_dtype, name):
    m, k = a.shape
    n = w.shape[1]
    return pl.pallas_call(
        _mm_kernel,
        grid=(n // tn, m // tm),
        in_specs=[pl.BlockSpec((tm, k), lambda j, i: (i, 0)),
                  pl.BlockSpec((k, tn), lambda j, i: (0, j))],
        out_specs=pl.BlockSpec((tm, tn), lambda j, i: (i, j)),
        out_shape=jax.ShapeDtypeStruct((m, n), out_dtype),
        compiler_params=_cparams(2),
        name=name,
    )(a, w)


def _mm_res_kernel(a_ref, w_ref, x_ref, g_ref, o_ref):
    o_ref[0] = x_ref[0] + g_ref[0] * _dot(a_ref[0], w_ref[...])


def _matmul_gated_residual(a, w, x, mod, k_gate, n_ctx, tn, name):
    bsz, t, k = a.shape
    d = w.shape[1]
    nct = n_ctx // ROW_TILE
    return pl.pallas_call(
        _mm_res_kernel,
        grid=(d // tn, bsz, t // ROW_TILE),
        in_specs=[pl.BlockSpec((1, ROW_TILE, k), lambda j, b, i: (b, i, 0)),
                  pl.BlockSpec((k, tn), lambda j, b, i: (0, j)),
                  pl.BlockSpec((1, ROW_TILE, tn), lambda j, b, i: (b, i, j)),
                  pl.BlockSpec((1, 1, tn),
                               lambda j, b, i: _mod_row(b, i, nct, bsz, k_gate)[:2] + (j,))],
        out_specs=pl.BlockSpec((1, ROW_TILE, tn), lambda j, b, i: (b, i, j)),
        out_shape=jax.ShapeDtypeStruct(x.shape, F32),
        input_output_aliases={2: 0},
        compiler_params=_cparams(3),
        name=name,
    )(a, w, x, mod)


def _ffn_conv_kernel(up_ref, um_ref, un_ref, v_ref, w_ref, b_ref, o_ref, *, n_ctx_tiles, n_tiles):
    i = pl.program_id(1)
    is_ctx = i < n_ctx_tiles
    has_up = jnp.logical_and(jnp.logical_not(is_ctx), i > n_ctx_tiles)
    has_dn = jnp.logical_and(jnp.logical_not(is_ctx), i < n_tiles - 1)
    row_on = jnp.where(is_ctx, 0.0, 1.0)
    main = um_ref[0].astype(F32)
    prev = up_ref[0].astype(F32) * jnp.where(has_up, 1.0, 0.0)
    nxt = un_ref[0].astype(F32) * jnp.where(has_dn, 1.0, 0.0)
    up = jnp.concatenate([prev, main[:ROW_TILE - GRID_W]], axis=0)
    dn = jnp.concatenate([main[GRID_W:], nxt], axis=0)
    w = w_ref[...]
    wu = w[0:3] * row_on
    wc = w[3:6]
    wd = w[6:9] * row_on

    def col(kw):
        return up * wu[kw:kw + 1] + main * wc[kw:kw + 1] + dn * wd[kw:kw + 1]

    r = lax.broadcasted_iota(jnp.int32, (ROW_TILE, 1), 0)
    c = jnp.where(is_ctx, r, jnp.bitwise_and(r, GRID_W - 1))
    last_c = jnp.where(is_ctx, ROW_TILE - 1, GRID_W - 1)
    left = jnp.where(c != 0, pltpu.roll(col(0), 1, 0), 0.0)
    right = jnp.where(c != last_c, pltpu.roll(col(2), ROW_TILE - 1, 0), 0.0)
    u = left + col(1) + right + b_ref[...]
    o_ref[0] = (_silu(u) * v_ref[0].astype(F32)).astype(o_ref.dtype)


def _ffn_conv(uv, conv_w, conv_b, n_ctx, tc):
    bsz, t, f2 = uv.shape
    f = f2 // 2
    assert ROW_TILE % GRID_W == 0 and n_ctx == ROW_TILE or n_ctx % ROW_TILE == 0
    nct = n_ctx // ROW_TILE
    nt = t // ROW_TILE
    per = ROW_TILE // GRID_W
    nrow = t // GRID_W
    kern = functools.partial(_ffn_conv_kernel, n_ctx_tiles=nct, n_tiles=nt)
    return pl.pallas_call(
        kern,
        grid=(bsz, nt, f // tc),
        in_specs=[pl.BlockSpec((1, GRID_W, tc), lambda b, i, j: (b, jnp.maximum(i * per - 1, 0), j)),
                  pl.BlockSpec((1, ROW_TILE, tc), lambda b, i, j: (b, i, j)),
                  pl.BlockSpec((1, GRID_W, tc), lambda b, i, j: (b, jnp.minimum(i * per + per, nrow - 1), j)),
                  pl.BlockSpec((1, ROW_TILE, tc), lambda b, i, j: (b, i, j + f // tc)),
                  pl.BlockSpec((9, tc), lambda b, i, j: (0, j)),
                  pl.BlockSpec((1, tc), lambda b, i, j: (0, j))],
        out_specs=pl.BlockSpec((1, ROW_TILE, tc), lambda b, i, j: (b, i, j)),
        out_shape=jax.ShapeDtypeStruct((bsz, t, f), BF16),
        compiler_params=_cparams(3),
        name="ffn_conv",
    )(uv, uv, uv, uv, conv_w.reshape(9, f), conv_b.reshape(1, f))


def _scan_both(group_fn, finish_fn, init, n_ctx_chunks, n_chunks, group=SCAN_GROUP):
    assert n_ctx_chunks % group == 0 and n_chunks % group == 0
    ncg = n_ctx_chunks // group

    def body(i, carry):
        top = jnp.where(i < ncg, n_ctx_chunks - group * i, n_chunks - group * (i - ncg))
        items = []
        for j in range(group):
            items.append((i * group + j, True))
            items.append((top - 1 - j, False))
        return group_fn(items, carry)

    lax.fori_loop(0, n_chunks // group, body, (init, init))

    def finish(i, carry):
        for j in range(group):
            finish_fn(i * group + j)
        return carry

    lax.fori_loop(0, n_chunks // group, finish, 0)


def _rows(c):
    return pl.ds(pl.multiple_of(c * CHUNK, CHUNK), CHUNK)


def _level_constants():
    idx = np.arange(CHUNK)
    x = idx[:, None] ^ idx[None, :]
    lvl = np.where(x > 0, np.floor(np.log2(np.maximum(x, 1))), 6).astype(np.int32)
    lvl_f = np.where(idx[:, None] >= idx[None, :], lvl, 7).astype(np.int32)
    lvl_b = lvl_f.T.copy()
    def stack(fwd):
        mats = []
        tri = (idx[None, :] <= idx[:, None]) if fwd else (idx[None, :] >= idx[:, None])
        mats.append(tri)
        for m in (32, 16, 8, 4, 2, 1):
            ref = (idx // (2 * m)) * (2 * m) + (m - 1 if fwd else m)
            mats.append(tri[ref])
        return np.concatenate(mats, axis=0).astype(np.float32)
    return lvl_f, lvl_b, stack(True), stack(False)


class _Item:
    pass


def _hgrn_kernel(zq_ref, zi_ref, zff_ref, zfb_ref, zg_ref, lbf_ref, lbb_ref, nw_ref,
                 self_ref, selb_ref, lvlf_ref, lvlb_ref, o_ref, of_scr, *, n_ctx_chunks, n_chunks):
    def group_fn(items, carry):
        st = {True: carry[0], False: carry[1]}
        its = []
        for c, fwd in items:
            it = _Item()
            it.fwd, it.rows = fwd, _rows(c)
            zf_ref, lb_ref = (zff_ref, lbf_ref) if fwd else (zfb_ref, lbb_ref)
            it.q = _silu(zq_ref[0, it.rows, :].astype(F32)) * SCALE
            it.v = zi_ref[0, it.rows, :]
            zf = zf_ref[0, it.rows, :].astype(F32)
            lb = lb_ref[...]
            logf = _log_sigmoid(zf) + jnp.log1p(lb * jnp.exp(-jnp.maximum(zf, -GATE_CLAMP)))
            it.k = (1.0 - lb) * jax.nn.sigmoid(-zf)
            hi = logf.astype(BF16)
            r1 = logf - hi.astype(F32)
            mid = r1.astype(BF16)
            lo = (r1 - mid.astype(F32)).astype(BF16)
            it.cat = jnp.concatenate([hi, mid, lo], axis=1)
            its.append(it)
        for it in its:
            cs = _dot((self_ref if it.fwd else selb_ref)[...], it.cat)
            it.cs = cs[:, :LANES] + (cs[:, LANES:2 * LANES] + cs[:, 2 * LANES:])
            it.b = it.cs[:CHUNK]
            it.lvl = (lvlf_ref if it.fwd else lvlb_ref)[...]
        for it in its:
            it.scores = jnp.where(it.lvl == 6, _dot_nt(_bf(it.q), _bf(it.k)), 0.0)
        for n in range(6):
            for it in its:
                wgt = jnp.exp(-jnp.abs(it.b - it.cs[(n + 1) * CHUNK:(n + 2) * CHUNK]))
                it.scores = jnp.where(it.lvl == 5 - n, _dot_nt(_bf(it.q * wgt), _bf(it.k * wgt)), it.scores)
        for it in its:
            b_tot = it.b[CHUNK - 1:CHUNK] if it.fwd else it.b[0:1]
            it.intra = _dot(_bf(it.scores), it.v)
            it.upd = _dot_tn(it.v, _bf(it.k * jnp.exp(b_tot - it.b)))
            it.qd = _bf(it.q * jnp.exp(it.b))
            it.dec = jnp.exp(b_tot)
        for it in its:
            out = it.intra + _dot_nt(it.qd, _bf(st[it.fwd]))
            st[it.fwd] = st[it.fwd] * it.dec + it.upd
            if it.fwd:
                of_scr[it.rows, :] = out
            else:
                o_ref[0, it.rows, :] = out.astype(o_ref.dtype)
        return st[True], st[False]

    def finish_fn(c):
        rows = _rows(c)
        o = of_scr[rows, :] + o_ref[0, rows, :].astype(F32)
        o = o * lax.rsqrt(jnp.mean(o * o, axis=-1, keepdims=True) + NORM_EPS) * nw_ref[...]
        o_ref[0, rows, :] = (o * _silu(zg_ref[0, rows, :].astype(F32))).astype(o_ref.dtype)

    _scan_both(group_fn, finish_fn, jnp.zeros((HEAD_DIM, HEAD_DIM), F32), n_ctx_chunks, n_chunks)


def _zspec(t, base):
    return pl.BlockSpec((1, t, LANES), lambda b, h: (b, 0, base + h))


def _const_spec(shape):
    return pl.BlockSpec(shape, lambda b, h: (0,) * len(shape))


def _hgrn(z, lb_f, lb_b, norm_w, n_ctx):
    bsz, t, _ = z.shape
    lvl_f, lvl_b, sel_f, sel_b = _level_constants()
    kern = functools.partial(_hgrn_kernel, n_ctx_chunks=n_ctx // CHUNK, n_chunks=t // CHUNK)
    head_vec = pl.BlockSpec((1, LANES), lambda b, h: (0, h))
    return pl.pallas_call(
        kern,
        grid=(bsz, A_HEADS),
        in_specs=[_zspec(t, ZB_AQ), _zspec(t, ZB_AI), _zspec(t, ZB_AFF), _zspec(t, ZB_AFB), _zspec(t, ZB_AG),
                  head_vec, head_vec, _const_spec((1, LANES)),
                  _const_spec(sel_f.shape), _const_spec(sel_b.shape),
                  _const_spec(lvl_f.shape), _const_spec(lvl_b.shape)],
        out_specs=pl.BlockSpec((1, t, LANES), lambda b, h: (b, 0, h)),
        out_shape=jax.ShapeDtypeStruct((bsz, t, A_W), BF16),
        scratch_shapes=[pltpu.VMEM((t, HEAD_DIM), F32)],
        compiler_params=_cparams(2),
        name="hgrn",
    )(z, z, z, z, z, lb_f.reshape(1, A_W), lb_b.reshape(1, A_W), norm_w.reshape(1, LANES),
      jnp.asarray(sel_f, BF16), jnp.asarray(sel_b, BF16), jnp.asarray(lvl_f), jnp.asarray(lvl_b))


def _gdn_kernel(zq_ref, zk_ref, zv_ref, zg_ref, wq_ref, wk_ref, wv_ref, gc_ref, gr_ref, par_ref, nw_ref,
                lvlf_ref, lvlb_ref, o_ref, q_scr, k_scr, v_scr, of_scr, *, n_ctx_chunks, n_chunks):
    t_total = n_chunks * CHUNK
    ri = lax.broadcasted_iota(jnp.int32, (CHUNK, CHUNK), 0)
    ci = lax.broadcasted_iota(jnp.int32, (CHUNK, CHUNK), 1)
    row_id = lax.broadcasted_iota(jnp.int32, (CHUNK, 1), 0)
    eye = jnp.where(ri == ci, 1.0, 0.0)

    def prep(c, carry):
        rows = _rows(c)
        has_prev = jnp.logical_and(c != 0, c != n_ctx_chunks)
        has_next = jnp.logical_and(c != n_ctx_chunks - 1, c != n_chunks - 1)
        p0 = pl.multiple_of(jnp.maximum(c * CHUNK - 16, 0), 16)
        n0 = pl.multiple_of(jnp.minimum(c * CHUNK + CHUNK, t_total - 16), 16)

        def conv(z_ref, w_ref):
            x = z_ref[0, rows, :].astype(F32)
            xp = z_ref[0, pl.ds(p0, 16), :].astype(F32)[15:16] * jnp.where(has_prev, 1.0, 0.0)
            xn = z_ref[0, pl.ds(n0, 16), :].astype(F32)[0:1] * jnp.where(has_next, 1.0, 0.0)
            xm1 = jnp.where(row_id == 0, xp, pltpu.roll(x, 1, 0))
            xp1 = jnp.where(row_id == CHUNK - 1, xn, pltpu.roll(x, CHUNK - 1, 0))
            w = w_ref[...]
            return _silu(xm1 * w[0:1] + x * w[1:2] + xp1 * w[2:3])

        def l2n(a):
            return a * lax.rsqrt(jnp.sum(a * a, axis=-1, keepdims=True) + NORM_EPS)

        q_scr[rows, :] = (l2n(conv(zq_ref, wq_ref)) * SCALE).astype(q_scr.dtype)
        k_scr[rows, :] = l2n(conv(zk_ref, wk_ref)).astype(k_scr.dtype)
        v_scr[rows, :] = conv(zv_ref, wv_ref).astype(v_scr.dtype)
        return carry

    lax.fori_loop(0, n_chunks, prep, 0)

    def group_fn(items, carry):
        st = {True: carry[0], False: carry[1]}
        par = par_ref[0]
        its = []
        for c, fwd in items:
            it = _Item()
            it.fwd, it.rows = fwd, _rows(c)
            ia, ib, pr = (0, 2, 0) if fwd else (1, 3, 2)
            neg_a = -jnp.exp(par[pr:pr + 1, 0:1])
            dt = par[pr + 1:pr + 2, 0:1]
            it.q = q_scr[it.rows, :]
            it.k = k_scr[it.rows, :]
            it.v = v_scr[it.rows, :].astype(F32)
            g_col = neg_a * _softplus(gc_ref[0, 0, it.rows, ia:ia + 1] + dt)
            g_row = neg_a * _softplus(gr_ref[0, 0, ia, pl.ds(c, 1), :] + dt)
            it.beta = jax.nn.sigmoid(gc_ref[0, 0, it.rows, ib:ib + 1])
            incl = (ci <= ri) if fwd else (ci >= ri)
            incl_t = (ri <= ci) if fwd else (ri >= ci)
            it.bc = jnp.sum(jnp.where(incl, g_row, 0.0), axis=1, keepdims=True)
            bc_row = jnp.sum(jnp.where(incl_t, g_col, 0.0), axis=0, keepdims=True)
            it.lmat = jnp.where(incl, jnp.exp(jnp.minimum(it.bc - bc_row, 0.0)), 0.0)
            it.lvl = (lvlf_ref if fwd else lvlb_ref)[...]
            its.append(it)
        for it in its:
            it.amat = it.beta * _dot_nt(it.k, it.k) * it.lmat
            it.qk = _bf(_dot_nt(it.q, it.k) * it.lmat)
        for it in its:
            it.x = eye - jnp.where(it.lvl == 0, it.amat, 0.0)
        for lev in range(1, 6):
            for it in its:
                it.y = _dot3(jnp.where(it.lvl == lev, it.amat, 0.0), it.x)
            for it in its:
                it.x = it.x - _dot3(it.x, it.y)
        for it in its:
            e_col = jnp.exp(it.bc)
            kf = it.k.astype(F32)
            sol = _dot3(it.x, jnp.concatenate([it.v * it.beta, kf * (it.beta * e_col)], axis=1))
            it.u = sol[:, :HEAD_DIM]
            it.w = _bf(sol[:, HEAD_DIM:])
            bc_tot = it.bc[CHUNK - 1:CHUNK] if it.fwd else it.bc[0:1]
            it.qd = _bf(it.q.astype(F32) * e_col)
            it.kd = _bf(kf * jnp.exp(bc_tot - it.bc))
            it.last = jnp.exp(bc_tot)
        for it in its:
            s = st[it.fwd]
            sb = _bf(s)
            v_new = _bf(it.u - _dot(it.w, sb))
            out = _dot(it.qd, sb) + _dot(it.qk, v_new)
            st[it.fwd] = it.last * s + _dot_tn(it.kd, v_new)
            if it.fwd:
                of_scr[it.rows, :] = out
            else:
                o_ref[0, it.rows, :] = out.astype(o_ref.dtype)
        return st[True], st[False]

    def finish_fn(c):
        rows = _rows(c)
        o = of_scr[rows, :] + o_ref[0, rows, :].astype(F32)
        o = o * lax.rsqrt(jnp.mean(o * o, axis=-1, keepdims=True) + NORM_EPS) * nw_ref[...]
        o_ref[0, rows, :] = (o * _silu(zg_ref[0, rows, :].astype(F32))).astype(o_ref.dtype)

    _scan_both(group_fn, finish_fn, jnp.zeros((HEAD_DIM, HEAD_DIM), F32), n_ctx_chunks, n_chunks)


def _gdn(z, zgate, conv_w, a_log_f, a_log_b, dt_f, dt_b, norm_w, n_ctx):
    bsz, t, _ = z.shape
    n = t // CHUNK
    g = zgate[:, :, :N_GATES].reshape(bsz, t, 4, B_HEADS)
    gate_cols = g.transpose(0, 3, 1, 2)
    gate_rows = g.reshape(bsz, n, CHUNK, 4, B_HEADS).transpose(0, 4, 3, 1, 2)
    par = jnp.stack([a_log_f, dt_f, a_log_b, dt_b] + [jnp.zeros_like(dt_f)] * 4, axis=1)
    par = jnp.broadcast_to(par[:, :, None], (B_HEADS, 8, LANES)).astype(F32)
    lvl_f, lvl_b, _, _ = _level_constants()
    kern = functools.partial(_gdn_kernel, n_ctx_chunks=n_ctx // CHUNK, n_chunks=n)

    def wspec(base):
        return pl.BlockSpec((3, LANES), lambda b, h: (0, base + h))

    return pl.pallas_call(
        kern,
        grid=(bsz, B_HEADS),
        in_specs=[_zspec(t, ZB_BQ), _zspec(t, ZB_BK), _zspec(t, ZB_BV), _zspec(t, ZB_BG),
                  wspec(0), wspec(B_HEADS), wspec(2 * B_HEADS),
                  pl.BlockSpec((1, 1, t, 4), lambda b, h: (b, h, 0, 0)),
                  pl.BlockSpec((1, 1, 4, n, CHUNK), lambda b, h: (b, h, 0, 0, 0)),
                  pl.BlockSpec((1, 8, LANES), lambda b, h: (h, 0, 0)),
                  _const_spec((1, LANES)), _const_spec(lvl_f.shape), _const_spec(lvl_b.shape)],
        out_specs=pl.BlockSpec((1, t, LANES), lambda b, h: (b, 0, h)),
        out_shape=jax.ShapeDtypeStruct((bsz, t, B_W), BF16),
        scratch_shapes=[pltpu.VMEM((t, HEAD_DIM), BF16)] * 3 + [pltpu.VMEM((t, HEAD_DIM), F32)],
        compiler_params=_cparams(2),
        name="gdn",
    )(z, z, z, z, conv_w, conv_w, conv_w, gate_cols, gate_rows, par, norm_w.reshape(1, LANES),
      jnp.asarray(lvl_f), jnp.asarray(lvl_b))


def _ret_kernel(zq_ref, zk_ref, zv_ref, zg_ref, base_ref, loc_ref, dec_ref, dmat_ref, cdec_ref, nw_ref,
                o_ref, of_scr, *, n_ctx_chunks, n_chunks):
    def group_fn(items, carry):
        st = {True: carry[0], False: carry[1]}
        its = []
        for c, fwd in items:
            it = _Item()
            it.fwd, it.rows, it.d = fwd, _rows(c), 0 if fwd else 1
            d = it.d
            cb = base_ref[2 * d, pl.ds(c, 1), :]
            sb = base_ref[2 * d + 1, pl.ds(c, 1), :]
            cl, sl, cls, sls = (loc_ref[4 * d + n] for n in range(4))
            cos = cb * cl - sb * sl
            sin = sb * cls + cb * sls

            def rot(a):
                return a * cos + pltpu.roll(a, HEAD_DIM // 2, 1) * sin

            q = rot(zq_ref[0, it.rows, :].astype(F32) * SCALE)
            k = rot(zk_ref[0, it.rows, :].astype(F32))
            it.q, it.k = _bf(q), _bf(k)
            it.qd = _bf(q * dec_ref[0, 2 * d])
            it.kd = _bf(k * dec_ref[0, 2 * d + 1])
            it.v = zv_ref[0, it.rows, :]
            its.append(it)
        for it in its:
            it.scores = _bf(_dot_nt(it.q, it.k) * dmat_ref[0, it.d])
        for it in its:
            it.intra = _dot(it.scores, it.v)
            it.upd = _dot_tn(it.v, it.kd)
        for it in its:
            out = it.intra + _dot_nt(it.qd, _bf(st[it.fwd]))
            st[it.fwd] = st[it.fwd] * cdec_ref[0, it.d:it.d + 1, :] + it.upd
            if it.fwd:
                of_scr[it.rows, :] = out
            else:
                o_ref[0, it.rows, :] = out.astype(o_ref.dtype)
        return st[True], st[False]

    def finish_fn(c):
        rows = _rows(c)
        o = of_scr[rows, :] + o_ref[0, rows, :].astype(F32)
        oc = o - jnp.mean(o, axis=-1, keepdims=True)
        o = oc * lax.rsqrt(jnp.mean(oc * oc, axis=-1, keepdims=True) + NORM_EPS) * nw_ref[...]
        o_ref[0, rows, :] = (o * _silu(zg_ref[0, rows, :].astype(F32))).astype(o_ref.dtype)

    _scan_both(group_fn, finish_fn, jnp.zeros((HEAD_DIM, HEAD_DIM), F32), n_ctx_chunks, n_chunks)


def _retention_tables(n_ctx, t):
    half = HEAD_DIM // 2
    n = t // CHUNK
    nc = n_ctx // CHUNK
    inv = ROPE_BASE ** (-jnp.arange(half, dtype=F32) / half)
    inv = jnp.concatenate([inv, inv])
    sgn = jnp.concatenate([-jnp.ones((half,), F32), jnp.ones((half,), F32)])
    c_idx = np.arange(n)
    base_f = (c_idx * CHUNK).astype(np.float32)
    base_b = np.where(c_idx < nc, n_ctx - CHUNK * (c_idx + 1), t - CHUNK * (c_idx - nc + 1)).astype(np.float32)
    loc_f = np.arange(CHUNK, dtype=np.float32)
    loc_b = loc_f[::-1].copy()

    def ang(p):
        return jnp.asarray(p)[:, None] * inv[None, :]

    base = jnp.stack([jnp.cos(ang(base_f)), jnp.sin(ang(base_f)), jnp.cos(ang(base_b)), jnp.sin(ang(base_b))])
    loc = jnp.stack([jnp.cos(ang(loc_f)), jnp.sin(ang(loc_f)), jnp.cos(ang(loc_f)) * sgn, jnp.sin(ang(loc_f)) * sgn,
                     jnp.cos(ang(loc_b)), jnp.sin(ang(loc_b)), jnp.cos(ang(loc_b)) * sgn, jnp.sin(ang(loc_b)) * sgn])
    lg_f = jnp.log1p(-jnp.exp2(-5.0 - jnp.arange(C_HEADS, dtype=F32)))
    lg_b = lg_f[::-1]
    idx = jnp.arange(CHUNK, dtype=F32)
    rel = idx[:, None] - idx[None, :]

    def masked_exp(a, mask):
        return jnp.where(mask, jnp.exp(jnp.where(mask, a, 0.0)), 0.0)

    dmat_f = masked_exp(lg_f[:, None, None] * rel, (rel >= 0)[None])
    dmat_b = masked_exp(-lg_b[:, None, None] * rel, (rel <= 0)[None])
    dq_f = jnp.exp(lg_f[:, None] * (idx + 1.0))
    dk_f = jnp.exp(lg_f[:, None] * (CHUNK - 1.0 - idx))
    dq_b = jnp.exp(lg_b[:, None] * (CHUNK - idx))
    dk_b = jnp.exp(lg_b[:, None] * idx)
    dec = jnp.stack([dq_f, dk_f, dq_b, dk_b], axis=1)
    dec = jnp.broadcast_to(dec[..., None], (C_HEADS, 4, CHUNK, LANES))
    cdec = jnp.stack([jnp.exp(lg_f * CHUNK), jnp.exp(lg_b * CHUNK)] + [jnp.zeros((C_HEADS,), F32)] * 6, axis=1)
    cdec = jnp.broadcast_to(cdec[..., None], (C_HEADS, 8, LANES))
    return base, loc, dec, jnp.stack([dmat_f, dmat_b], axis=1), cdec


def _retention(z, norm_w, n_ctx):
    bsz, t, _ = z.shape
    n = t // CHUNK
    base, loc, dec, dmat, cdec = _retention_tables(n_ctx, t)
    kern = functools.partial(_ret_kernel, n_ctx_chunks=n_ctx // CHUNK, n_chunks=n)
    return pl.pallas_call(
        kern,
        grid=(bsz, C_HEADS),
        in_specs=[_zspec(t, ZB_CQ), _zspec(t, ZB_CK), _zspec(t, ZB_CV), _zspec(t, ZB_CG),
                  _const_spec(base.shape), _const_spec(loc.shape),
                  pl.BlockSpec((1, 4, CHUNK, LANES), lambda b, h: (h, 0, 0, 0)),
                  pl.BlockSpec((1, 2, CHUNK, CHUNK), lambda b, h: (h, 0, 0, 0)),
                  pl.BlockSpec((1, 8, LANES), lambda b, h: (h, 0, 0)),
                  pl.BlockSpec((1, LANES), lambda b, h: (0, h))],
        out_specs=pl.BlockSpec((1, t, LANES), lambda b, h: (b, 0, h)),
        out_shape=jax.ShapeDtypeStruct((bsz, t, C_W), BF16),
        scratch_shapes=[pltpu.VMEM((t, HEAD_DIM), F32)],
        compiler_params=_cparams(2),
        name="retention",
    )(z, z, z, z, base, loc, dec, dmat, cdec, norm_w.reshape(1, C_W))


def _reorder_in_proj(w):
    g0 = 5 * A_W + 4 * B_W
    main = jnp.concatenate([w[:, :g0], w[:, g0 + N_GATES:]], axis=1).astype(BF16)
    gates = jnp.pad(w[:, g0:g0 + N_GATES], ((0, 0), (0, LANES - N_GATES))).astype(BF16)
    return main, gates


def _lower_bounds(logits):
    p = jax.nn.softmax(logits.astype(F32), axis=0)
    return jnp.cumsum(p, axis=0) - p[0]


def kernel(x, c, ctx, c_ctx, w_mod, b_mod, norm1_w, norm2_w, w_in, qkv_conv_w, hgrn_lb_logits_fwd, hgrn_lb_logits_bwd, hgrn_norm_w, gdn_a_log_fwd, gdn_a_log_bwd, gdn_dt_bias_fwd, gdn_dt_bias_bwd, gdn_norm_w, ret_norm_w, w_out, ffn_w_up, ffn_conv_w, ffn_conv_b, ffn_w_down, final_norm_w):
    bsz, n_lat, d = x.shape
    n_ctx = ctx.shape[1]
    depth = w_mod.shape[0]
    t = n_ctx + n_lat
    assert n_ctx % ROW_TILE == 0 and n_lat % ROW_TILE == 0 and bsz < MOD_ROWS
    lb_fwd = _lower_bounds(hgrn_lb_logits_fwd)
    lb_bwd = _lower_bounds(hgrn_lb_logits_bwd)

    rows = jnp.concatenate([c, c_ctx[None, :], jnp.zeros((MOD_ROWS - bsz - 1, d), F32)], axis=0)
    mods = _modulation(rows, w_mod, b_mod)
    xs = jnp.concatenate([ctx, x], axis=1)

    tm = 512
    m_rows = bsz * t
    assert m_rows % tm == 0
    for l in range(depth):
        mod = mods[l].reshape(MOD_ROWS * N_MOD, 1, d)
        w_main, w_gate = _reorder_in_proj(w_in[l])
        h = _norm_mod(xs, norm1_w[l], mod, 1, 0, n_ctx).reshape(m_rows, d)
        z = _matmul(h, w_main, tm, 23 * LANES, BF16, "in_proj").reshape(bsz, t, Z_BLOCKS * LANES)
        zgate = _matmul(h, w_gate, tm, LANES, F32, "in_proj_gates").reshape(bsz, t, LANES)
        oa = _hgrn(z, lb_fwd[l], lb_bwd[l], hgrn_norm_w[l], n_ctx)
        ob = _gdn(z, zgate, qkv_conv_w[l], gdn_a_log_fwd[l], gdn_a_log_bwd[l],
                  gdn_dt_bias_fwd[l], gdn_dt_bias_bwd[l], gdn_norm_w[l], n_ctx)
        oc = _retention(z, ret_norm_w[l], n_ctx)
        merged = jnp.concatenate([oa, ob, oc], axis=-1)
        xs = _matmul_gated_residual(merged, w_out[l].astype(BF16), xs, mod, 2, n_ctx, d, "out_proj")

        h = _norm_mod(xs, norm2_w[l], mod, 4, 3, n_ctx).reshape(m_rows, d)
        f2 = ffn_w_up.shape[2]
        uv = _matmul(h, ffn_w_up[l].astype(BF16), tm, f2 // 4, BF16, "ffn_up").reshape(bsz, t, f2)
        act = _ffn_conv(uv, ffn_conv_w[l], ffn_conv_b[l], n_ctx, 512)
        xs = _matmul_gated_residual(act, ffn_w_down[l].astype(BF16), xs, mod, 5, n_ctx, 512, "ffn_down")
    return _final_norm(xs, final_norm_w, n_ctx)
```

```python
import functools
import math

import numpy as np
import jax
import jax.numpy as jnp
from jax import lax
from jax.experimental import pallas as pl
from jax.experimental.pallas import tpu as pltpu

F32 = jnp.float32
BF16 = jnp.bfloat16

HEAD_DIM = 128
N_HEADS = 16
A_HEADS = 5
C_HEADS = 5
B_HEADS = 6
A_W = A_HEADS * HEAD_DIM
B_W = B_HEADS * HEAD_DIM
C_W = C_HEADS * HEAD_DIM
N_GATES = 4 * B_HEADS
CHUNK = 64
GRID_W = 64
ROPE_BASE = 10000.0
NORM_EPS = 1e-6
N_MOD = 6
GATE_CLAMP = 60.0
SCALE = HEAD_DIM ** -0.5
LANES = 128
MOD_ROWS = 8
ROW_TILE = 256
VMEM_LIMIT = 56 * 1024 * 1024
HGRN_GROUP = 4
GDN_GROUP = 12
RET_GROUP = 12
PREP_GROUP = 4

ZB_AQ, ZB_AI, ZB_AFF, ZB_AFB, ZB_AG = 0, 5, 10, 15, 20
ZB_BQ, ZB_BK, ZB_BV, ZB_BG = 25, 31, 37, 43
ZB_CQ, ZB_CK, ZB_CV, ZB_CG = 49, 54, 59, 64
Z_BLOCKS = 69


def _cparams(n_axes):
    return pltpu.CompilerParams(dimension_semantics=("arbitrary",) * n_axes,
                                vmem_limit_bytes=VMEM_LIMIT)


def _dot(a, b):
    return jnp.dot(a, b, preferred_element_type=F32)


def _dot_nt(a, b):
    return lax.dot_general(a, b, (((1,), (1,)), ((), ())), preferred_element_type=F32)


def _dot_tn(a, b):
    return lax.dot_general(a, b, (((0,), (0,)), ((), ())), preferred_element_type=F32)


def _bf(a):
    return a.astype(BF16)


def _split2(a):
    hi = a.astype(BF16)
    lo = (a - hi.astype(F32)).astype(BF16)
    return hi, lo


def _dot3(a, b):
    ah, al = _split2(a)
    bh, bl = _split2(b)
    return _dot(ah, bh) + (_dot(al, bh) + _dot(ah, bl))


def _row_mean(a):
    hi, lo = _split2(a)
    ones = jnp.ones((a.shape[1], a.shape[1]), BF16)
    return (_dot(hi, ones) + _dot(lo, ones)) * (1.0 / a.shape[1])


def _silu(a):
    return a * jax.nn.sigmoid(a)


def _softplus(a):
    return jnp.maximum(a, 0.0) + jnp.log1p(jnp.exp(-jnp.abs(a)))


def _log_sigmoid(a):
    return jnp.minimum(a, 0.0) - jnp.log1p(jnp.exp(-jnp.abs(a)))


def _mod_kernel(a_ref, w_ref, b_ref, o_ref):
    a = _silu(a_ref[...])
    o_ref[0] = _dot3(a, w_ref[0]) + b_ref[0]


def _modulation(rows, w_mod, b_mod):
    depth, d, n = w_mod.shape
    tn = 1024
    return pl.pallas_call(
        _mod_kernel,
        grid=(depth, n // tn),
        in_specs=[pl.BlockSpec((MOD_ROWS, d), lambda l, j: (0, 0)),
                  pl.BlockSpec((1, d, tn), lambda l, j: (l, 0, j)),
                  pl.BlockSpec((1, 1, tn), lambda l, j: (l, 0, j))],
        out_specs=pl.BlockSpec((1, MOD_ROWS, tn), lambda l, j: (l, 0, j)),
        out_shape=jax.ShapeDtypeStruct((depth, MOD_ROWS, n), F32),
        compiler_params=_cparams(2),
        name="modulation",
    )(rows, w_mod, b_mod.reshape(depth, 1, n))


def _norm_mod_kernel(x_ref, w_ref, sc_ref, sh_ref, o_ref):
    x = x_ref[0]
    y = x * lax.rsqrt(jnp.mean(x * x, axis=-1, keepdims=True) + NORM_EPS) * w_ref[...]
    o_ref[0] = (y * (1.0 + sc_ref[0]) + sh_ref[0]).astype(o_ref.dtype)


def _mod_row(b, i, n_ctx_tiles, n_batch, k):
    return (jnp.where(i < n_ctx_tiles, n_batch, b) * N_MOD + k, 0, 0)


def _norm_mod(x, w, mod, k_scale, k_shift, n_ctx):
    bsz, t, d = x.shape
    nct = n_ctx // ROW_TILE
    return pl.pallas_call(
        _norm_mod_kernel,
        grid=(bsz, t // ROW_TILE),
        in_specs=[pl.BlockSpec((1, ROW_TILE, d), lambda b, i: (b, i, 0)),
                  pl.BlockSpec((1, d), lambda b, i: (0, 0)),
                  pl.BlockSpec((1, 1, d), lambda b, i: _mod_row(b, i, nct, bsz, k_scale)),
                  pl.BlockSpec((1, 1, d), lambda b, i: _mod_row(b, i, nct, bsz, k_shift))],
        out_specs=pl.BlockSpec((1, ROW_TILE, d), lambda b, i: (b, i, 0)),
        out_shape=jax.ShapeDtypeStruct((bsz, t, d), BF16),
        compiler_params=_cparams(2),
        name="norm_mod",
    )(x, w.reshape(1, d), mod, mod)


def _final_norm_kernel(x_ref, w_ref, o_ref):
    x = x_ref[0]
    o_ref[0] = x * lax.rsqrt(jnp.mean(x * x, axis=-1, keepdims=True) + NORM_EPS) * w_ref[...]


def _final_norm(x, w, n_ctx):
    bsz, t, d = x.shape
    nct = n_ctx // ROW_TILE
    return pl.pallas_call(
        _final_norm_kernel,
        grid=(bsz, (t - n_ctx) // ROW_TILE),
        in_specs=[pl.BlockSpec((1, ROW_TILE, d), lambda b, i: (b, i + nct, 0)),
                  pl.BlockSpec((1, d), lambda b, i: (0, 0))],
        out_specs=pl.BlockSpec((1, ROW_TILE, d), lambda b, i: (b, i, 0)),
        out_shape=jax.ShapeDtypeStruct((bsz, t - n_ctx, d), F32),
        compiler_params=_cparams(2),
        name="final_norm",
    )(x, w.reshape(1, d))


def _mm_kernel(a_ref, w_ref, o_ref):
    o_ref[...] = _dot(a_ref[...], w_ref[...]).astype(o_ref.dtype)


def _matmul(a, w, tm, tn, out_dtype, name):
    m, k = a.shape
    n = w.shape[1]
    return pl.pallas_call(
        _mm_kernel,
        grid=(n // tn, m // tm),
        in_specs=[pl.BlockSpec((tm, k), lambda j, i: (i, 0)),
                  pl.BlockSpec((k, tn), lambda j, i: (0, j))],
        out_specs=pl.BlockSpec((tm, tn), lambda j, i: (i, j)),
        out_shape=jax.ShapeDtypeStruct((m, n), out_dtype),
        compiler_params=_cparams(2),
        name=name,
    )(a, w)


def _mm_res_kernel(a_ref, w_ref, x_ref, g_ref, o_ref):
    o_ref[0] = x_ref[0] + g_ref[0] * _dot(a_ref[0], w_ref[...])


def _matmul_gated_residual(a, w, x, mod, k_gate, n_ctx, tn, name):
    bsz, t, k = a.shape
    d = w.shape[1]
    nct = n_ctx // ROW_TILE
    return pl.pallas_call(
        _mm_res_kernel,
        grid=(d // tn, bsz, t // ROW_TILE),
        in_specs=[pl.BlockSpec((1, ROW_TILE, k), lambda j, b, i: (b, i, 0)),
                  pl.BlockSpec((k, tn), lambda j, b, i: (0, j)),
                  pl.BlockSpec((1, ROW_TILE, tn), lambda j, b, i: (b, i, j)),
                  pl.BlockSpec((1, 1, tn),
                               lambda j, b, i: _mod_row(b, i, nct, bsz, k_gate)[:2] + (j,))],
        out_specs=pl.BlockSpec((1, ROW_TILE, tn), lambda j, b, i: (b, i, j)),
        out_shape=jax.ShapeDtypeStruct(x.shape, F32),
        input_output_aliases={2: 0},
        compiler_params=_cparams(3),
        name=name,
    )(a, w, x, mod)


def _ffn_tail_kernel(up_ref, um_ref, un_ref, v_ref, cw_ref, cb_ref, wd_ref, x_ref, g_ref, o_ref,
                     *, n_ctx_tiles, n_tiles, tc):
    i = pl.program_id(1)
    is_ctx = i < n_ctx_tiles
    has_up = jnp.logical_and(jnp.logical_not(is_ctx), i > n_ctx_tiles)
    has_dn = jnp.logical_and(jnp.logical_not(is_ctx), i < n_tiles - 1)
    row_on = jnp.where(is_ctx, 0.0, 1.0)
    up_on = jnp.where(has_up, 1.0, 0.0)
    dn_on = jnp.where(has_dn, 1.0, 0.0)
    r = lax.broadcasted_iota(jnp.int32, (ROW_TILE, 1), 0)
    c = jnp.where(is_ctx, r, jnp.bitwise_and(r, GRID_W - 1))
    not_first = c != 0
    not_last = c != jnp.where(is_ctx, ROW_TILE - 1, GRID_W - 1)
    acc = None
    for j in range(um_ref.shape[2] // tc):
        cols = slice(j * tc, (j + 1) * tc)
        main = um_ref[0, :, cols].astype(F32)
        prev = up_ref[0, :, cols].astype(F32) * up_on
        nxt = un_ref[0, :, cols].astype(F32) * dn_on
        up = jnp.concatenate([prev, main[:ROW_TILE - GRID_W]], axis=0)
        dn = jnp.concatenate([main[GRID_W:], nxt], axis=0)
        w = cw_ref[:, cols]
        wu = w[0:3] * row_on
        wc = w[3:6]
        wd = w[6:9] * row_on

        def col(kw):
            return up * wu[kw:kw + 1] + main * wc[kw:kw + 1] + dn * wd[kw:kw + 1]

        left = jnp.where(not_first, pltpu.roll(col(0), 1, 0), 0.0)
        right = jnp.where(not_last, pltpu.roll(col(2), ROW_TILE - 1, 0), 0.0)
        u = left + col(1) + right + cb_ref[:, cols]
        act = (_silu(u) * v_ref[0, :, cols].astype(F32)).astype(BF16)
        part = _dot(act, wd_ref[cols, :])
        acc = part if acc is None else acc + part
    o_ref[0] = x_ref[0] + g_ref[0] * acc


def _ffn_tail(uv, conv_w, conv_b, w_down, x, mod, k_gate, n_ctx, tc):
    bsz, t, f2 = uv.shape
    f = f2 // 2
    d = w_down.shape[1]
    assert n_ctx == ROW_TILE and ROW_TILE % GRID_W == 0 and f % tc == 0
    nct = n_ctx // ROW_TILE
    nt = t // ROW_TILE
    per = ROW_TILE // GRID_W
    nrow = t // GRID_W
    kern = functools.partial(_ffn_tail_kernel, n_ctx_tiles=nct, n_tiles=nt, tc=tc)
    return pl.pallas_call(
        kern,
        grid=(bsz, nt),
        in_specs=[pl.BlockSpec((1, GRID_W, f), lambda b, i: (b, jnp.maximum(i * per - 1, 0), 0)),
                  pl.BlockSpec((1, ROW_TILE, f), lambda b, i: (b, i, 0)),
                  pl.BlockSpec((1, GRID_W, f), lambda b, i: (b, jnp.minimum(i * per + per, nrow - 1), 0)),
                  pl.BlockSpec((1, ROW_TILE, f), lambda b, i: (b, i, 1)),
                  pl.BlockSpec((9, f), lambda b, i: (0, 0)),
                  pl.BlockSpec((1, f), lambda b, i: (0, 0)),
                  pl.BlockSpec((f, d), lambda b, i: (0, 0), pipeline_mode=pl.Buffered(1)),
                  pl.BlockSpec((1, ROW_TILE, d), lambda b, i: (b, i, 0)),
                  pl.BlockSpec((1, 1, d), lambda b, i: _mod_row(b, i, nct, bsz, k_gate))],
        out_specs=pl.BlockSpec((1, ROW_TILE, d), lambda b, i: (b, i, 0)),
        out_shape=jax.ShapeDtypeStruct(x.shape, F32),
        input_output_aliases={7: 0},
        compiler_params=_cparams(2),
        name="ffn_tail",
    )(uv, uv, uv, uv, conv_w.reshape(9, f), conv_b.reshape(1, f), w_down, x, mod)


def _scan_both(group_fn, finish_fn, init, n_ctx_chunks, n_chunks, group):
    assert n_chunks % group == 0

    def body(i, carry):
        items = []
        for j in range(group):
            p = i * group + j
            items.append((p, True))
            items.append((jnp.where(p < n_ctx_chunks, n_ctx_chunks - 1 - p, n_chunks + n_ctx_chunks - 1 - p), False))
        return group_fn(items, carry)

    lax.fori_loop(0, n_chunks // group, body, (init, init))

    def finish(i, carry):
        finish_fn([_rows(i * group + j) for j in range(group)])
        return carry

    lax.fori_loop(0, n_chunks // group, finish, 0)


def _rows(c):
    return pl.ds(pl.multiple_of(c * CHUNK, CHUNK), CHUNK)


def _finish_heads(rows_list, o_ref, of_scr, zg_ref, nw_ref, centre):
    outs = [of_scr[r, :] + o_ref[0, r, :].astype(F32) for r in rows_list]
    if centre:
        mus = [_row_mean(o) for o in outs]
        outs = [o - mu for o, mu in zip(outs, mus)]
    mss = [_row_mean(o * o) for o in outs]
    for r, o, ms in zip(rows_list, outs, mss):
        o = o * lax.rsqrt(ms + NORM_EPS) * nw_ref[...]
        o_ref[0, r, :] = (o * _silu(zg_ref[0, r, :].astype(F32))).astype(o_ref.dtype)


def _level_constants():
    idx = np.arange(CHUNK)
    x = idx[:, None] ^ idx[None, :]
    lvl = np.where(x > 0, np.floor(np.log2(np.maximum(x, 1))), 6).astype(np.int32)
    lvl_f = np.where(idx[:, None] >= idx[None, :], lvl, 7).astype(np.int32)
    lvl_b = lvl_f.T.copy()
    def stack(fwd):
        mats = []
        tri = (idx[None, :] <= idx[:, None]) if fwd else (idx[None, :] >= idx[:, None])
        mats.append(tri)
        for m in (32, 16, 8, 4, 2, 1):
            ref = (idx // (2 * m)) * (2 * m) + (m - 1 if fwd else m)
            mats.append(tri[ref])
        return np.concatenate(mats, axis=0).astype(np.float32)
    return lvl_f, lvl_b, stack(True), stack(False)


class _Item:
    pass


def _hgrn_kernel(zq_ref, zi_ref, zff_ref, zfb_ref, zg_ref, lbf_ref, lbb_ref, nw_ref,
                 self_ref, selb_ref, lvlf_ref, lvlb_ref, o_ref, of_scr, *, n_ctx_chunks, n_chunks):
    def group_fn(items, carry):
        st = {True: carry[0], False: carry[1]}
        its = []
        for c, fwd in items:
            it = _Item()
            it.fwd, it.rows = fwd, _rows(c)
            zf_ref, lb_ref = (zff_ref, lbf_ref) if fwd else (zfb_ref, lbb_ref)
            it.q = _silu(zq_ref[0, it.rows, :].astype(F32)) * SCALE
            it.v = zi_ref[0, it.rows, :]
            zf = zf_ref[0, it.rows, :].astype(F32)
            lb = lb_ref[...]
            logf = _log_sigmoid(zf) + jnp.log1p(lb * jnp.exp(-jnp.maximum(zf, -GATE_CLAMP)))
            it.k = (1.0 - lb) * jax.nn.sigmoid(-zf)
            hi = logf.astype(BF16)
            r1 = logf - hi.astype(F32)
            mid = r1.astype(BF16)
            lo = (r1 - mid.astype(F32)).astype(BF16)
            it.cat = jnp.concatenate([hi, mid, lo], axis=1)
            its.append(it)
        for it in its:
            cs = _dot((self_ref if it.fwd else selb_ref)[...], it.cat)
            it.cs = cs[:, :LANES] + (cs[:, LANES:2 * LANES] + cs[:, 2 * LANES:])
            it.b = it.cs[:CHUNK]
            it.lvl = (lvlf_ref if it.fwd else lvlb_ref)[...]
        for it in its:
            it.scores = jnp.where(it.lvl == 6, _dot_nt(_bf(it.q), _bf(it.k)), 0.0)
        for n in range(6):
            for it in its:
                wgt = jnp.exp(-jnp.abs(it.b - it.cs[(n + 1) * CHUNK:(n + 2) * CHUNK]))
                it.scores = jnp.where(it.lvl == 5 - n, _dot_nt(_bf(it.q * wgt), _bf(it.k * wgt)), it.scores)
        for it in its:
            b_tot = it.b[CHUNK - 1:CHUNK] if it.fwd else it.b[0:1]
            it.intra = _dot(_bf(it.scores), it.v)
            it.upd = _dot_tn(it.v, _bf(it.k * jnp.exp(b_tot - it.b)))
            it.qd = _bf(it.q * jnp.exp(it.b))
            it.dec = jnp.exp(b_tot)
        for it in its:
            out = it.intra + _dot_nt(it.qd, _bf(st[it.fwd]))
            st[it.fwd] = st[it.fwd] * it.dec + it.upd
            if it.fwd:
                of_scr[it.rows, :] = out
            else:
                o_ref[0, it.rows, :] = out.astype(o_ref.dtype)
        return st[True], st[False]

    def finish_fn(rows_list):
        _finish_heads(rows_list, o_ref, of_scr, zg_ref, nw_ref, centre=False)

    _scan_both(group_fn, finish_fn, jnp.zeros((HEAD_DIM, HEAD_DIM), F32), n_ctx_chunks, n_chunks, HGRN_GROUP)


def _zspec(t, base):
    return pl.BlockSpec((1, t, LANES), lambda b, h: (b, 0, base + h))


def _const_spec(shape):
    return pl.BlockSpec(shape, lambda b, h: (0,) * len(shape))


def _hgrn(z, lb_f, lb_b, norm_w, n_ctx):
    bsz, t, _ = z.shape
    lvl_f, lvl_b, sel_f, sel_b = _level_constants()
    kern = functools.partial(_hgrn_kernel, n_ctx_chunks=n_ctx // CHUNK, n_chunks=t // CHUNK)
    head_vec = pl.BlockSpec((1, LANES), lambda b, h: (0, h))
    return pl.pallas_call(
        kern,
        grid=(bsz, A_HEADS),
        in_specs=[_zspec(t, ZB_AQ), _zspec(t, ZB_AI), _zspec(t, ZB_AFF), _zspec(t, ZB_AFB), _zspec(t, ZB_AG),
                  head_vec, head_vec, _const_spec((1, LANES)),
                  _const_spec(sel_f.shape), _const_spec(sel_b.shape),
                  _const_spec(lvl_f.shape), _const_spec(lvl_b.shape)],
        out_specs=pl.BlockSpec((1, t, LANES), lambda b, h: (b, 0, h)),
        out_shape=jax.ShapeDtypeStruct((bsz, t, A_W), BF16),
        scratch_shapes=[pltpu.VMEM((t, HEAD_DIM), F32)],
        compiler_params=_cparams(2),
        name="hgrn",
    )(z, z, z, z, z, lb_f.reshape(1, A_W), lb_b.reshape(1, A_W), norm_w.reshape(1, LANES),
      jnp.asarray(sel_f, BF16), jnp.asarray(sel_b, BF16), jnp.asarray(lvl_f), jnp.asarray(lvl_b))


def _gdn_kernel(zq_ref, zk_ref, zv_ref, zg_ref, wq_ref, wk_ref, wv_ref, gc_ref, gr_ref, par_ref, nw_ref,
                lvlf_ref, lvlb_ref, o_ref, q_scr, k_scr, v_scr, of_scr, *, n_ctx_chunks, n_chunks):
    t_total = n_chunks * CHUNK
    ri = lax.broadcasted_iota(jnp.int32, (CHUNK, CHUNK), 0)
    ci = lax.broadcasted_iota(jnp.int32, (CHUNK, CHUNK), 1)
    row_id = lax.broadcasted_iota(jnp.int32, (CHUNK, 1), 0)
    eye = jnp.where(ri == ci, 1.0, 0.0)

    def conv(z_ref, w_ref, c):
        has_prev = jnp.logical_and(c != 0, c != n_ctx_chunks)
        has_next = jnp.logical_and(c != n_ctx_chunks - 1, c != n_chunks - 1)
        p0 = pl.multiple_of(jnp.maximum(c * CHUNK - 16, 0), 16)
        n0 = pl.multiple_of(jnp.minimum(c * CHUNK + CHUNK, t_total - 16), 16)
        x = z_ref[0, _rows(c), :].astype(F32)
        xp = z_ref[0, pl.ds(p0, 16), :].astype(F32)[15:16] * jnp.where(has_prev, 1.0, 0.0)
        xn = z_ref[0, pl.ds(n0, 16), :].astype(F32)[0:1] * jnp.where(has_next, 1.0, 0.0)
        xm1 = jnp.where(row_id == 0, xp, pltpu.roll(x, 1, 0))
        xp1 = jnp.where(row_id == CHUNK - 1, xn, pltpu.roll(x, CHUNK - 1, 0))
        w = w_ref[...]
        return _silu(xm1 * w[0:1] + x * w[1:2] + xp1 * w[2:3])

    def prep(i, carry):
        cs = [i * PREP_GROUP + j for j in range(PREP_GROUP)]
        qs = [conv(zq_ref, wq_ref, c) for c in cs]
        ks = [conv(zk_ref, wk_ref, c) for c in cs]
        q_ss = [_row_mean(a * a) * HEAD_DIM for a in qs]
        k_ss = [_row_mean(a * a) * HEAD_DIM for a in ks]
        for c, q, k, qn, kn in zip(cs, qs, ks, q_ss, k_ss):
            q_scr[_rows(c), :] = (q * (lax.rsqrt(qn + NORM_EPS) * SCALE)).astype(q_scr.dtype)
            k_scr[_rows(c), :] = (k * lax.rsqrt(kn + NORM_EPS)).astype(k_scr.dtype)
            v_scr[_rows(c), :] = conv(zv_ref, wv_ref, c).astype(v_scr.dtype)
        return carry

    lax.fori_loop(0, n_chunks // PREP_GROUP, prep, 0)

    def group_fn(items, carry):
        st = {True: carry[0], False: carry[1]}
        par = par_ref[0]
        its = []
        for c, fwd in items:
            it = _Item()
            it.fwd, it.rows = fwd, _rows(c)
            ia, ib, pr = (0, 2, 0) if fwd else (1, 3, 2)
            neg_a = -jnp.exp(par[pr:pr + 1, 0:1])
            dt = par[pr + 1:pr + 2, 0:1]
            it.q = q_scr[it.rows, :]
            it.k = k_scr[it.rows, :]
            it.v = v_scr[it.rows, :].astype(F32)
            g_col = neg_a * _softplus(gc_ref[0, 0, it.rows, ia:ia + 1] + dt)
            g_row = neg_a * _softplus(gr_ref[0, 0, ia, pl.ds(c, 1), :] + dt)
            it.beta = jax.nn.sigmoid(gc_ref[0, 0, it.rows, ib:ib + 1])
            incl = (ci <= ri) if fwd else (ci >= ri)
            incl_t = (ri <= ci) if fwd else (ri >= ci)
            it.bc = jnp.sum(jnp.where(incl, g_row, 0.0), axis=1, keepdims=True)
            bc_row = jnp.sum(jnp.where(incl_t, g_col, 0.0), axis=0, keepdims=True)
            it.lmat = jnp.where(incl, jnp.exp(jnp.minimum(it.bc - bc_row, 0.0)), 0.0)
            it.lvl = (lvlf_ref if fwd else lvlb_ref)[...]
            its.append(it)
        for it in its:
            it.amat = it.beta * _dot_nt(it.k, it.k) * it.lmat
            it.qk = _bf(_dot_nt(it.q, it.k) * it.lmat)
        for it in its:
            it.x = eye - jnp.where(it.lvl == 0, it.amat, 0.0)
        for lev in range(1, 6):
            for it in its:
                it.xb = _bf(it.x)
                it.y = _dot(_bf(jnp.where(it.lvl == lev, it.amat, 0.0)), it.xb)
            for it in its:
                it.x = it.x - _dot(it.xb, _bf(it.y))
        for it in its:
            e_col = jnp.exp(it.bc)
            kf = it.k.astype(F32)
            sol = _dot(_bf(it.x), _bf(jnp.concatenate([it.v * it.beta, kf * (it.beta * e_col)], axis=1)))
            it.u = _bf(sol[:, :HEAD_DIM])
            it.w = _bf(sol[:, HEAD_DIM:])
            bc_tot = it.bc[CHUNK - 1:CHUNK] if it.fwd else it.bc[0:1]
            it.qd = it.q.astype(F32) * e_col
            it.kd = _bf(kf * jnp.exp(bc_tot - it.bc))
            it.last = jnp.exp(bc_tot)
        for it in its:
            it.kw = _bf(_dot_tn(it.kd, it.w))
            it.ku = _dot_tn(it.kd, it.u)
            it.qs = _bf(it.qd - _dot(it.qk, it.w))
            it.o0 = _dot(it.qk, it.u)
        for it in its:
            s = st[it.fwd]
            sb = _bf(s)
            out = it.o0 + _dot(it.qs, sb)
            st[it.fwd] = it.last * s + (it.ku - _dot(it.kw, sb))
            if it.fwd:
                of_scr[it.rows, :] = out
            else:
                o_ref[0, it.rows, :] = out.astype(o_ref.dtype)
        return st[True], st[False]

    def finish_fn(rows_list):
        _finish_heads(rows_list, o_ref, of_scr, zg_ref, nw_ref, centre=False)

    _scan_both(group_fn, finish_fn, jnp.zeros((HEAD_DIM, HEAD_DIM), F32), n_ctx_chunks, n_chunks, GDN_GROUP)


def _gdn(z, zgate, conv_w, a_log_f, a_log_b, dt_f, dt_b, norm_w, n_ctx):
    bsz, t, _ = z.shape
    n = t // CHUNK
    g = zgate[:, :, :N_GATES].reshape(bsz, t, 4, B_HEADS)
    gate_cols = g.transpose(0, 3, 1, 2)
    gate_rows = g.reshape(bsz, n, CHUNK, 4, B_HEADS).transpose(0, 4, 3, 1, 2)
    par = jnp.stack([a_log_f, dt_f, a_log_b, dt_b] + [jnp.zeros_like(dt_f)] * 4, axis=1)
    par = jnp.broadcast_to(par[:, :, None], (B_HEADS, 8, LANES)).astype(F32)
    lvl_f, lvl_b, _, _ = _level_constants()
    kern = functools.partial(_gdn_kernel, n_ctx_chunks=n_ctx // CHUNK, n_chunks=n)

    def wspec(base):
        return pl.BlockSpec((3, LANES), lambda b, h: (0, base + h))

    return pl.pallas_call(
        kern,
        grid=(bsz, B_HEADS),
        in_specs=[_zspec(t, ZB_BQ), _zspec(t, ZB_BK), _zspec(t, ZB_BV), _zspec(t, ZB_BG),
                  wspec(0), wspec(B_HEADS), wspec(2 * B_HEADS),
                  pl.BlockSpec((1, 1, t, 4), lambda b, h: (b, h, 0, 0)),
                  pl.BlockSpec((1, 1, 4, n, CHUNK), lambda b, h: (b, h, 0, 0, 0)),
                  pl.BlockSpec((1, 8, LANES), lambda b, h: (h, 0, 0)),
                  _const_spec((1, LANES)), _const_spec(lvl_f.shape), _const_spec(lvl_b.shape)],
        out_specs=pl.BlockSpec((1, t, LANES), lambda b, h: (b, 0, h)),
        out_shape=jax.ShapeDtypeStruct((bsz, t, B_W), BF16),
        scratch_shapes=[pltpu.VMEM((t, HEAD_DIM), BF16)] * 3 + [pltpu.VMEM((t, HEAD_DIM), F32)],
        compiler_params=_cparams(2),
        name="gdn",
    )(z, z, z, z, conv_w, conv_w, conv_w, gate_cols, gate_rows, par, norm_w.reshape(1, LANES),
      jnp.asarray(lvl_f), jnp.asarray(lvl_b))


def _ret_kernel(zq_ref, zk_ref, zv_ref, zg_ref, base_ref, loc_ref, dec_ref, dmat_ref, cdec_ref, nw_ref,
                o_ref, of_scr, *, n_ctx_chunks, n_chunks):
    def group_fn(items, carry):
        st = {True: carry[0], False: carry[1]}
        its = []
        for c, fwd in items:
            it = _Item()
            it.fwd, it.rows, it.d = fwd, _rows(c), 0 if fwd else 1
            d = it.d
            cb = base_ref[2 * d, pl.ds(c, 1), :]
            sb = base_ref[2 * d + 1, pl.ds(c, 1), :]
            cl, sl, cls, sls = (loc_ref[4 * d + n] for n in range(4))
            cos = cb * cl - sb * sl
            sin = sb * cls + cb * sls

            def rot(a):
                return a * cos + pltpu.roll(a, HEAD_DIM // 2, 1) * sin

            q = rot(zq_ref[0, it.rows, :].astype(F32) * SCALE)
            k = rot(zk_ref[0, it.rows, :].astype(F32))
            it.q, it.k = _bf(q), _bf(k)
            it.qd = _bf(q * dec_ref[0, 2 * d])
            it.kd = _bf(k * dec_ref[0, 2 * d + 1])
            it.v = zv_ref[0, it.rows, :]
            its.append(it)
        for it in its:
            it.scores = _bf(_dot_nt(it.q, it.k) * dmat_ref[0, it.d])
        for it in its:
            it.intra = _dot(it.scores, it.v)
            it.upd = _dot_tn(it.v, it.kd)
        for it in its:
            out = it.intra + _dot_nt(it.qd, _bf(st[it.fwd]))
            st[it.fwd] = st[it.fwd] * cdec_ref[0, it.d:it.d + 1, :] + it.upd
            if it.fwd:
                of_scr[it.rows, :] = out
            else:
                o_ref[0, it.rows, :] = out.astype(o_ref.dtype)
        return st[True], st[False]

    def finish_fn(rows_list):
        _finish_heads(rows_list, o_ref, of_scr, zg_ref, nw_ref, centre=True)

    _scan_both(group_fn, finish_fn, jnp.zeros((HEAD_DIM, HEAD_DIM), F32), n_ctx_chunks, n_chunks, RET_GROUP)


def _retention_tables(n_ctx, t):
    half = HEAD_DIM // 2
    n = t // CHUNK
    nc = n_ctx // CHUNK
    inv = ROPE_BASE ** (-jnp.arange(half, dtype=F32) / half)
    inv = jnp.concatenate([inv, inv])
    sgn = jnp.concatenate([-jnp.ones((half,), F32), jnp.ones((half,), F32)])
    c_idx = np.arange(n)
    base_f = (c_idx * CHUNK).astype(np.float32)
    base_b = np.where(c_idx < nc, n_ctx - CHUNK * (c_idx + 1), t - CHUNK * (c_idx - nc + 1)).astype(np.float32)
    loc_f = np.arange(CHUNK, dtype=np.float32)
    loc_b = loc_f[::-1].copy()

    def ang(p):
        return jnp.asarray(p)[:, None] * inv[None, :]

    base = jnp.stack([jnp.cos(ang(base_f)), jnp.sin(ang(base_f)), jnp.cos(ang(base_b)), jnp.sin(ang(base_b))])
    loc = jnp.stack([jnp.cos(ang(loc_f)), jnp.sin(ang(loc_f)), jnp.cos(ang(loc_f)) * sgn, jnp.sin(ang(loc_f)) * sgn,
                     jnp.cos(ang(loc_b)), jnp.sin(ang(loc_b)), jnp.cos(ang(loc_b)) * sgn, jnp.sin(ang(loc_b)) * sgn])
    lg_f = jnp.log1p(-jnp.exp2(-5.0 - jnp.arange(C_HEADS, dtype=F32)))
    lg_b = lg_f[::-1]
    idx = jnp.arange(CHUNK, dtype=F32)
    rel = idx[:, None] - idx[None, :]

    def masked_exp(a, mask):
        return jnp.where(mask, jnp.exp(jnp.where(mask, a, 0.0)), 0.0)

    dmat_f = masked_exp(lg_f[:, None, None] * rel, (rel >= 0)[None])
    dmat_b = masked_exp(-lg_b[:, None, None] * rel, (rel <= 0)[None])
    dq_f = jnp.exp(lg_f[:, None] * (idx + 1.0))
    dk_f = jnp.exp(lg_f[:, None] * (CHUNK - 1.0 - idx))
    dq_b = jnp.exp(lg_b[:, None] * (CHUNK - idx))
    dk_b = jnp.exp(lg_b[:, None] * idx)
    dec = jnp.stack([dq_f, dk_f, dq_b, dk_b], axis=1)
    dec = jnp.broadcast_to(dec[..., None], (C_HEADS, 4, CHUNK, LANES))
    cdec = jnp.stack([jnp.exp(lg_f * CHUNK), jnp.exp(lg_b * CHUNK)] + [jnp.zeros((C_HEADS,), F32)] * 6, axis=1)
    cdec = jnp.broadcast_to(cdec[..., None], (C_HEADS, 8, LANES))
    return base, loc, dec, jnp.stack([dmat_f, dmat_b], axis=1), cdec


def _retention(z, norm_w, n_ctx):
    bsz, t, _ = z.shape
    n = t // CHUNK
    base, loc, dec, dmat, cdec = _retention_tables(n_ctx, t)
    kern = functools.partial(_ret_kernel, n_ctx_chunks=n_ctx // CHUNK, n_chunks=n)
    return pl.pallas_call(
        kern,
        grid=(bsz, C_HEADS),
        in_specs=[_zspec(t, ZB_CQ), _zspec(t, ZB_CK), _zspec(t, ZB_CV), _zspec(t, ZB_CG),
                  _const_spec(base.shape), _const_spec(loc.shape),
                  pl.BlockSpec((1, 4, CHUNK, LANES), lambda b, h: (h, 0, 0, 0)),
                  pl.BlockSpec((1, 2, CHUNK, CHUNK), lambda b, h: (h, 0, 0, 0)),
                  pl.BlockSpec((1, 8, LANES), lambda b, h: (h, 0, 0)),
                  pl.BlockSpec((1, LANES), lambda b, h: (0, h))],
        out_specs=pl.BlockSpec((1, t, LANES), lambda b, h: (b, 0, h)),
        out_shape=jax.ShapeDtypeStruct((bsz, t, C_W), BF16),
        scratch_shapes=[pltpu.VMEM((t, HEAD_DIM), F32)],
        compiler_params=_cparams(2),
        name="retention",
    )(z, z, z, z, base, loc, dec, dmat, cdec, norm_w.reshape(1, C_W))


def _reorder_in_proj(w):
    g0 = 5 * A_W + 4 * B_W
    main = jnp.concatenate([w[:, :g0], w[:, g0 + N_GATES:]], axis=1).astype(BF16)
    gates = jnp.pad(w[:, g0:g0 + N_GATES], ((0, 0), (0, LANES - N_GATES))).astype(BF16)
    return main, gates


def _lower_bounds(logits):
    p = jax.nn.softmax(logits.astype(F32), axis=0)
    return jnp.cumsum(p, axis=0) - p[0]


def kernel(x, c, ctx, c_ctx, w_mod, b_mod, norm1_w, norm2_w, w_in, qkv_conv_w, hgrn_lb_logits_fwd, hgrn_lb_logits_bwd, hgrn_norm_w, gdn_a_log_fwd, gdn_a_log_bwd, gdn_dt_bias_fwd, gdn_dt_bias_bwd, gdn_norm_w, ret_norm_w, w_out, ffn_w_up, ffn_conv_w, ffn_conv_b, ffn_w_down, final_norm_w):
    bsz, n_lat, d = x.shape
    n_ctx = ctx.shape[1]
    depth = w_mod.shape[0]
    t = n_ctx + n_lat
    assert n_ctx % ROW_TILE == 0 and n_lat % ROW_TILE == 0 and bsz < MOD_ROWS
    lb_fwd = _lower_bounds(hgrn_lb_logits_fwd)
    lb_bwd = _lower_bounds(hgrn_lb_logits_bwd)

    rows = jnp.concatenate([c, c_ctx[None, :], jnp.zeros((MOD_ROWS - bsz - 1, d), F32)], axis=0)
    mods = _modulation(rows, w_mod, b_mod)
    xs = jnp.concatenate([ctx, x], axis=1)

    tm = 512
    m_rows = bsz * t
    assert m_rows % tm == 0
    for l in range(depth):
        mod = mods[l].reshape(MOD_ROWS * N_MOD, 1, d)
        w_main, w_gate = _reorder_in_proj(w_in[l])
        h = _norm_mod(xs, norm1_w[l], mod, 1, 0, n_ctx).reshape(m_rows, d)
        z = _matmul(h, w_main, tm, 23 * LANES, BF16, "in_proj").reshape(bsz, t, Z_BLOCKS * LANES)
        zgate = _matmul(h, w_gate, tm, LANES, F32, "in_proj_gates").reshape(bsz, t, LANES)
        oa = _hgrn(z, lb_fwd[l], lb_bwd[l], hgrn_norm_w[l], n_ctx)
        ob = _gdn(z, zgate, qkv_conv_w[l], gdn_a_log_fwd[l], gdn_a_log_bwd[l],
                  gdn_dt_bias_fwd[l], gdn_dt_bias_bwd[l], gdn_norm_w[l], n_ctx)
        oc = _retention(z, ret_norm_w[l], n_ctx)
        merged = jnp.concatenate([oa, ob, oc], axis=-1)
        xs = _matmul_gated_residual(merged, w_out[l].astype(BF16), xs, mod, 2, n_ctx, d, "out_proj")

        h = _norm_mod(xs, norm2_w[l], mod, 4, 3, n_ctx).reshape(m_rows, d)
        f2 = ffn_w_up.shape[2]
        uv = _matmul(h, ffn_w_up[l].astype(BF16), tm, f2 // 4, BF16, "ffn_up").reshape(bsz, t, f2)
        xs = _ffn_tail(uv, ffn_conv_w[l], ffn_conv_b[l], ffn_w_down[l].astype(BF16), xs, mod, 5, n_ctx, 512)
    return _final_norm(xs, final_norm_w, n_ctx)
```

```python
import functools
import math

import numpy as np
import jax
import jax.numpy as jnp
from jax import lax
from jax.experimental import pallas as pl
from jax.experimental.pallas import tpu as pltpu

F32 = jnp.float32
BF16 = jnp.bfloat16

HEAD_DIM = 128
N_HEADS = 16
A_HEADS = 5
C_HEADS = 5
B_HEADS = 6
A_W = A_HEADS * HEAD_DIM
B_W = B_HEADS * HEAD_DIM
C_W = C_HEADS * HEAD_DIM
N_GATES = 4 * B_HEADS
CHUNK = 64
GRID_W = 64
ROPE_BASE = 10000.0
NORM_EPS = 1e-6
N_MOD = 6
GATE_CLAMP = 60.0
SCALE = HEAD_DIM ** -0.5
LOG2_E = 1.4426950408889634
LANES = 128
MOD_ROWS = 8
ROW_TILE = 256
VMEM_LIMIT = 56 * 1024 * 1024
HGRN_GROUP = 6
GDN_GROUP = 12
RET_GROUP = 12
PREP_GROUP = 4

ZB_AQ, ZB_AI, ZB_AFF, ZB_AFB, ZB_AG = 0, 5, 10, 15, 20
ZB_BQ, ZB_BK, ZB_BV, ZB_BG = 25, 31, 37, 43
ZB_CQ, ZB_CK, ZB_CV, ZB_CG = 49, 54, 59, 64
Z_BLOCKS = 69


def _cparams(n_axes):
    return pltpu.CompilerParams(dimension_semantics=("arbitrary",) * n_axes,
                                vmem_limit_bytes=VMEM_LIMIT)


def _dot(a, b):
    return jnp.dot(a, b, preferred_element_type=F32)


def _dot_nt(a, b):
    return lax.dot_general(a, b, (((1,), (1,)), ((), ())), preferred_element_type=F32)


def _dot_tn(a, b):
    return lax.dot_general(a, b, (((0,), (0,)), ((), ())), preferred_element_type=F32)


def _bf(a):
    return a.astype(BF16)


def _split2(a):
    hi = a.astype(BF16)
    lo = (a - hi.astype(F32)).astype(BF16)
    return hi, lo


def _dot3(a, b):
    ah, al = _split2(a)
    bh, bl = _split2(b)
    return _dot(ah, bh) + (_dot(al, bh) + _dot(ah, bl))


def _row_mean(a):
    hi, lo = _split2(a)
    ones = jnp.ones((a.shape[1], a.shape[1]), BF16)
    return (_dot(hi, ones) + _dot(lo, ones)) * (1.0 / a.shape[1])


def _silu(a):
    return a * jax.nn.sigmoid(a)


def _softplus(a):
    return jnp.maximum(a, 0.0) + jnp.log1p(jnp.exp(-jnp.abs(a)))


def _log_sigmoid(a):
    return jnp.minimum(a, 0.0) - jnp.log1p(jnp.exp(-jnp.abs(a)))


def _mod_kernel(a_ref, w_ref, b_ref, o_ref):
    a = _silu(a_ref[...])
    o_ref[0] = _dot3(a, w_ref[0]) + b_ref[0]


def _modulation(rows, w_mod, b_mod):
    depth, d, n = w_mod.shape
    tn = 1024
    return pl.pallas_call(
        _mod_kernel,
        grid=(depth, n // tn),
        in_specs=[pl.BlockSpec((MOD_ROWS, d), lambda l, j: (0, 0)),
                  pl.BlockSpec((1, d, tn), lambda l, j: (l, 0, j)),
                  pl.BlockSpec((1, 1, tn), lambda l, j: (l, 0, j))],
        out_specs=pl.BlockSpec((1, MOD_ROWS, tn), lambda l, j: (l, 0, j)),
        out_shape=jax.ShapeDtypeStruct((depth, MOD_ROWS, n), F32),
        compiler_params=_cparams(2),
        name="modulation",
    )(rows, w_mod, b_mod.reshape(depth, 1, n))


def _norm_mod_kernel(x_ref, w_ref, sc_ref, sh_ref, o_ref):
    o_ref[0] = _rms_mod(x_ref[0], w_ref[...], sc_ref[0], sh_ref[0]).astype(o_ref.dtype)


def _mod_row(b, i, n_ctx_tiles, n_batch, k):
    return (jnp.where(i < n_ctx_tiles, n_batch, b) * N_MOD + k, 0, 0)


def _norm_mod(x, w, mod, k_scale, k_shift, n_ctx):
    bsz, t, d = x.shape
    nct = n_ctx // ROW_TILE
    return pl.pallas_call(
        _norm_mod_kernel,
        grid=(bsz, t // ROW_TILE),
        in_specs=[pl.BlockSpec((1, ROW_TILE, d), lambda b, i: (b, i, 0)),
                  pl.BlockSpec((1, d), lambda b, i: (0, 0)),
                  pl.BlockSpec((1, 1, d), lambda b, i: _mod_row(b, i, nct, bsz, k_scale)),
                  pl.BlockSpec((1, 1, d), lambda b, i: _mod_row(b, i, nct, bsz, k_shift))],
        out_specs=pl.BlockSpec((1, ROW_TILE, d), lambda b, i: (b, i, 0)),
        out_shape=jax.ShapeDtypeStruct((bsz, t, d), BF16),
        compiler_params=_cparams(2),
        name="norm_mod",
    )(x, w.reshape(1, d), mod, mod)


def _mm_kernel(a_ref, w_ref, o_ref):
    o_ref[...] = _dot(a_ref[...], w_ref[...]).astype(o_ref.dtype)


def _matmul(a, w, tm, tn, out_dtype, name):
    m, k = a.shape
    n = w.shape[1]
    return pl.pallas_call(
        _mm_kernel,
        grid=(n // tn, m // tm),
        in_specs=[pl.BlockSpec((tm, k), lambda j, i: (i, 0)),
                  pl.BlockSpec((k, tn), lambda j, i: (0, j))],
        out_specs=pl.BlockSpec((tm, tn), lambda j, i: (i, j)),
        out_shape=jax.ShapeDtypeStruct((m, n), out_dtype),
        compiler_params=_cparams(2),
        name=name,
    )(a, w)


def _rms_mod(x, w, scale, shift):
    return (x * lax.rsqrt(jnp.mean(x * x, axis=-1, keepdims=True) + NORM_EPS) * w) * (1.0 + scale) + shift


def _out_proj_kernel(a_ref, w_ref, x_ref, g_ref, nw_ref, sc_ref, sh_ref, o_ref, h_ref):
    x = x_ref[0] + g_ref[0] * _dot(a_ref[0], w_ref[...])
    o_ref[0] = x
    h_ref[0] = _rms_mod(x, nw_ref[...], sc_ref[0], sh_ref[0]).astype(h_ref.dtype)


def _out_proj(a, w, x, mod, k_gate, norm_w, k_scale, k_shift, n_ctx):
    bsz, t, k = a.shape
    d = w.shape[1]
    nct = n_ctx // ROW_TILE

    def mod_spec(kk):
        return pl.BlockSpec((1, 1, d), lambda b, i: _mod_row(b, i, nct, bsz, kk))

    row_spec = pl.BlockSpec((1, ROW_TILE, d), lambda b, i: (b, i, 0))
    return pl.pallas_call(
        _out_proj_kernel,
        grid=(bsz, t // ROW_TILE),
        in_specs=[pl.BlockSpec((1, ROW_TILE, k), lambda b, i: (b, i, 0)),
                  pl.BlockSpec((k, d), lambda b, i: (0, 0)),
                  row_spec, mod_spec(k_gate),
                  pl.BlockSpec((1, d), lambda b, i: (0, 0)), mod_spec(k_scale), mod_spec(k_shift)],
        out_specs=[row_spec, row_spec],
        out_shape=[jax.ShapeDtypeStruct(x.shape, F32), jax.ShapeDtypeStruct(x.shape, BF16)],
        input_output_aliases={2: 0},
        compiler_params=_cparams(2),
        name="out_proj",
    )(a, w, x, mod, norm_w.reshape(1, d), mod, mod)


def _ffn_tail_kernel(up_ref, um_ref, un_ref, v_ref, cw_ref, cb_ref, wd_ref, x_ref, g_ref, nw_ref, *rest,
                     n_ctx_tiles, n_tiles, tile_offset, tc, last):
    i = pl.program_id(1) + tile_offset
    is_ctx = i < n_ctx_tiles
    has_up = jnp.logical_and(jnp.logical_not(is_ctx), i > n_ctx_tiles)
    has_dn = jnp.logical_and(jnp.logical_not(is_ctx), i < n_tiles - 1)
    row_on = jnp.where(is_ctx, 0.0, 1.0)
    up_on = jnp.where(has_up, 1.0, 0.0)
    dn_on = jnp.where(has_dn, 1.0, 0.0)
    r = lax.broadcasted_iota(jnp.int32, (ROW_TILE, 1), 0)
    c = jnp.where(is_ctx, r, jnp.bitwise_and(r, GRID_W - 1))
    not_first = c != 0
    not_last = c != jnp.where(is_ctx, ROW_TILE - 1, GRID_W - 1)
    acc = None
    for j in range(um_ref.shape[2] // tc):
        cols = slice(j * tc, (j + 1) * tc)
        main = um_ref[0, :, cols].astype(F32)
        prev = up_ref[0, :, cols].astype(F32) * up_on
        nxt = un_ref[0, :, cols].astype(F32) * dn_on
        up = jnp.concatenate([prev, main[:ROW_TILE - GRID_W]], axis=0)
        dn = jnp.concatenate([main[GRID_W:], nxt], axis=0)
        w = cw_ref[:, cols]
        wu = w[0:3] * row_on
        wc = w[3:6]
        wd = w[6:9] * row_on

        def col(kw):
            return up * wu[kw:kw + 1] + main * wc[kw:kw + 1] + dn * wd[kw:kw + 1]

        left = jnp.where(not_first, pltpu.roll(col(0), 1, 0), 0.0)
        right = jnp.where(not_last, pltpu.roll(col(2), ROW_TILE - 1, 0), 0.0)
        u = left + col(1) + right + cb_ref[:, cols]
        act = (_silu(u) * v_ref[0, :, cols].astype(F32)).astype(BF16)
        part = _dot(act, wd_ref[cols, :])
        acc = part if acc is None else acc + part
    x = x_ref[0] + g_ref[0] * acc
    if last:
        (o_ref,) = rest
        o_ref[0] = x * lax.rsqrt(jnp.mean(x * x, axis=-1, keepdims=True) + NORM_EPS) * nw_ref[...]
    else:
        sc_ref, sh_ref, o_ref, h_ref = rest
        o_ref[0] = x
        h_ref[0] = _rms_mod(x, nw_ref[...], sc_ref[0], sh_ref[0]).astype(h_ref.dtype)


def _ffn_tail(uv, conv_w, conv_b, w_down, x, mod, k_gate, norm_w, mod_next, k_scale, k_shift, n_ctx, tc):
    bsz, t, f2 = uv.shape
    f = f2 // 2
    d = w_down.shape[1]
    assert n_ctx == ROW_TILE and ROW_TILE % GRID_W == 0 and f % tc == 0
    last = mod_next is None
    nct = n_ctx // ROW_TILE
    nt = t // ROW_TILE
    off = nct if last else 0
    per = ROW_TILE // GRID_W
    nrow = t // GRID_W
    kern = functools.partial(_ffn_tail_kernel, n_ctx_tiles=nct, n_tiles=nt, tile_offset=off, tc=tc, last=last)

    def mod_spec(kk):
        return pl.BlockSpec((1, 1, d), lambda b, i: _mod_row(b, i + off, nct, bsz, kk))

    row_spec = pl.BlockSpec((1, ROW_TILE, d), lambda b, i: (b, i, 0))
    in_specs = [pl.BlockSpec((1, GRID_W, f), lambda b, i: (b, jnp.maximum((i + off) * per - 1, 0), 0)),
                pl.BlockSpec((1, ROW_TILE, f), lambda b, i: (b, i + off, 0)),
                pl.BlockSpec((1, GRID_W, f), lambda b, i: (b, jnp.minimum((i + off) * per + per, nrow - 1), 0)),
                pl.BlockSpec((1, ROW_TILE, f), lambda b, i: (b, i + off, 1)),
                pl.BlockSpec((9, f), lambda b, i: (0, 0)),
                pl.BlockSpec((1, f), lambda b, i: (0, 0)),
                pl.BlockSpec((f, d), lambda b, i: (0, 0), pipeline_mode=pl.Buffered(1)),
                pl.BlockSpec((1, ROW_TILE, d), lambda b, i: (b, i + off, 0)),
                mod_spec(k_gate),
                pl.BlockSpec((1, d), lambda b, i: (0, 0))]
    args = [uv, uv, uv, uv, conv_w.reshape(9, f), conv_b.reshape(1, f), w_down, x, mod, norm_w.reshape(1, d)]
    if last:
        return pl.pallas_call(
            kern, grid=(bsz, nt - off), in_specs=in_specs, out_specs=row_spec,
            out_shape=jax.ShapeDtypeStruct((bsz, t - n_ctx, d), F32),
            compiler_params=_cparams(2), name="ffn_tail_final",
        )(*args)
    return pl.pallas_call(
        kern, grid=(bsz, nt), in_specs=in_specs + [mod_spec(k_scale), mod_spec(k_shift)],
        out_specs=[row_spec, row_spec],
        out_shape=[jax.ShapeDtypeStruct(x.shape, F32), jax.ShapeDtypeStruct(x.shape, BF16)],
        input_output_aliases={7: 0},
        compiler_params=_cparams(2), name="ffn_tail",
    )(*args, mod_next, mod_next)


def _scan_both(group_fn, finish_fn, init, n_ctx_chunks, n_chunks, group):
    assert n_chunks % group == 0

    def body(i, carry):
        items = []
        for j in range(group):
            p = i * group + j
            items.append((p, True))
            items.append((jnp.where(p < n_ctx_chunks, n_ctx_chunks - 1 - p, n_chunks + n_ctx_chunks - 1 - p), False))
        return group_fn(items, carry)

    lax.fori_loop(0, n_chunks // group, body, (init, init))

    def finish(i, carry):
        finish_fn([_rows(i * group + j) for j in range(group)])
        return carry

    lax.fori_loop(0, n_chunks // group, finish, 0)


def _rows(c):
    return pl.ds(pl.multiple_of(c * CHUNK, CHUNK), CHUNK)


def _finish_heads(rows_list, o_ref, of_scr, zg_ref, nw_ref, centre):
    outs = [of_scr[r, :] + o_ref[0, r, :].astype(F32) for r in rows_list]
    if centre:
        mus = [_row_mean(o) for o in outs]
        outs = [o - mu for o, mu in zip(outs, mus)]
    mss = [_row_mean(o * o) for o in outs]
    for r, o, ms in zip(rows_list, outs, mss):
        o = o * lax.rsqrt(ms + NORM_EPS) * nw_ref[...]
        o_ref[0, r, :] = (o * _silu(zg_ref[0, r, :].astype(F32))).astype(o_ref.dtype)


def _level_constants():
    idx = np.arange(CHUNK)
    x = idx[:, None] ^ idx[None, :]
    lvl = np.where(x > 0, np.floor(np.log2(np.maximum(x, 1))), 6).astype(np.int32)
    lvl_f = np.where(idx[:, None] >= idx[None, :], lvl, 7).astype(np.int32)
    lvl_b = lvl_f.T.copy()
    def stack(fwd):
        tri = ((idx[None, :] <= idx[:, None]) if fwd else (idx[None, :] >= idx[:, None])).astype(np.float32)
        mats = [tri]
        for m in (32, 16, 8, 4, 2, 1):
            ref = (idx // (2 * m)) * (2 * m) + (m - 1 if fwd else m)
            mats.append(tri - tri[ref])
        return np.concatenate(mats, axis=0)
    return lvl_f, lvl_b, stack(True), stack(False)


class _Item:
    pass


def _hgrn_kernel(zq_ref, zi_ref, zff_ref, zfb_ref, zg_ref, lbf_ref, lbb_ref, nw_ref,
                 self_ref, selb_ref, lvlf_ref, lvlb_ref, o_ref, of_scr, *, n_ctx_chunks, n_chunks):
    def group_fn(items, carry):
        st = {True: carry[0], False: carry[1]}
        its = []
        for c, fwd in items:
            it = _Item()
            it.fwd, it.rows = fwd, _rows(c)
            zf_ref, lb_ref = (zff_ref, lbf_ref) if fwd else (zfb_ref, lbb_ref)
            it.q = _silu(zq_ref[0, it.rows, :].astype(F32)) * SCALE
            it.v = zi_ref[0, it.rows, :]
            zf = zf_ref[0, it.rows, :].astype(F32)
            lb = lb_ref[...]
            logf = _log_sigmoid(zf) + jnp.log1p(lb * jnp.exp(-jnp.maximum(zf, -GATE_CLAMP)))
            it.k = (1.0 - lb) * jax.nn.sigmoid(-zf)
            it.qb, it.kb = _bf(it.q), _bf(it.k)
            it.cat = jnp.concatenate(_split2(logf * LOG2_E), axis=1)
            its.append(it)
        for it in its:
            cs = _dot((self_ref if it.fwd else selb_ref)[...], it.cat)
            it.cs = cs[:, :LANES] + cs[:, LANES:]
            it.b = it.cs[:CHUNK]
            it.lvl = (lvlf_ref if it.fwd else lvlb_ref)[...]
        for it in its:
            it.scores = jnp.where(it.lvl == 6, _dot_nt(it.qb, it.kb), 0.0)
        for n in range(6):
            for it in its:
                wgt = _bf(jnp.exp2(-jnp.abs(it.cs[(n + 1) * CHUNK:(n + 2) * CHUNK])))
                it.scores = jnp.where(it.lvl == 5 - n, _dot_nt(it.qb * wgt, it.kb * wgt), it.scores)
        for it in its:
            b_tot = it.b[CHUNK - 1:CHUNK] if it.fwd else it.b[0:1]
            it.intra = _dot(_bf(it.scores), it.v)
            it.upd = _dot_tn(it.v, _bf(it.k * jnp.exp2(b_tot - it.b)))
            it.qd = _bf(it.q * jnp.exp2(it.b))
            it.dec = jnp.exp2(b_tot)
        for it in its:
            out = it.intra + _dot_nt(it.qd, _bf(st[it.fwd]))
            st[it.fwd] = st[it.fwd] * it.dec + it.upd
            if it.fwd:
                of_scr[it.rows, :] = out
            else:
                o_ref[0, it.rows, :] = out.astype(o_ref.dtype)
        return st[True], st[False]

    def finish_fn(rows_list):
        _finish_heads(rows_list, o_ref, of_scr, zg_ref, nw_ref, centre=False)

    _scan_both(group_fn, finish_fn, jnp.zeros((HEAD_DIM, HEAD_DIM), F32), n_ctx_chunks, n_chunks, HGRN_GROUP)


def _zspec(t, base):
    return pl.BlockSpec((1, t, LANES), lambda b, h: (b, 0, base + h))


def _const_spec(shape):
    return pl.BlockSpec(shape, lambda b, h: (0,) * len(shape))


def _hgrn(z, lb_f, lb_b, norm_w, n_ctx):
    bsz, t, _ = z.shape
    lvl_f, lvl_b, sel_f, sel_b = _level_constants()
    kern = functools.partial(_hgrn_kernel, n_ctx_chunks=n_ctx // CHUNK, n_chunks=t // CHUNK)
    head_vec = pl.BlockSpec((1, LANES), lambda b, h: (0, h))
    return pl.pallas_call(
        kern,
        grid=(bsz, A_HEADS),
        in_specs=[_zspec(t, ZB_AQ), _zspec(t, ZB_AI), _zspec(t, ZB_AFF), _zspec(t, ZB_AFB), _zspec(t, ZB_AG),
                  head_vec, head_vec, _const_spec((1, LANES)),
                  _const_spec(sel_f.shape), _const_spec(sel_b.shape),
                  _const_spec(lvl_f.shape), _const_spec(lvl_b.shape)],
        out_specs=pl.BlockSpec((1, t, LANES), lambda b, h: (b, 0, h)),
        out_shape=jax.ShapeDtypeStruct((bsz, t, N_HEADS * HEAD_DIM), BF16),
        scratch_shapes=[pltpu.VMEM((t, HEAD_DIM), F32)],
        compiler_params=_cparams(2),
        name="hgrn",
    )(z, z, z, z, z, lb_f.reshape(1, A_W), lb_b.reshape(1, A_W), norm_w.reshape(1, LANES),
      jnp.asarray(sel_f, BF16), jnp.asarray(sel_b, BF16), jnp.asarray(lvl_f), jnp.asarray(lvl_b))


def _gdn_kernel(zq_ref, zk_ref, zv_ref, zg_ref, wq_ref, wk_ref, wv_ref, gc_ref, gr_ref, par_ref, nw_ref,
                lvlf_ref, lvlb_ref, merged_ref, o_ref, q_scr, k_scr, v_scr, of_scr, *, n_ctx_chunks, n_chunks):
    del merged_ref
    t_total = n_chunks * CHUNK
    ri = lax.broadcasted_iota(jnp.int32, (CHUNK, CHUNK), 0)
    ci = lax.broadcasted_iota(jnp.int32, (CHUNK, CHUNK), 1)
    row_id = lax.broadcasted_iota(jnp.int32, (CHUNK, 1), 0)
    eye = jnp.where(ri == ci, 1.0, 0.0)

    def conv(z_ref, w_ref, c):
        has_prev = jnp.logical_and(c != 0, c != n_ctx_chunks)
        has_next = jnp.logical_and(c != n_ctx_chunks - 1, c != n_chunks - 1)
        p0 = pl.multiple_of(jnp.maximum(c * CHUNK - 16, 0), 16)
        n0 = pl.multiple_of(jnp.minimum(c * CHUNK + CHUNK, t_total - 16), 16)
        x = z_ref[0, _rows(c), :].astype(F32)
        xp = z_ref[0, pl.ds(p0, 16), :].astype(F32)[15:16] * jnp.where(has_prev, 1.0, 0.0)
        xn = z_ref[0, pl.ds(n0, 16), :].astype(F32)[0:1] * jnp.where(has_next, 1.0, 0.0)
        xm1 = jnp.where(row_id == 0, xp, pltpu.roll(x, 1, 0))
        xp1 = jnp.where(row_id == CHUNK - 1, xn, pltpu.roll(x, CHUNK - 1, 0))
        w = w_ref[...]
        return _silu(xm1 * w[0:1] + x * w[1:2] + xp1 * w[2:3])

    def prep(i, carry):
        cs = [i * PREP_GROUP + j for j in range(PREP_GROUP)]
        qs = [conv(zq_ref, wq_ref, c) for c in cs]
        ks = [conv(zk_ref, wk_ref, c) for c in cs]
        q_ss = [_row_mean(a * a) * HEAD_DIM for a in qs]
        k_ss = [_row_mean(a * a) * HEAD_DIM for a in ks]
        for c, q, k, qn, kn in zip(cs, qs, ks, q_ss, k_ss):
            q_scr[_rows(c), :] = (q * (lax.rsqrt(qn + NORM_EPS) * SCALE)).astype(q_scr.dtype)
            k_scr[_rows(c), :] = (k * lax.rsqrt(kn + NORM_EPS)).astype(k_scr.dtype)
            v_scr[_rows(c), :] = conv(zv_ref, wv_ref, c).astype(v_scr.dtype)
        return carry

    lax.fori_loop(0, n_chunks // PREP_GROUP, prep, 0)

    def group_fn(items, carry):
        st = {True: carry[0], False: carry[1]}
        par = par_ref[0]
        its = []
        for c, fwd in items:
            it = _Item()
            it.fwd, it.rows = fwd, _rows(c)
            ia, ib, pr = (0, 2, 0) if fwd else (1, 3, 2)
            neg_a = -jnp.exp(par[pr:pr + 1, 0:1])
            dt = par[pr + 1:pr + 2, 0:1]
            it.q = q_scr[it.rows, :]
            it.k = k_scr[it.rows, :]
            it.v = v_scr[it.rows, :].astype(F32)
            g_col = neg_a * _softplus(gc_ref[0, 0, it.rows, ia:ia + 1] + dt)
            g_row = neg_a * _softplus(gr_ref[0, 0, ia, pl.ds(c, 1), :] + dt)
            it.beta = jax.nn.sigmoid(gc_ref[0, 0, it.rows, ib:ib + 1])
            incl = (ci <= ri) if fwd else (ci >= ri)
            incl_t = (ri <= ci) if fwd else (ri >= ci)
            it.bc = jnp.sum(jnp.where(incl, g_row, 0.0), axis=1, keepdims=True)
            bc_row = jnp.sum(jnp.where(incl_t, g_col, 0.0), axis=0, keepdims=True)
            it.lmat = jnp.where(incl, jnp.exp(jnp.minimum(it.bc - bc_row, 0.0)), 0.0)
            it.lvl = (lvlf_ref if fwd else lvlb_ref)[...]
            its.append(it)
        for it in its:
            it.amat = it.beta * _dot_nt(it.k, it.k) * it.lmat
            it.qk = _bf(_dot_nt(it.q, it.k) * it.lmat)
        for it in its:
            it.x = eye - jnp.where(it.lvl == 0, it.amat, 0.0)
        for lev in range(1, 6):
            for it in its:
                it.xb = _bf(it.x)
                it.y = _dot(_bf(jnp.where(it.lvl == lev, it.amat, 0.0)), it.xb)
            for it in its:
                it.x = it.x - _dot(it.xb, _bf(it.y))
        for it in its:
            e_col = jnp.exp(it.bc)
            kf = it.k.astype(F32)
            sol = _dot(_bf(it.x), _bf(jnp.concatenate([it.v * it.beta, kf * (it.beta * e_col)], axis=1)))
            it.u = _bf(sol[:, :HEAD_DIM])
            it.w = _bf(sol[:, HEAD_DIM:])
            bc_tot = it.bc[CHUNK - 1:CHUNK] if it.fwd else it.bc[0:1]
            it.qd = it.q.astype(F32) * e_col
            it.kd = _bf(kf * jnp.exp(bc_tot - it.bc))
            it.last = jnp.exp(bc_tot)
        for it in its:
            it.kw = _bf(_dot_tn(it.kd, it.w))
            it.ku = _dot_tn(it.kd, it.u)
            it.qs = _bf(it.qd - _dot(it.qk, it.w))
            it.o0 = _dot(it.qk, it.u)
        for it in its:
            s = st[it.fwd]
            sb = _bf(s)
            out = it.o0 + _dot(it.qs, sb)
            st[it.fwd] = it.last * s + (it.ku - _dot(it.kw, sb))
            if it.fwd:
                of_scr[it.rows, :] = out
            else:
                o_ref[0, it.rows, :] = out.astype(o_ref.dtype)
        return st[True], st[False]

    def finish_fn(rows_list):
        _finish_heads(rows_list, o_ref, of_scr, zg_ref, nw_ref, centre=False)

    _scan_both(group_fn, finish_fn, jnp.zeros((HEAD_DIM, HEAD_DIM), F32), n_ctx_chunks, n_chunks, GDN_GROUP)


def _gdn(z, zgate, merged, conv_w, a_log_f, a_log_b, dt_f, dt_b, norm_w, n_ctx):
    bsz, t, _ = z.shape
    n = t // CHUNK
    g = zgate[:, :, :N_GATES].reshape(bsz, t, 4, B_HEADS)
    gate_cols = g.transpose(0, 3, 1, 2)
    gate_rows = g.reshape(bsz, n, CHUNK, 4, B_HEADS).transpose(0, 4, 3, 1, 2)
    par = jnp.stack([a_log_f, dt_f, a_log_b, dt_b] + [jnp.zeros_like(dt_f)] * 4, axis=1)
    par = jnp.broadcast_to(par[:, :, None], (B_HEADS, 8, LANES)).astype(F32)
    lvl_f, lvl_b, _, _ = _level_constants()
    kern = functools.partial(_gdn_kernel, n_ctx_chunks=n_ctx // CHUNK, n_chunks=n)

    def wspec(base):
        return pl.BlockSpec((3, LANES), lambda b, h: (0, base + h))

    return pl.pallas_call(
        kern,
        grid=(bsz, B_HEADS),
        in_specs=[_zspec(t, ZB_BQ), _zspec(t, ZB_BK), _zspec(t, ZB_BV), _zspec(t, ZB_BG),
                  wspec(0), wspec(B_HEADS), wspec(2 * B_HEADS),
                  pl.BlockSpec((1, 1, t, 4), lambda b, h: (b, h, 0, 0)),
                  pl.BlockSpec((1, 1, 4, n, CHUNK), lambda b, h: (b, h, 0, 0, 0)),
                  pl.BlockSpec((1, 8, LANES), lambda b, h: (h, 0, 0)),
                  _const_spec((1, LANES)), _const_spec(lvl_f.shape), _const_spec(lvl_b.shape),
                  pl.BlockSpec(memory_space=pl.ANY)],
        out_specs=pl.BlockSpec((1, t, LANES), lambda b, h: (b, 0, A_HEADS + h)),
        out_shape=jax.ShapeDtypeStruct(merged.shape, BF16),
        input_output_aliases={13: 0},
        scratch_shapes=[pltpu.VMEM((t, HEAD_DIM), BF16)] * 3 + [pltpu.VMEM((t, HEAD_DIM), F32)],
        compiler_params=_cparams(2),
        name="gdn",
    )(z, z, z, z, conv_w, conv_w, conv_w, gate_cols, gate_rows, par, norm_w.reshape(1, LANES),
      jnp.asarray(lvl_f), jnp.asarray(lvl_b), merged)


def _ret_kernel(zq_ref, zk_ref, zv_ref, zg_ref, base_ref, loc_ref, dec_ref, dmat_ref, cdec_ref, nw_ref,
                merged_ref, o_ref, of_scr, *, n_ctx_chunks, n_chunks):
    del merged_ref
    def group_fn(items, carry):
        st = {True: carry[0], False: carry[1]}
        its = []
        for c, fwd in items:
            it = _Item()
            it.fwd, it.rows, it.d = fwd, _rows(c), 0 if fwd else 1
            d = it.d
            cb = base_ref[2 * d, pl.ds(c, 1), :]
            sb = base_ref[2 * d + 1, pl.ds(c, 1), :]
            cl, sl, cls, sls = (loc_ref[4 * d + n] for n in range(4))
            cos = cb * cl - sb * sl
            sin = sb * cls + cb * sls

            def rot(a):
                return a * cos + pltpu.roll(a, HEAD_DIM // 2, 1) * sin

            q = rot(zq_ref[0, it.rows, :].astype(F32) * SCALE)
            k = rot(zk_ref[0, it.rows, :].astype(F32))
            it.q, it.k = _bf(q), _bf(k)
            it.qd = _bf(q * dec_ref[0, 2 * d])
            it.kd = _bf(k * dec_ref[0, 2 * d + 1])
            it.v = zv_ref[0, it.rows, :]
            its.append(it)
        for it in its:
            it.scores = _bf(_dot_nt(it.q, it.k) * dmat_ref[0, it.d])
        for it in its:
            it.intra = _dot(it.scores, it.v)
            it.upd = _dot_tn(it.v, it.kd)
        for it in its:
            out = it.intra + _dot_nt(it.qd, _bf(st[it.fwd]))
            st[it.fwd] = st[it.fwd] * cdec_ref[0, it.d:it.d + 1, :] + it.upd
            if it.fwd:
                of_scr[it.rows, :] = out
            else:
                o_ref[0, it.rows, :] = out.astype(o_ref.dtype)
        return st[True], st[False]

    def finish_fn(rows_list):
        _finish_heads(rows_list, o_ref, of_scr, zg_ref, nw_ref, centre=True)

    _scan_both(group_fn, finish_fn, jnp.zeros((HEAD_DIM, HEAD_DIM), F32), n_ctx_chunks, n_chunks, RET_GROUP)


def _retention_tables(n_ctx, t):
    half = HEAD_DIM // 2
    n = t // CHUNK
    nc = n_ctx // CHUNK
    inv = ROPE_BASE ** (-jnp.arange(half, dtype=F32) / half)
    inv = jnp.concatenate([inv, inv])
    sgn = jnp.concatenate([-jnp.ones((half,), F32), jnp.ones((half,), F32)])
    c_idx = np.arange(n)
    base_f = (c_idx * CHUNK).astype(np.float32)
    base_b = np.where(c_idx < nc, n_ctx - CHUNK * (c_idx + 1), t - CHUNK * (c_idx - nc + 1)).astype(np.float32)
    loc_f = np.arange(CHUNK, dtype=np.float32)
    loc_b = loc_f[::-1].copy()

    def ang(p):
        return jnp.asarray(p)[:, None] * inv[None, :]

    base = jnp.stack([jnp.cos(ang(base_f)), jnp.sin(ang(base_f)), jnp.cos(ang(base_b)), jnp.sin(ang(base_b))])
    loc = jnp.stack([jnp.cos(ang(loc_f)), jnp.sin(ang(loc_f)), jnp.cos(ang(loc_f)) * sgn, jnp.sin(ang(loc_f)) * sgn,
                     jnp.cos(ang(loc_b)), jnp.sin(ang(loc_b)), jnp.cos(ang(loc_b)) * sgn, jnp.sin(ang(loc_b)) * sgn])
    lg_f = jnp.log1p(-jnp.exp2(-5.0 - jnp.arange(C_HEADS, dtype=F32)))
    lg_b = lg_f[::-1]
    idx = jnp.arange(CHUNK, dtype=F32)
    rel = idx[:, None] - idx[None, :]

    def masked_exp(a, mask):
        return jnp.where(mask, jnp.exp(jnp.where(mask, a, 0.0)), 0.0)

    dmat_f = masked_exp(lg_f[:, None, None] * rel, (rel >= 0)[None])
    dmat_b = masked_exp(-lg_b[:, None, None] * rel, (rel <= 0)[None])
    dq_f = jnp.exp(lg_f[:, None] * (idx + 1.0))
    dk_f = jnp.exp(lg_f[:, None] * (CHUNK - 1.0 - idx))
    dq_b = jnp.exp(lg_b[:, None] * (CHUNK - idx))
    dk_b = jnp.exp(lg_b[:, None] * idx)
    dec = jnp.stack([dq_f, dk_f, dq_b, dk_b], axis=1)
    dec = jnp.broadcast_to(dec[..., None], (C_HEADS, 4, CHUNK, LANES))
    cdec = jnp.stack([jnp.exp(lg_f * CHUNK), jnp.exp(lg_b * CHUNK)] + [jnp.zeros((C_HEADS,), F32)] * 6, axis=1)
    cdec = jnp.broadcast_to(cdec[..., None], (C_HEADS, 8, LANES))
    return base, loc, dec, jnp.stack([dmat_f, dmat_b], axis=1), cdec


def _retention(z, merged, norm_w, n_ctx):
    bsz, t, _ = z.shape
    n = t // CHUNK
    base, loc, dec, dmat, cdec = _retention_tables(n_ctx, t)
    kern = functools.partial(_ret_kernel, n_ctx_chunks=n_ctx // CHUNK, n_chunks=n)
    return pl.pallas_call(
        kern,
        grid=(bsz, C_HEADS),
        in_specs=[_zspec(t, ZB_CQ), _zspec(t, ZB_CK), _zspec(t, ZB_CV), _zspec(t, ZB_CG),
                  _const_spec(base.shape), _const_spec(loc.shape),
                  pl.BlockSpec((1, 4, CHUNK, LANES), lambda b, h: (h, 0, 0, 0)),
                  pl.BlockSpec((1, 2, CHUNK, CHUNK), lambda b, h: (h, 0, 0, 0)),
                  pl.BlockSpec((1, 8, LANES), lambda b, h: (h, 0, 0)),
                  pl.BlockSpec((1, LANES), lambda b, h: (0, h)),
                  pl.BlockSpec(memory_space=pl.ANY)],
        out_specs=pl.BlockSpec((1, t, LANES), lambda b, h: (b, 0, A_HEADS + B_HEADS + h)),
        out_shape=jax.ShapeDtypeStruct(merged.shape, BF16),
        input_output_aliases={10: 0},
        scratch_shapes=[pltpu.VMEM((t, HEAD_DIM), F32)],
        compiler_params=_cparams(2),
        name="retention",
    )(z, z, z, z, base, loc, dec, dmat, cdec, norm_w.reshape(1, C_W), merged)


def _reorder_in_proj(w):
    g0 = 5 * A_W + 4 * B_W
    main = jnp.concatenate([w[:, :g0], w[:, g0 + N_GATES:]], axis=1).astype(BF16)
    gates = jnp.pad(w[:, g0:g0 + N_GATES], ((0, 0), (0, LANES - N_GATES))).astype(BF16)
    return main, gates


def _lower_bounds(logits):
    p = jax.nn.softmax(logits.astype(F32), axis=0)
    return jnp.cumsum(p, axis=0) - p[0]


def kernel(x, c, ctx, c_ctx, w_mod, b_mod, norm1_w, norm2_w, w_in, qkv_conv_w, hgrn_lb_logits_fwd, hgrn_lb_logits_bwd, hgrn_norm_w, gdn_a_log_fwd, gdn_a_log_bwd, gdn_dt_bias_fwd, gdn_dt_bias_bwd, gdn_norm_w, ret_norm_w, w_out, ffn_w_up, ffn_conv_w, ffn_conv_b, ffn_w_down, final_norm_w):
    bsz, n_lat, d = x.shape
    n_ctx = ctx.shape[1]
    depth = w_mod.shape[0]
    t = n_ctx + n_lat
    assert n_ctx % ROW_TILE == 0 and n_lat % ROW_TILE == 0 and bsz < MOD_ROWS
    lb_fwd = _lower_bounds(hgrn_lb_logits_fwd)
    lb_bwd = _lower_bounds(hgrn_lb_logits_bwd)

    rows = jnp.concatenate([c, c_ctx[None, :], jnp.zeros((MOD_ROWS - bsz - 1, d), F32)], axis=0)
    mods = _modulation(rows, w_mod, b_mod)
    xs = jnp.concatenate([ctx, x], axis=1)

    tm = 512
    m_rows = bsz * t
    assert m_rows % tm == 0
    f2 = ffn_w_up.shape[2]
    mod = [mods[l].reshape(MOD_ROWS * N_MOD, 1, d) for l in range(depth)]
    h = _norm_mod(xs, norm1_w[0], mod[0], 1, 0, n_ctx)
    for l in range(depth):
        w_main, w_gate = _reorder_in_proj(w_in[l])
        h = h.reshape(m_rows, d)
        z = _matmul(h, w_main, tm, 23 * LANES, BF16, "in_proj").reshape(bsz, t, Z_BLOCKS * LANES)
        zgate = _matmul(h, w_gate, tm, LANES, F32, "in_proj_gates").reshape(bsz, t, LANES)
        merged = _hgrn(z, lb_fwd[l], lb_bwd[l], hgrn_norm_w[l], n_ctx)
        merged = _gdn(z, zgate, merged, qkv_conv_w[l], gdn_a_log_fwd[l], gdn_a_log_bwd[l],
                      gdn_dt_bias_fwd[l], gdn_dt_bias_bwd[l], gdn_norm_w[l], n_ctx)
        merged = _retention(z, merged, ret_norm_w[l], n_ctx)
        xs, h = _out_proj(merged, w_out[l].astype(BF16), xs, mod[l], 2, norm2_w[l], 4, 3, n_ctx)

        uv = _matmul(h.reshape(m_rows, d), ffn_w_up[l].astype(BF16), tm, f2 // 4, BF16, "ffn_up").reshape(bsz, t, f2)
        w_down = ffn_w_down[l].astype(BF16)
        if l + 1 < depth:
            xs, h = _ffn_tail(uv, ffn_conv_w[l], ffn_conv_b[l], w_down, xs, mod[l], 5,
                              norm1_w[l + 1], mod[l + 1], 1, 0, n_ctx, 512)
    return _ffn_tail(uv, ffn_conv_w[-1], ffn_conv_b[-1], w_down, xs, mod[-1], 5,
                     final_norm_w, None, None, None, n_ctx, 512)
```

```python
import functools
import math

import numpy as np
import jax
import jax.numpy as jnp
from jax import lax
from jax.experimental import pallas as pl
from jax.experimental.pallas import tpu as pltpu

F32 = jnp.float32
BF16 = jnp.bfloat16

HEAD_DIM = 128
N_HEADS = 16
A_HEADS = 5
C_HEADS = 5
B_HEADS = 6
A_W = A_HEADS * HEAD_DIM
B_W = B_HEADS * HEAD_DIM
C_W = C_HEADS * HEAD_DIM
N_GATES = 4 * B_HEADS
CHUNK = 64
GRID_W = 64
ROPE_BASE = 10000.0
NORM_EPS = 1e-6
N_MOD = 6
GATE_CLAMP = 60.0
SCALE = HEAD_DIM ** -0.5
LOG2_E = 1.4426950408889634
LANES = 128
MOD_ROWS = 8
ROW_TILE = 256
VMEM_LIMIT = 56 * 1024 * 1024
HGRN_GROUP = 6
GDN_GROUP = 12
RET_GROUP = 12
PREP_GROUP = 4

ZB_AQ, ZB_AI, ZB_AFF, ZB_AFB, ZB_AG = 0, 5, 10, 15, 20
ZB_BQ, ZB_BK, ZB_BV, ZB_BG = 25, 31, 37, 43
ZB_CQ, ZB_CK, ZB_CV, ZB_CG = 49, 54, 59, 64
Z_BLOCKS = 69


def _cparams(n_axes):
    return pltpu.CompilerParams(dimension_semantics=("arbitrary",) * n_axes,
                                vmem_limit_bytes=VMEM_LIMIT)


def _dot(a, b):
    return jnp.dot(a, b, preferred_element_type=F32)


def _dot_nt(a, b):
    return lax.dot_general(a, b, (((1,), (1,)), ((), ())), preferred_element_type=F32)


def _dot_tn(a, b):
    return lax.dot_general(a, b, (((0,), (0,)), ((), ())), preferred_element_type=F32)


def _bf(a):
    return a.astype(BF16)


def _split2(a):
    hi = a.astype(BF16)
    lo = (a - hi.astype(F32)).astype(BF16)
    return hi, lo


def _dot3(a, b):
    ah, al = _split2(a)
    bh, bl = _split2(b)
    return _dot(ah, bh) + (_dot(al, bh) + _dot(ah, bl))


def _row_mean(a):
    hi, lo = _split2(a)
    ones = jnp.ones((a.shape[1], a.shape[1]), BF16)
    return (_dot(hi, ones) + _dot(lo, ones)) * (1.0 / a.shape[1])


def _silu(a):
    return a * jax.nn.sigmoid(a)


def _softplus(a):
    return jnp.maximum(a, 0.0) + jnp.log1p(jnp.exp(-jnp.abs(a)))


def _log_sigmoid(a):
    return jnp.minimum(a, 0.0) - jnp.log1p(jnp.exp(-jnp.abs(a)))


def _mod_kernel(a_ref, w_ref, b_ref, o_ref):
    a = _silu(a_ref[...])
    o_ref[0] = _dot3(a, w_ref[0]) + b_ref[0]


def _modulation(rows, w_mod, b_mod):
    depth, d, n = w_mod.shape
    tn = 1024
    return pl.pallas_call(
        _mod_kernel,
        grid=(depth, n // tn),
        in_specs=[pl.BlockSpec((MOD_ROWS, d), lambda l, j: (0, 0)),
                  pl.BlockSpec((1, d, tn), lambda l, j: (l, 0, j)),
                  pl.BlockSpec((1, 1, tn), lambda l, j: (l, 0, j))],
        out_specs=pl.BlockSpec((1, MOD_ROWS, tn), lambda l, j: (l, 0, j)),
        out_shape=jax.ShapeDtypeStruct((depth, MOD_ROWS, n), F32),
        compiler_params=_cparams(2),
        name="modulation",
    )(rows, w_mod, b_mod.reshape(depth, 1, n))


def _mod_row(b, i, n_ctx_tiles, n_batch, k):
    return (jnp.where(i < n_ctx_tiles, n_batch, b) * N_MOD + k, 0, 0)


def _split_stream_specs(n_ctx_tiles, d):
    return [pl.BlockSpec((1, ROW_TILE, d), lambda b, i: (b, jnp.minimum(i, n_ctx_tiles - 1), 0)),
            pl.BlockSpec((1, ROW_TILE, d), lambda b, i: (b, jnp.maximum(i - n_ctx_tiles, 0), 0))]


def _for_stream_tile(n_ctx_tiles, ctx_ref, lat_ref, body):
    i = pl.program_id(1)

    @pl.when(i < n_ctx_tiles)
    def _():
        body(ctx_ref[0])

    @pl.when(i >= n_ctx_tiles)
    def _():
        body(lat_ref[0])


def _norm_mod_kernel(ctx_ref, lat_ref, w_ref, sc_ref, sh_ref, o_ref, *, n_ctx_tiles):
    def body(x):
        o_ref[0] = _rms_mod(x, w_ref[...], sc_ref[0], sh_ref[0]).astype(o_ref.dtype)

    _for_stream_tile(n_ctx_tiles, ctx_ref, lat_ref, body)


def _norm_mod(ctx, x, w, mod, k_scale, k_shift):
    bsz, n_ctx, d = ctx.shape
    t = n_ctx + x.shape[1]
    nct = n_ctx // ROW_TILE
    return pl.pallas_call(
        functools.partial(_norm_mod_kernel, n_ctx_tiles=nct),
        grid=(bsz, t // ROW_TILE),
        in_specs=_split_stream_specs(nct, d) + [
            pl.BlockSpec((1, d), lambda b, i: (0, 0)),
            pl.BlockSpec((1, 1, d), lambda b, i: _mod_row(b, i, nct, bsz, k_scale)),
            pl.BlockSpec((1, 1, d), lambda b, i: _mod_row(b, i, nct, bsz, k_shift))],
        out_specs=pl.BlockSpec((1, ROW_TILE, d), lambda b, i: (b, i, 0)),
        out_shape=jax.ShapeDtypeStruct((bsz, t, d), BF16),
        compiler_params=_cparams(2),
        name="norm_mod",
    )(ctx, x, w.reshape(1, d), mod, mod)


def _mm_kernel(a_ref, w_ref, o_ref):
    o_ref[...] = _dot(a_ref[...], w_ref[...]).astype(o_ref.dtype)


def _matmul(a, w, tm, tn, out_dtype, name):
    m, k = a.shape
    n = w.shape[1]
    return pl.pallas_call(
        _mm_kernel,
        grid=(n // tn, m // tm),
        in_specs=[pl.BlockSpec((tm, k), lambda j, i: (i, 0)),
                  pl.BlockSpec((k, tn), lambda j, i: (0, j))],
        out_specs=pl.BlockSpec((tm, tn), lambda j, i: (i, j)),
        out_shape=jax.ShapeDtypeStruct((m, n), out_dtype),
        compiler_params=_cparams(2),
        name=name,
    )(a, w)


def _rms_mod(x, w, scale, shift):
    return (x * lax.rsqrt(jnp.mean(x * x, axis=-1, keepdims=True) + NORM_EPS) * w) * (1.0 + scale) + shift


def _out_proj_kernel(a_ref, w_ref, g_ref, nw_ref, sc_ref, sh_ref, *rest, n_ctx_tiles, split):
    def body(x):
        x = x + g_ref[0] * _dot(a_ref[0], w_ref[...])
        o_ref[0] = x
        h_ref[0] = _rms_mod(x, nw_ref[...], sc_ref[0], sh_ref[0]).astype(h_ref.dtype)

    if split:
        ctx_ref, lat_ref, o_ref, h_ref = rest
        _for_stream_tile(n_ctx_tiles, ctx_ref, lat_ref, body)
    else:
        x_ref, o_ref, h_ref = rest
        body(x_ref[0])


def _out_proj(a, w, ctx, x, mod, k_gate, norm_w, k_scale, k_shift, n_ctx):
    bsz, t, k = a.shape
    d = w.shape[1]
    nct = n_ctx // ROW_TILE
    split = ctx is not None

    def mod_spec(kk):
        return pl.BlockSpec((1, 1, d), lambda b, i: _mod_row(b, i, nct, bsz, kk))

    row_spec = pl.BlockSpec((1, ROW_TILE, d), lambda b, i: (b, i, 0))
    in_specs = [pl.BlockSpec((1, ROW_TILE, k), lambda b, i: (b, i, 0)),
                pl.BlockSpec((k, d), lambda b, i: (0, 0)),
                mod_spec(k_gate), pl.BlockSpec((1, d), lambda b, i: (0, 0)), mod_spec(k_scale), mod_spec(k_shift)]
    in_specs += _split_stream_specs(nct, d) if split else [row_spec]
    return pl.pallas_call(
        functools.partial(_out_proj_kernel, n_ctx_tiles=nct, split=split),
        grid=(bsz, t // ROW_TILE),
        in_specs=in_specs,
        out_specs=[row_spec, row_spec],
        out_shape=[jax.ShapeDtypeStruct((bsz, t, d), F32), jax.ShapeDtypeStruct((bsz, t, d), BF16)],
        input_output_aliases={} if split else {6: 0},
        compiler_params=_cparams(2),
        name="out_proj",
    )(a, w, mod, norm_w.reshape(1, d), mod, mod, *((ctx, x) if split else (x,)))


def _ffn_tail_kernel(up_ref, um_ref, un_ref, v_ref, cw_ref, cb_ref, shift_ref, wd_ref, x_ref, g_ref, nw_ref, *rest,
                     n_ctx_tiles, n_tiles, tile_offset, tc, last):
    if last:
        (o_ref,) = rest
    else:
        sc_ref, sh_ref, o_ref, h_ref = rest
    i = pl.program_id(1) + tile_offset
    is_ctx = i < n_ctx_tiles
    has_up = jnp.logical_and(jnp.logical_not(is_ctx), i > n_ctx_tiles)
    has_dn = jnp.logical_and(jnp.logical_not(is_ctx), i < n_tiles - 1)
    row_on = jnp.where(is_ctx, 0.0, 1.0)
    up_on = jnp.where(has_up, 1.0, 0.0)
    dn_on = jnp.where(has_dn, 1.0, 0.0)
    shift = shift_ref[jnp.where(is_ctx, 0, 1)]
    acc = None
    for j in range(um_ref.shape[2] // tc):
        cols = slice(j * tc, (j + 1) * tc)
        main = um_ref[0, :, cols].astype(F32)
        prev = up_ref[0, :, cols].astype(F32) * up_on
        nxt = un_ref[0, :, cols].astype(F32) * dn_on
        up = jnp.concatenate([prev, main[:ROW_TILE - GRID_W]], axis=0)
        dn = jnp.concatenate([main[GRID_W:], nxt], axis=0)
        w = cw_ref[:, cols]
        wu = w[0:3] * row_on
        wc = w[3:6]
        wd = w[6:9] * row_on

        def col(kw):
            return up * wu[kw:kw + 1] + main * wc[kw:kw + 1] + dn * wd[kw:kw + 1]

        sides = _dot(shift, jnp.concatenate([_bf(col(0)), _bf(col(2))], axis=0))
        u = sides + col(1) + cb_ref[:, cols]
        act = _bf(_silu(u)) * v_ref[0, :, cols]
        part = _dot(act, wd_ref[cols, :])
        acc = part if acc is None else acc + part
    x = x_ref[0] + g_ref[0] * acc
    if last:
        o_ref[0] = x * lax.rsqrt(jnp.mean(x * x, axis=-1, keepdims=True) + NORM_EPS) * nw_ref[...]
    else:
        o_ref[0] = x
        h_ref[0] = _rms_mod(x, nw_ref[...], sc_ref[0], sh_ref[0]).astype(h_ref.dtype)


def _shift_constants(n_ctx):
    r = np.arange(ROW_TILE)
    out = np.zeros((2, ROW_TILE, 2 * ROW_TILE), np.float32)
    for kind, width in enumerate((n_ctx, GRID_W)):
        has_left = r % width != 0
        has_right = r % width != width - 1
        out[kind, r[has_left], r[has_left] - 1] = 1.0
        out[kind, r[has_right], ROW_TILE + r[has_right] + 1] = 1.0
    return out


def _ffn_tail(uv, conv_w, conv_b, w_down, x, mod, k_gate, norm_w, mod_next, k_scale, k_shift, n_ctx, tc):
    bsz, t, f2 = uv.shape
    f = f2 // 2
    d = w_down.shape[1]
    assert n_ctx == ROW_TILE and ROW_TILE % GRID_W == 0 and f % tc == 0
    last = mod_next is None
    nct = n_ctx // ROW_TILE
    nt = t // ROW_TILE
    off = nct if last else 0
    per = ROW_TILE // GRID_W
    nrow = t // GRID_W
    kern = functools.partial(_ffn_tail_kernel, n_ctx_tiles=nct, n_tiles=nt, tile_offset=off, tc=tc, last=last)

    def mod_spec(kk):
        return pl.BlockSpec((1, 1, d), lambda b, i: _mod_row(b, i + off, nct, bsz, kk))

    row_spec = pl.BlockSpec((1, ROW_TILE, d), lambda b, i: (b, i, 0))
    in_specs = [pl.BlockSpec((1, GRID_W, f), lambda b, i: (b, jnp.maximum((i + off) * per - 1, 0), 0)),
                pl.BlockSpec((1, ROW_TILE, f), lambda b, i: (b, i + off, 0)),
                pl.BlockSpec((1, GRID_W, f), lambda b, i: (b, jnp.minimum((i + off) * per + per, nrow - 1), 0)),
                pl.BlockSpec((1, ROW_TILE, f), lambda b, i: (b, i + off, 1)),
                pl.BlockSpec((9, f), lambda b, i: (0, 0)),
                pl.BlockSpec((1, f), lambda b, i: (0, 0)),
                pl.BlockSpec((2, ROW_TILE, 2 * ROW_TILE), lambda b, i: (0, 0, 0)),
                pl.BlockSpec((f, d), lambda b, i: (0, 0), pipeline_mode=pl.Buffered(1)),
                pl.BlockSpec((1, ROW_TILE, d), lambda b, i: (b, i + off, 0)),
                mod_spec(k_gate),
                pl.BlockSpec((1, d), lambda b, i: (0, 0))]
    args = [uv, uv, uv, uv, conv_w.reshape(9, f), conv_b.reshape(1, f), jnp.asarray(_shift_constants(n_ctx), BF16),
            w_down, x, mod, norm_w.reshape(1, d)]
    if last:
        return pl.pallas_call(
            kern, grid=(bsz, nt - off), in_specs=in_specs, out_specs=row_spec,
            out_shape=jax.ShapeDtypeStruct((bsz, t - n_ctx, d), F32),
            compiler_params=_cparams(2), name="ffn_tail_final",
        )(*args)
    return pl.pallas_call(
        kern, grid=(bsz, nt), in_specs=in_specs + [mod_spec(k_scale), mod_spec(k_shift)],
        out_specs=[row_spec, row_spec],
        out_shape=[jax.ShapeDtypeStruct(x.shape, F32), jax.ShapeDtypeStruct(x.shape, BF16)],
        input_output_aliases={8: 0},
        compiler_params=_cparams(2), name="ffn_tail",
    )(*args, mod_next, mod_next)


def _scan_both(group_fn, finish_fn, init, n_ctx_chunks, n_chunks, group):
    assert n_chunks % group == 0

    def body(i, carry):
        items = []
        for j in range(group):
            p = i * group + j
            items.append((p, True))
            items.append((jnp.where(p < n_ctx_chunks, n_ctx_chunks - 1 - p, n_chunks + n_ctx_chunks - 1 - p), False))
        return group_fn(items, carry)

    lax.fori_loop(0, n_chunks // group, body, (init, init))

    def finish(i, carry):
        finish_fn([_rows(i * group + j) for j in range(group)])
        return carry

    lax.fori_loop(0, n_chunks // group, finish, 0)


def _rows(c):
    return pl.ds(pl.multiple_of(c * CHUNK, CHUNK), CHUNK)


def _finish_heads(rows_list, o_ref, of_scr, zg_ref, nw_ref, centre):
    outs = [of_scr[r, :] + o_ref[0, r, :].astype(F32) for r in rows_list]
    if centre:
        mus = [_row_mean(o) for o in outs]
        outs = [o - mu for o, mu in zip(outs, mus)]
    mss = [_row_mean(o * o) for o in outs]
    for r, o, ms in zip(rows_list, outs, mss):
        o = o * lax.rsqrt(ms + NORM_EPS) * nw_ref[...]
        o_ref[0, r, :] = (o * _silu(zg_ref[0, r, :].astype(F32))).astype(o_ref.dtype)


def _level_constants():
    idx = np.arange(CHUNK)
    x = idx[:, None] ^ idx[None, :]
    lvl = np.where(x > 0, np.floor(np.log2(np.maximum(x, 1))), 6).astype(np.int32)
    lvl_f = np.where(idx[:, None] >= idx[None, :], lvl, 7).astype(np.int32)
    lvl_b = lvl_f.T.copy()
    def stack(fwd):
        tri = ((idx[None, :] <= idx[:, None]) if fwd else (idx[None, :] >= idx[:, None])).astype(np.float32)
        mats = [tri]
        for m in (32, 16, 8, 4, 2, 1):
            ref = (idx // (2 * m)) * (2 * m) + (m - 1 if fwd else m)
            mats.append(tri - tri[ref])
        return np.concatenate(mats, axis=0)
    return lvl_f, lvl_b, stack(True), stack(False)


class _Item:
    pass


def _hgrn_kernel(zq_ref, zi_ref, zff_ref, zfb_ref, zg_ref, lbf_ref, lbb_ref, nw_ref,
                 self_ref, selb_ref, lvlf_ref, lvlb_ref, o_ref, of_scr, *, n_ctx_chunks, n_chunks):
    def group_fn(items, carry):
        st = {True: carry[0], False: carry[1]}
        its = []
        for c, fwd in items:
            it = _Item()
            it.fwd, it.rows = fwd, _rows(c)
            zf_ref, lb_ref = (zff_ref, lbf_ref) if fwd else (zfb_ref, lbb_ref)
            it.q = _silu(zq_ref[0, it.rows, :].astype(F32)) * SCALE
            it.v = zi_ref[0, it.rows, :]
            zf = zf_ref[0, it.rows, :].astype(F32)
            lb = lb_ref[...]
            logf = _log_sigmoid(zf) + jnp.log1p(lb * jnp.exp(-jnp.maximum(zf, -GATE_CLAMP)))
            it.k = (1.0 - lb) * jax.nn.sigmoid(-zf)
            it.qb, it.kb = _bf(it.q), _bf(it.k)
            it.cat = jnp.concatenate(_split2(logf * LOG2_E), axis=1)
            its.append(it)
        for it in its:
            cs = _dot((self_ref if it.fwd else selb_ref)[...], it.cat)
            it.cs = cs[:, :LANES] + cs[:, LANES:]
            it.b = it.cs[:CHUNK]
            it.lvl = (lvlf_ref if it.fwd else lvlb_ref)[...]
        for it in its:
            it.scores = jnp.where(it.lvl == 6, _dot_nt(it.qb, it.kb), 0.0)
        for n in range(6):
            for it in its:
                wgt = _bf(jnp.exp2(-jnp.abs(it.cs[(n + 1) * CHUNK:(n + 2) * CHUNK])))
                it.scores = jnp.where(it.lvl == 5 - n, _dot_nt(it.qb * wgt, it.kb * wgt), it.scores)
        for it in its:
            b_tot = it.b[CHUNK - 1:CHUNK] if it.fwd else it.b[0:1]
            it.intra = _dot(_bf(it.scores), it.v)
            it.upd = _dot_tn(it.v, _bf(it.k * jnp.exp2(b_tot - it.b)))
            it.qd = _bf(it.q * jnp.exp2(it.b))
            it.dec = jnp.exp2(b_tot)
        for it in its:
            out = it.intra + _dot_nt(it.qd, _bf(st[it.fwd]))
            st[it.fwd] = st[it.fwd] * it.dec + it.upd
            if it.fwd:
                of_scr[it.rows, :] = out
            else:
                o_ref[0, it.rows, :] = out.astype(o_ref.dtype)
        return st[True], st[False]

    def finish_fn(rows_list):
        _finish_heads(rows_list, o_ref, of_scr, zg_ref, nw_ref, centre=False)

    _scan_both(group_fn, finish_fn, jnp.zeros((HEAD_DIM, HEAD_DIM), F32), n_ctx_chunks, n_chunks, HGRN_GROUP)


def _zspec(t, base):
    return pl.BlockSpec((1, t, LANES), lambda b, h: (b, 0, base + h))


def _const_spec(shape):
    return pl.BlockSpec(shape, lambda b, h: (0,) * len(shape))


def _hgrn(z, lb_f, lb_b, norm_w, n_ctx):
    bsz, t, _ = z.shape
    lvl_f, lvl_b, sel_f, sel_b = _level_constants()
    kern = functools.partial(_hgrn_kernel, n_ctx_chunks=n_ctx // CHUNK, n_chunks=t // CHUNK)
    head_vec = pl.BlockSpec((1, LANES), lambda b, h: (0, h))
    return pl.pallas_call(
        kern,
        grid=(bsz, A_HEADS),
        in_specs=[_zspec(t, ZB_AQ), _zspec(t, ZB_AI), _zspec(t, ZB_AFF), _zspec(t, ZB_AFB), _zspec(t, ZB_AG),
                  head_vec, head_vec, _const_spec((1, LANES)),
                  _const_spec(sel_f.shape), _const_spec(sel_b.shape),
                  _const_spec(lvl_f.shape), _const_spec(lvl_b.shape)],
        out_specs=pl.BlockSpec((1, t, LANES), lambda b, h: (b, 0, h)),
        out_shape=jax.ShapeDtypeStruct((bsz, t, N_HEADS * HEAD_DIM), BF16),
        scratch_shapes=[pltpu.VMEM((t, HEAD_DIM), F32)],
        compiler_params=_cparams(2),
        name="hgrn",
    )(z, z, z, z, z, lb_f.reshape(1, A_W), lb_b.reshape(1, A_W), norm_w.reshape(1, LANES),
      jnp.asarray(sel_f, BF16), jnp.asarray(sel_b, BF16), jnp.asarray(lvl_f), jnp.asarray(lvl_b))


def _gdn_kernel(zq_ref, zk_ref, zv_ref, zg_ref, wq_ref, wk_ref, wv_ref, gc_ref, gr_ref, par_ref, nw_ref,
                lvlf_ref, lvlb_ref, merged_ref, o_ref, q_scr, k_scr, v_scr, of_scr, *, n_ctx_chunks, n_chunks):
    del merged_ref
    t_total = n_chunks * CHUNK
    ri = lax.broadcasted_iota(jnp.int32, (CHUNK, CHUNK), 0)
    ci = lax.broadcasted_iota(jnp.int32, (CHUNK, CHUNK), 1)
    row_id = lax.broadcasted_iota(jnp.int32, (CHUNK, 1), 0)
    eye = jnp.where(ri == ci, 1.0, 0.0)

    def conv(z_ref, w_ref, c):
        has_prev = jnp.logical_and(c != 0, c != n_ctx_chunks)
        has_next = jnp.logical_and(c != n_ctx_chunks - 1, c != n_chunks - 1)
        p0 = pl.multiple_of(jnp.maximum(c * CHUNK - 16, 0), 16)
        n0 = pl.multiple_of(jnp.minimum(c * CHUNK + CHUNK, t_total - 16), 16)
        x = z_ref[0, _rows(c), :].astype(F32)
        xp = z_ref[0, pl.ds(p0, 16), :].astype(F32)[15:16] * jnp.where(has_prev, 1.0, 0.0)
        xn = z_ref[0, pl.ds(n0, 16), :].astype(F32)[0:1] * jnp.where(has_next, 1.0, 0.0)
        xm1 = jnp.where(row_id == 0, xp, pltpu.roll(x, 1, 0))
        xp1 = jnp.where(row_id == CHUNK - 1, xn, pltpu.roll(x, CHUNK - 1, 0))
        w = w_ref[...]
        return _silu(xm1 * w[0:1] + x * w[1:2] + xp1 * w[2:3])

    def prep(i, carry):
        cs = [i * PREP_GROUP + j for j in range(PREP_GROUP)]
        qs = [conv(zq_ref, wq_ref, c) for c in cs]
        ks = [conv(zk_ref, wk_ref, c) for c in cs]
        q_ss = [_row_mean(a * a) * HEAD_DIM for a in qs]
        k_ss = [_row_mean(a * a) * HEAD_DIM for a in ks]
        for c, q, k, qn, kn in zip(cs, qs, ks, q_ss, k_ss):
            q_scr[_rows(c), :] = (q * (lax.rsqrt(qn + NORM_EPS) * SCALE)).astype(q_scr.dtype)
            k_scr[_rows(c), :] = (k * lax.rsqrt(kn + NORM_EPS)).astype(k_scr.dtype)
            v_scr[_rows(c), :] = conv(zv_ref, wv_ref, c).astype(v_scr.dtype)
        return carry

    lax.fori_loop(0, n_chunks // PREP_GROUP, prep, 0)

    def group_fn(items, carry):
        st = {True: carry[0], False: carry[1]}
        par = par_ref[0]
        its = []
        for c, fwd in items:
            it = _Item()
            it.fwd, it.rows = fwd, _rows(c)
            ia, ib, pr = (0, 2, 0) if fwd else (1, 3, 2)
            neg_a = -jnp.exp(par[pr:pr + 1, 0:1])
            dt = par[pr + 1:pr + 2, 0:1]
            it.q = q_scr[it.rows, :]
            it.k = k_scr[it.rows, :]
            it.v = v_scr[it.rows, :].astype(F32)
            g_col = neg_a * _softplus(gc_ref[0, 0, it.rows, ia:ia + 1] + dt)
            g_row = neg_a * _softplus(gr_ref[0, 0, ia, pl.ds(c, 1), :] + dt)
            it.beta = jax.nn.sigmoid(gc_ref[0, 0, it.rows, ib:ib + 1])
            incl = (ci <= ri) if fwd else (ci >= ri)
            incl_t = (ri <= ci) if fwd else (ri >= ci)
            it.bc = jnp.sum(jnp.where(incl, g_row, 0.0), axis=1, keepdims=True)
            bc_row = jnp.sum(jnp.where(incl_t, g_col, 0.0), axis=0, keepdims=True)
            it.lmat = jnp.where(incl, jnp.exp(jnp.minimum(it.bc - bc_row, 0.0)), 0.0)
            it.lvl = (lvlf_ref if fwd else lvlb_ref)[...]
            its.append(it)
        for it in its:
            it.amat = it.beta * _dot_nt(it.k, it.k) * it.lmat
            it.qk = _bf(_dot_nt(it.q, it.k) * it.lmat)
        for it in its:
            it.x = eye - jnp.where(it.lvl == 0, it.amat, 0.0)
        for lev in range(1, 6):
            for it in its:
                it.xb = _bf(it.x)
                it.y = _dot(_bf(jnp.where(it.lvl == lev, it.amat, 0.0)), it.xb)
            for it in its:
                it.x = it.x - _dot(it.xb, _bf(it.y))
        for it in its:
            e_col = jnp.exp(it.bc)
            kf = it.k.astype(F32)
            sol = _dot(_bf(it.x), _bf(jnp.concatenate([it.v * it.beta, kf * (it.beta * e_col)], axis=1)))
            it.u = _bf(sol[:, :HEAD_DIM])
            it.w = _bf(sol[:, HEAD_DIM:])
            bc_tot = it.bc[CHUNK - 1:CHUNK] if it.fwd else it.bc[0:1]
            it.qd = it.q.astype(F32) * e_col
            it.kd = _bf(kf * jnp.exp(bc_tot - it.bc))
            it.last = jnp.exp(bc_tot)
        for it in its:
            it.kw = _bf(_dot_tn(it.kd, it.w))
            it.ku = _dot_tn(it.kd, it.u)
            it.qs = _bf(it.qd - _dot(it.qk, it.w))
            it.o0 = _dot(it.qk, it.u)
        for it in its:
            s = st[it.fwd]
            sb = _bf(s)
            out = it.o0 + _dot(it.qs, sb)
            st[it.fwd] = it.last * s + (it.ku - _dot(it.kw, sb))
            if it.fwd:
                of_scr[it.rows, :] = out
            else:
                o_ref[0, it.rows, :] = out.astype(o_ref.dtype)
        return st[True], st[False]

    def finish_fn(rows_list):
        _finish_heads(rows_list, o_ref, of_scr, zg_ref, nw_ref, centre=False)

    _scan_both(group_fn, finish_fn, jnp.zeros((HEAD_DIM, HEAD_DIM), F32), n_ctx_chunks, n_chunks, GDN_GROUP)


def _gdn(z, zgate, merged, conv_w, a_log_f, a_log_b, dt_f, dt_b, norm_w, n_ctx):
    bsz, t, _ = z.shape
    n = t // CHUNK
    g = zgate[:, :, :N_GATES].reshape(bsz, t, 4, B_HEADS)
    gate_cols = g.transpose(0, 3, 1, 2)
    gate_rows = g.reshape(bsz, n, CHUNK, 4, B_HEADS).transpose(0, 4, 3, 1, 2)
    par = jnp.stack([a_log_f, dt_f, a_log_b, dt_b] + [jnp.zeros_like(dt_f)] * 4, axis=1)
    par = jnp.broadcast_to(par[:, :, None], (B_HEADS, 8, LANES)).astype(F32)
    lvl_f, lvl_b, _, _ = _level_constants()
    kern = functools.partial(_gdn_kernel, n_ctx_chunks=n_ctx // CHUNK, n_chunks=n)

    def wspec(base):
        return pl.BlockSpec((3, LANES), lambda b, h: (0, base + h))

    return pl.pallas_call(
        kern,
        grid=(bsz, B_HEADS),
        in_specs=[_zspec(t, ZB_BQ), _zspec(t, ZB_BK), _zspec(t, ZB_BV), _zspec(t, ZB_BG),
                  wspec(0), wspec(B_HEADS), wspec(2 * B_HEADS),
                  pl.BlockSpec((1, 1, t, 4), lambda b, h: (b, h, 0, 0)),
                  pl.BlockSpec((1, 1, 4, n, CHUNK), lambda b, h: (b, h, 0, 0, 0)),
                  pl.BlockSpec((1, 8, LANES), lambda b, h: (h, 0, 0)),
                  _const_spec((1, LANES)), _const_spec(lvl_f.shape), _const_spec(lvl_b.shape),
                  pl.BlockSpec(memory_space=pl.ANY)],
        out_specs=pl.BlockSpec((1, t, LANES), lambda b, h: (b, 0, A_HEADS + h)),
        out_shape=jax.ShapeDtypeStruct(merged.shape, BF16),
        input_output_aliases={13: 0},
        scratch_shapes=[pltpu.VMEM((t, HEAD_DIM), BF16)] * 3 + [pltpu.VMEM((t, HEAD_DIM), F32)],
        compiler_params=_cparams(2),
        name="gdn",
    )(z, z, z, z, conv_w, conv_w, conv_w, gate_cols, gate_rows, par, norm_w.reshape(1, LANES),
      jnp.asarray(lvl_f), jnp.asarray(lvl_b), merged)


def _ret_kernel(zq_ref, zk_ref, zv_ref, zg_ref, base_ref, loc_ref, dec_ref, dmat_ref, cdec_ref, nw_ref,
                merged_ref, o_ref, of_scr, *, n_ctx_chunks, n_chunks):
    del merged_ref
    def group_fn(items, carry):
        st = {True: carry[0], False: carry[1]}
        its = []
        for c, fwd in items:
            it = _Item()
            it.fwd, it.rows, it.d = fwd, _rows(c), 0 if fwd else 1
            d = it.d
            cb = base_ref[2 * d, pl.ds(c, 1), :]
            sb = base_ref[2 * d + 1, pl.ds(c, 1), :]
            cl, sl, cls, sls = (loc_ref[4 * d + n] for n in range(4))
            cos = cb * cl - sb * sl
            sin = sb * cls + cb * sls

            def rot(a):
                return a * cos + pltpu.roll(a, HEAD_DIM // 2, 1) * sin

            q = rot(zq_ref[0, it.rows, :].astype(F32) * SCALE)
            k = rot(zk_ref[0, it.rows, :].astype(F32))
            it.q, it.k = _bf(q), _bf(k)
            it.qd = _bf(q * dec_ref[0, 2 * d])
            it.kd = _bf(k * dec_ref[0, 2 * d + 1])
            it.v = zv_ref[0, it.rows, :]
            its.append(it)
        for it in its:
            it.scores = _bf(_dot_nt(it.q, it.k) * dmat_ref[0, it.d])
        for it in its:
            it.intra = _dot(it.scores, it.v)
            it.upd = _dot_tn(it.v, it.kd)
        for it in its:
            out = it.intra + _dot_nt(it.qd, _bf(st[it.fwd]))
            st[it.fwd] = st[it.fwd] * cdec_ref[0, it.d:it.d + 1, :] + it.upd
            if it.fwd:
                of_scr[it.rows, :] = out
            else:
                o_ref[0, it.rows, :] = out.astype(o_ref.dtype)
        return st[True], st[False]

    def finish_fn(rows_list):
        _finish_heads(rows_list, o_ref, of_scr, zg_ref, nw_ref, centre=True)

    _scan_both(group_fn, finish_fn, jnp.zeros((HEAD_DIM, HEAD_DIM), F32), n_ctx_chunks, n_chunks, RET_GROUP)


def _retention_tables(n_ctx, t):
    half = HEAD_DIM // 2
    n = t // CHUNK
    nc = n_ctx // CHUNK
    inv = ROPE_BASE ** (-jnp.arange(half, dtype=F32) / half)
    inv = jnp.concatenate([inv, inv])
    sgn = jnp.concatenate([-jnp.ones((half,), F32), jnp.ones((half,), F32)])
    c_idx = np.arange(n)
    base_f = (c_idx * CHUNK).astype(np.float32)
    base_b = np.where(c_idx < nc, n_ctx - CHUNK * (c_idx + 1), t - CHUNK * (c_idx - nc + 1)).astype(np.float32)
    loc_f = np.arange(CHUNK, dtype=np.float32)
    loc_b = loc_f[::-1].copy()

    def ang(p):
        return jnp.asarray(p)[:, None] * inv[None, :]

    base = jnp.stack([jnp.cos(ang(base_f)), jnp.sin(ang(base_f)), jnp.cos(ang(base_b)), jnp.sin(ang(base_b))])
    loc = jnp.stack([jnp.cos(ang(loc_f)), jnp.sin(ang(loc_f)), jnp.cos(ang(loc_f)) * sgn, jnp.sin(ang(loc_f)) * sgn,
                     jnp.cos(ang(loc_b)), jnp.sin(ang(loc_b)), jnp.cos(ang(loc_b)) * sgn, jnp.sin(ang(loc_b)) * sgn])
    lg_f = jnp.log1p(-jnp.exp2(-5.0 - jnp.arange(C_HEADS, dtype=F32)))
    lg_b = lg_f[::-1]
    idx = jnp.arange(CHUNK, dtype=F32)
    rel = idx[:, None] - idx[None, :]

    def masked_exp(a, mask):
        return jnp.where(mask, jnp.exp(jnp.where(mask, a, 0.0)), 0.0)

    dmat_f = masked_exp(lg_f[:, None, None] * rel, (rel >= 0)[None])
    dmat_b = masked_exp(-lg_b[:, None, None] * rel, (rel <= 0)[None])
    dq_f = jnp.exp(lg_f[:, None] * (idx + 1.0))
    dk_f = jnp.exp(lg_f[:, None] * (CHUNK - 1.0 - idx))
    dq_b = jnp.exp(lg_b[:, None] * (CHUNK - idx))
    dk_b = jnp.exp(lg_b[:, None] * idx)
    dec = jnp.stack([dq_f, dk_f, dq_b, dk_b], axis=1)
    dec = jnp.broadcast_to(dec[..., None], (C_HEADS, 4, CHUNK, LANES))
    cdec = jnp.stack([jnp.exp(lg_f * CHUNK), jnp.exp(lg_b * CHUNK)] + [jnp.zeros((C_HEADS,), F32)] * 6, axis=1)
    cdec = jnp.broadcast_to(cdec[..., None], (C_HEADS, 8, LANES))
    return base, loc, dec, jnp.stack([dmat_f, dmat_b], axis=1), cdec


def _retention(z, merged, norm_w, n_ctx):
    bsz, t, _ = z.shape
    n = t // CHUNK
    base, loc, dec, dmat, cdec = _retention_tables(n_ctx, t)
    kern = functools.partial(_ret_kernel, n_ctx_chunks=n_ctx // CHUNK, n_chunks=n)
    return pl.pallas_call(
        kern,
        grid=(bsz, C_HEADS),
        in_specs=[_zspec(t, ZB_CQ), _zspec(t, ZB_CK), _zspec(t, ZB_CV), _zspec(t, ZB_CG),
                  _const_spec(base.shape), _const_spec(loc.shape),
                  pl.BlockSpec((1, 4, CHUNK, LANES), lambda b, h: (h, 0, 0, 0)),
                  pl.BlockSpec((1, 2, CHUNK, CHUNK), lambda b, h: (h, 0, 0, 0)),
                  pl.BlockSpec((1, 8, LANES), lambda b, h: (h, 0, 0)),
                  pl.BlockSpec((1, LANES), lambda b, h: (0, h)),
                  pl.BlockSpec(memory_space=pl.ANY)],
        out_specs=pl.BlockSpec((1, t, LANES), lambda b, h: (b, 0, A_HEADS + B_HEADS + h)),
        out_shape=jax.ShapeDtypeStruct(merged.shape, BF16),
        input_output_aliases={10: 0},
        scratch_shapes=[pltpu.VMEM((t, HEAD_DIM), F32)],
        compiler_params=_cparams(2),
        name="retention",
    )(z, z, z, z, base, loc, dec, dmat, cdec, norm_w.reshape(1, C_W), merged)


def _reorder_in_proj(w):
    g0 = 5 * A_W + 4 * B_W
    main = jnp.concatenate([w[:, :g0], w[:, g0 + N_GATES:]], axis=1).astype(BF16)
    gates = jnp.pad(w[:, g0:g0 + N_GATES], ((0, 0), (0, LANES - N_GATES))).astype(BF16)
    return main, gates


def _lower_bounds(logits):
    p = jax.nn.softmax(logits.astype(F32), axis=0)
    return jnp.cumsum(p, axis=0) - p[0]


def kernel(x, c, ctx, c_ctx, w_mod, b_mod, norm1_w, norm2_w, w_in, qkv_conv_w, hgrn_lb_logits_fwd, hgrn_lb_logits_bwd, hgrn_norm_w, gdn_a_log_fwd, gdn_a_log_bwd, gdn_dt_bias_fwd, gdn_dt_bias_bwd, gdn_norm_w, ret_norm_w, w_out, ffn_w_up, ffn_conv_w, ffn_conv_b, ffn_w_down, final_norm_w):
    bsz, n_lat, d = x.shape
    n_ctx = ctx.shape[1]
    depth = w_mod.shape[0]
    t = n_ctx + n_lat
    assert n_ctx % ROW_TILE == 0 and n_lat % ROW_TILE == 0 and bsz < MOD_ROWS
    lb_fwd = _lower_bounds(hgrn_lb_logits_fwd)
    lb_bwd = _lower_bounds(hgrn_lb_logits_bwd)

    rows = jnp.concatenate([c, c_ctx[None, :], jnp.zeros((MOD_ROWS - bsz - 1, d), F32)], axis=0)
    mods = _modulation(rows, w_mod, b_mod)

    tm = 512
    m_rows = bsz * t
    assert m_rows % tm == 0
    f2 = ffn_w_up.shape[2]
    mod = [mods[l].reshape(MOD_ROWS * N_MOD, 1, d) for l in range(depth)]
    h = _norm_mod(ctx, x, norm1_w[0], mod[0], 1, 0)
    xs = x
    for l in range(depth):
        w_main, w_gate = _reorder_in_proj(w_in[l])
        h = h.reshape(m_rows, d)
        z = _matmul(h, w_main, tm, 23 * LANES, BF16, "in_proj").reshape(bsz, t, Z_BLOCKS * LANES)
        zgate = _matmul(h, w_gate, tm, LANES, F32, "in_proj_gates").reshape(bsz, t, LANES)
        merged = _hgrn(z, lb_fwd[l], lb_bwd[l], hgrn_norm_w[l], n_ctx)
        merged = _gdn(z, zgate, merged, qkv_conv_w[l], gdn_a_log_fwd[l], gdn_a_log_bwd[l],
                      gdn_dt_bias_fwd[l], gdn_dt_bias_bwd[l], gdn_norm_w[l], n_ctx)
        merged = _retention(z, merged, ret_norm_w[l], n_ctx)
        xs, h = _out_proj(merged, w_out[l].astype(BF16), ctx if l == 0 else None, xs, mod[l], 2,
                          norm2_w[l], 4, 3, n_ctx)

        uv = _matmul(h.reshape(m_rows, d), ffn_w_up[l].astype(BF16), tm, f2 // 4, BF16, "ffn_up").reshape(bsz, t, f2)
        w_down = ffn_w_down[l].astype(BF16)
        if l + 1 < depth:
            xs, h = _ffn_tail(uv, ffn_conv_w[l], ffn_conv_b[l], w_down, xs, mod[l], 5,
                              norm1_w[l + 1], mod[l + 1], 1, 0, n_ctx, 512)
    return _ffn_tail(uv, ffn_conv_w[-1], ffn_conv_b[-1], w_down, xs, mod[-1], 5,
                     final_norm_w, None, None, None, n_ctx, 512)
```

```python
import functools
import math

import numpy as np
import jax
import jax.numpy as jnp
from jax import lax
from jax.experimental import pallas as pl
from jax.experimental.pallas import tpu as pltpu

F32 = jnp.float32
BF16 = jnp.bfloat16

HEAD_DIM = 128
N_HEADS = 16
A_HEADS = 5
C_HEADS = 5
B_HEADS = 6
A_W = A_HEADS * HEAD_DIM
B_W = B_HEADS * HEAD_DIM
C_W = C_HEADS * HEAD_DIM
N_GATES = 4 * B_HEADS
CHUNK = 64
GRID_W = 64
ROPE_BASE = 10000.0
NORM_EPS = 1e-6
N_MOD = 6
GATE_CLAMP = 60.0
SCALE = HEAD_DIM ** -0.5
LOG2_E = 1.4426950408889634
LANES = 128
MOD_ROWS = 8
ROW_TILE = 256
VMEM_LIMIT = 56 * 1024 * 1024
HGRN_GROUP, HGRN_WAVES = 6, 1
GDN_GROUP, GDN_WAVES = 12, 12
RET_GROUP, RET_WAVES = 12, 12
PREP_GROUP = 4

ZB_AQ, ZB_AI, ZB_AFF, ZB_AFB, ZB_AG = 0, 5, 10, 15, 20
ZB_BQ, ZB_BK, ZB_BV, ZB_BG = 25, 31, 37, 43
ZB_CQ, ZB_CK, ZB_CV, ZB_CG = 49, 54, 59, 64
Z_BLOCKS = 69


def _cparams(n_axes):
    return pltpu.CompilerParams(dimension_semantics=("arbitrary",) * n_axes,
                                vmem_limit_bytes=VMEM_LIMIT)


def _dot(a, b):
    return jnp.dot(a, b, preferred_element_type=F32)


def _dot_nt(a, b):
    return lax.dot_general(a, b, (((1,), (1,)), ((), ())), preferred_element_type=F32)


def _dot_tn(a, b):
    return lax.dot_general(a, b, (((0,), (0,)), ((), ())), preferred_element_type=F32)


def _bf(a):
    return a.astype(BF16)


def _split2(a):
    hi = a.astype(BF16)
    lo = (a - hi.astype(F32)).astype(BF16)
    return hi, lo


def _dot3(a, b):
    ah, al = _split2(a)
    bh, bl = _split2(b)
    return _dot(ah, bh) + (_dot(al, bh) + _dot(ah, bl))


def _row_mean(a):
    hi, lo = _split2(a)
    ones = jnp.ones((a.shape[1], a.shape[1]), BF16)
    return (_dot(hi, ones) + _dot(lo, ones)) * (1.0 / a.shape[1])


def _silu(a):
    return a * jax.nn.sigmoid(a)


def _softplus(a):
    return jnp.maximum(a, 0.0) + jnp.log1p(jnp.exp(-jnp.abs(a)))


def _log_sigmoid(a):
    return jnp.minimum(a, 0.0) - jnp.log1p(jnp.exp(-jnp.abs(a)))


def _mod_kernel(a_ref, w_ref, b_ref, o_ref):
    a = _silu(a_ref[...])
    o_ref[0] = _dot3(a, w_ref[0]) + b_ref[0]


def _modulation(rows, w_mod, b_mod):
    depth, d, n = w_mod.shape
    tn = 1024
    return pl.pallas_call(
        _mod_kernel,
        grid=(depth, n // tn),
        in_specs=[pl.BlockSpec((MOD_ROWS, d), lambda l, j: (0, 0)),
                  pl.BlockSpec((1, d, tn), lambda l, j: (l, 0, j)),
                  pl.BlockSpec((1, 1, tn), lambda l, j: (l, 0, j))],
        out_specs=pl.BlockSpec((1, MOD_ROWS, tn), lambda l, j: (l, 0, j)),
        out_shape=jax.ShapeDtypeStruct((depth, MOD_ROWS, n), F32),
        compiler_params=_cparams(2),
        name="modulation",
    )(rows, w_mod, b_mod.reshape(depth, 1, n))


def _mod_row(b, i, n_ctx_tiles, n_batch, k):
    return (jnp.where(i < n_ctx_tiles, n_batch, b) * N_MOD + k, 0, 0)


def _split_stream_specs(n_ctx_tiles, d):
    return [pl.BlockSpec((1, ROW_TILE, d), lambda b, i: (b, jnp.minimum(i, n_ctx_tiles - 1), 0)),
            pl.BlockSpec((1, ROW_TILE, d), lambda b, i: (b, jnp.maximum(i - n_ctx_tiles, 0), 0))]


def _for_stream_tile(n_ctx_tiles, ctx_ref, lat_ref, body):
    i = pl.program_id(1)

    @pl.when(i < n_ctx_tiles)
    def _():
        body(ctx_ref[0])

    @pl.when(i >= n_ctx_tiles)
    def _():
        body(lat_ref[0])


def _norm_mod_kernel(ctx_ref, lat_ref, w_ref, sc_ref, sh_ref, o_ref, *, n_ctx_tiles):
    def body(x):
        o_ref[0] = _rms_mod(x, w_ref[...], sc_ref[0], sh_ref[0]).astype(o_ref.dtype)

    _for_stream_tile(n_ctx_tiles, ctx_ref, lat_ref, body)


def _norm_mod(ctx, x, w, mod, k_scale, k_shift):
    bsz, n_ctx, d = ctx.shape
    t = n_ctx + x.shape[1]
    nct = n_ctx // ROW_TILE
    return pl.pallas_call(
        functools.partial(_norm_mod_kernel, n_ctx_tiles=nct),
        grid=(bsz, t // ROW_TILE),
        in_specs=_split_stream_specs(nct, d) + [
            pl.BlockSpec((1, d), lambda b, i: (0, 0)),
            pl.BlockSpec((1, 1, d), lambda b, i: _mod_row(b, i, nct, bsz, k_scale)),
            pl.BlockSpec((1, 1, d), lambda b, i: _mod_row(b, i, nct, bsz, k_shift))],
        out_specs=pl.BlockSpec((1, ROW_TILE, d), lambda b, i: (b, i, 0)),
        out_shape=jax.ShapeDtypeStruct((bsz, t, d), BF16),
        compiler_params=_cparams(2),
        name="norm_mod",
    )(ctx, x, w.reshape(1, d), mod, mod)


def _mm_kernel(a_ref, w_ref, o_ref):
    o_ref[...] = _dot(a_ref[...], w_ref[...]).astype(o_ref.dtype)


def _matmul(a, w, tm, tn, out_dtype, name):
    m, k = a.shape
    n = w.shape[1]
    return pl.pallas_call(
        _mm_kernel,
        grid=(n // tn, m // tm),
        in_specs=[pl.BlockSpec((tm, k), lambda j, i: (i, 0)),
                  pl.BlockSpec((k, tn), lambda j, i: (0, j))],
        out_specs=pl.BlockSpec((tm, tn), lambda j, i: (i, j)),
        out_shape=jax.ShapeDtypeStruct((m, n), out_dtype),
        compiler_params=_cparams(2),
        name=name,
    )(a, w)


def _rms_mod(x, w, scale, shift):
    return (x * lax.rsqrt(jnp.mean(x * x, axis=-1, keepdims=True) + NORM_EPS) * w) * (1.0 + scale) + shift


def _out_proj_kernel(a_ref, w_ref, g_ref, nw_ref, sc_ref, sh_ref, *rest, n_ctx_tiles, split):
    def body(x):
        x = x + g_ref[0] * _dot(a_ref[0], w_ref[...])
        o_ref[0] = x
        h_ref[0] = _rms_mod(x, nw_ref[...], sc_ref[0], sh_ref[0]).astype(h_ref.dtype)

    if split:
        ctx_ref, lat_ref, o_ref, h_ref = rest
        _for_stream_tile(n_ctx_tiles, ctx_ref, lat_ref, body)
    else:
        x_ref, o_ref, h_ref = rest
        body(x_ref[0])


def _out_proj(a, w, ctx, x, mod, k_gate, norm_w, k_scale, k_shift, n_ctx):
    bsz, t, k = a.shape
    d = w.shape[1]
    nct = n_ctx // ROW_TILE
    split = ctx is not None

    def mod_spec(kk):
        return pl.BlockSpec((1, 1, d), lambda b, i: _mod_row(b, i, nct, bsz, kk))

    row_spec = pl.BlockSpec((1, ROW_TILE, d), lambda b, i: (b, i, 0))
    in_specs = [pl.BlockSpec((1, ROW_TILE, k), lambda b, i: (b, i, 0)),
                pl.BlockSpec((k, d), lambda b, i: (0, 0)),
                mod_spec(k_gate), pl.BlockSpec((1, d), lambda b, i: (0, 0)), mod_spec(k_scale), mod_spec(k_shift)]
    in_specs += _split_stream_specs(nct, d) if split else [row_spec]
    return pl.pallas_call(
        functools.partial(_out_proj_kernel, n_ctx_tiles=nct, split=split),
        grid=(bsz, t // ROW_TILE),
        in_specs=in_specs,
        out_specs=[row_spec, row_spec],
        out_shape=[jax.ShapeDtypeStruct((bsz, t, d), F32), jax.ShapeDtypeStruct((bsz, t, d), BF16)],
        input_output_aliases={} if split else {6: 0},
        compiler_params=_cparams(2),
        name="out_proj",
    )(a, w, mod, norm_w.reshape(1, d), mod, mod, *((ctx, x) if split else (x,)))


def _ffn_tail_kernel(up_ref, um_ref, un_ref, v_ref, cw_ref, cb_ref, shift_ref, wd_ref, x_ref, g_ref, nw_ref, *rest,
                     n_ctx_tiles, n_tiles, tile_offset, tc, last):
    if last:
        (o_ref,) = rest
    else:
        sc_ref, sh_ref, o_ref, h_ref = rest
    i = pl.program_id(1) + tile_offset
    is_ctx = i < n_ctx_tiles
    has_up = jnp.logical_and(jnp.logical_not(is_ctx), i > n_ctx_tiles)
    has_dn = jnp.logical_and(jnp.logical_not(is_ctx), i < n_tiles - 1)
    row_on = jnp.where(is_ctx, 0.0, 1.0)
    up_on = jnp.where(has_up, 1.0, 0.0)
    dn_on = jnp.where(has_dn, 1.0, 0.0)
    shift = shift_ref[jnp.where(is_ctx, 0, 1)]
    acc = None
    for j in range(um_ref.shape[2] // tc):
        cols = slice(j * tc, (j + 1) * tc)
        main = um_ref[0, :, cols].astype(F32)
        prev = up_ref[0, :, cols].astype(F32) * up_on
        nxt = un_ref[0, :, cols].astype(F32) * dn_on
        up = jnp.concatenate([prev, main[:ROW_TILE - GRID_W]], axis=0)
        dn = jnp.concatenate([main[GRID_W:], nxt], axis=0)
        w = cw_ref[:, cols]
        wu = w[0:3] * row_on
        wc = w[3:6]
        wd = w[6:9] * row_on

        def col(kw):
            return up * wu[kw:kw + 1] + main * wc[kw:kw + 1] + dn * wd[kw:kw + 1]

        sides = _dot(shift, jnp.concatenate([_bf(col(0)), _bf(col(2))], axis=0))
        u = sides + col(1) + cb_ref[:, cols]
        act = _bf(_silu(u)) * v_ref[0, :, cols]
        part = _dot(act, wd_ref[cols, :])
        acc = part if acc is None else acc + part
    x = x_ref[0] + g_ref[0] * acc
    if last:
        o_ref[0] = x * lax.rsqrt(jnp.mean(x * x, axis=-1, keepdims=True) + NORM_EPS) * nw_ref[...]
    else:
        o_ref[0] = x
        h_ref[0] = _rms_mod(x, nw_ref[...], sc_ref[0], sh_ref[0]).astype(h_ref.dtype)


def _shift_constants(n_ctx):
    r = np.arange(ROW_TILE)
    out = np.zeros((2, ROW_TILE, 2 * ROW_TILE), np.float32)
    for kind, width in enumerate((n_ctx, GRID_W)):
        has_left = r % width != 0
        has_right = r % width != width - 1
        out[kind, r[has_left], r[has_left] - 1] = 1.0
        out[kind, r[has_right], ROW_TILE + r[has_right] + 1] = 1.0
    return out


def _ffn_tail(uv, conv_w, conv_b, w_down, x, mod, k_gate, norm_w, mod_next, k_scale, k_shift, n_ctx, tc):
    bsz, t, f2 = uv.shape
    f = f2 // 2
    d = w_down.shape[1]
    assert n_ctx == ROW_TILE and ROW_TILE % GRID_W == 0 and f % tc == 0
    last = mod_next is None
    nct = n_ctx // ROW_TILE
    nt = t // ROW_TILE
    off = nct if last else 0
    per = ROW_TILE // GRID_W
    nrow = t // GRID_W
    kern = functools.partial(_ffn_tail_kernel, n_ctx_tiles=nct, n_tiles=nt, tile_offset=off, tc=tc, last=last)

    def mod_spec(kk):
        return pl.BlockSpec((1, 1, d), lambda b, i: _mod_row(b, i + off, nct, bsz, kk))

    row_spec = pl.BlockSpec((1, ROW_TILE, d), lambda b, i: (b, i, 0))
    in_specs = [pl.BlockSpec((1, GRID_W, f), lambda b, i: (b, jnp.maximum((i + off) * per - 1, 0), 0)),
                pl.BlockSpec((1, ROW_TILE, f), lambda b, i: (b, i + off, 0)),
                pl.BlockSpec((1, GRID_W, f), lambda b, i: (b, jnp.minimum((i + off) * per + per, nrow - 1), 0)),
                pl.BlockSpec((1, ROW_TILE, f), lambda b, i: (b, i + off, 1)),
                pl.BlockSpec((9, f), lambda b, i: (0, 0)),
                pl.BlockSpec((1, f), lambda b, i: (0, 0)),
                pl.BlockSpec((2, ROW_TILE, 2 * ROW_TILE), lambda b, i: (0, 0, 0)),
                pl.BlockSpec((f, d), lambda b, i: (0, 0), pipeline_mode=pl.Buffered(1)),
                pl.BlockSpec((1, ROW_TILE, d), lambda b, i: (b, i + off, 0)),
                mod_spec(k_gate),
                pl.BlockSpec((1, d), lambda b, i: (0, 0))]
    args = [uv, uv, uv, uv, conv_w.reshape(9, f), conv_b.reshape(1, f), jnp.asarray(_shift_constants(n_ctx), BF16),
            w_down, x, mod, norm_w.reshape(1, d)]
    if last:
        return pl.pallas_call(
            kern, grid=(bsz, nt - off), in_specs=in_specs, out_specs=row_spec,
            out_shape=jax.ShapeDtypeStruct((bsz, t - n_ctx, d), F32),
            compiler_params=_cparams(2), name="ffn_tail_final",
        )(*args)
    return pl.pallas_call(
        kern, grid=(bsz, nt), in_specs=in_specs + [mod_spec(k_scale), mod_spec(k_shift)],
        out_specs=[row_spec, row_spec],
        out_shape=[jax.ShapeDtypeStruct(x.shape, F32), jax.ShapeDtypeStruct(x.shape, BF16)],
        input_output_aliases={8: 0},
        compiler_params=_cparams(2), name="ffn_tail",
    )(*args, mod_next, mod_next)


def _scan_both(group_fn, finish_fn, n_ctx_chunks, n_chunks, group, waves):
    assert n_chunks % group == 0 and group % waves == 0
    per_wave = group // waves
    init = jnp.zeros((HEAD_DIM, HEAD_DIM), F32)

    def body(i, carry):
        st = {True: carry[0], False: carry[1]}
        gens = []
        for w in range(waves):
            items = []
            for j in range(per_wave):
                p = i * group + w * per_wave + j
                items.append((p, True))
                items.append((jnp.where(p < n_ctx_chunks, n_ctx_chunks - 1 - p, n_chunks + n_ctx_chunks - 1 - p), False))
            gens.append(group_fn(items, st))
        live = set(range(waves))
        step = 0
        while live:
            for w in sorted(live):
                if step >= w:
                    try:
                        next(gens[w])
                    except StopIteration:
                        live.discard(w)
            step += 1
        return st[True], st[False]

    lax.fori_loop(0, n_chunks // group, body, (init, init))

    def finish(i, carry):
        finish_fn([_rows(i * group + j) for j in range(group)])
        return carry

    lax.fori_loop(0, n_chunks // group, finish, 0)


def _rows(c):
    return pl.ds(pl.multiple_of(c * CHUNK, CHUNK), CHUNK)


def _finish_heads(rows_list, o_ref, of_scr, zg_ref, nw_ref, centre):
    outs = [of_scr[r, :] + o_ref[0, r, :].astype(F32) for r in rows_list]
    if centre:
        mus = [_row_mean(o) for o in outs]
        outs = [o - mu for o, mu in zip(outs, mus)]
    mss = [_row_mean(o * o) for o in outs]
    for r, o, ms in zip(rows_list, outs, mss):
        o = o * lax.rsqrt(ms + NORM_EPS) * nw_ref[...]
        o_ref[0, r, :] = (o * _silu(zg_ref[0, r, :].astype(F32))).astype(o_ref.dtype)


def _level_constants():
    idx = np.arange(CHUNK)
    x = idx[:, None] ^ idx[None, :]
    lvl = np.where(x > 0, np.floor(np.log2(np.maximum(x, 1))), 6).astype(np.int32)
    lvl_f = np.where(idx[:, None] >= idx[None, :], lvl, 7).astype(np.int32)
    lvl_b = lvl_f.T.copy()
    def stack(fwd):
        tri = ((idx[None, :] <= idx[:, None]) if fwd else (idx[None, :] >= idx[:, None])).astype(np.float32)
        mats = [tri]
        for m in (32, 16, 8, 4, 2, 1):
            ref = (idx // (2 * m)) * (2 * m) + (m - 1 if fwd else m)
            mats.append(tri - tri[ref])
        return np.concatenate(mats, axis=0)
    return lvl_f, lvl_b, stack(True), stack(False)


class _Item:
    pass


def _hgrn_kernel(zq_ref, zi_ref, zff_ref, zfb_ref, zg_ref, lbf_ref, lbb_ref, nw_ref,
                 self_ref, selb_ref, lvlf_ref, lvlb_ref, o_ref, of_scr, *, n_ctx_chunks, n_chunks):
    def group_fn(items, st):
        its = []
        for c, fwd in items:
            it = _Item()
            it.fwd, it.rows = fwd, _rows(c)
            zf_ref, lb_ref = (zff_ref, lbf_ref) if fwd else (zfb_ref, lbb_ref)
            it.q = _silu(zq_ref[0, it.rows, :].astype(F32)) * SCALE
            it.v = zi_ref[0, it.rows, :]
            zf = zf_ref[0, it.rows, :].astype(F32)
            lb = lb_ref[...]
            logf = _log_sigmoid(zf) + jnp.log1p(lb * jnp.exp(-jnp.maximum(zf, -GATE_CLAMP)))
            it.k = (1.0 - lb) * jax.nn.sigmoid(-zf)
            it.qb, it.kb = _bf(it.q), _bf(it.k)
            it.cat = jnp.concatenate(_split2(logf * LOG2_E), axis=1)
            its.append(it)
        yield
        for it in its:
            cs = _dot((self_ref if it.fwd else selb_ref)[...], it.cat)
            it.cs = cs[:, :LANES] + cs[:, LANES:]
            it.b = it.cs[:CHUNK]
            it.lvl = (lvlf_ref if it.fwd else lvlb_ref)[...]
        yield
        for it in its:
            it.scores = jnp.where(it.lvl == 6, _dot_nt(it.qb, it.kb), 0.0)
        for n in range(6):
            yield
            for it in its:
                wgt = _bf(jnp.exp2(-jnp.abs(it.cs[(n + 1) * CHUNK:(n + 2) * CHUNK])))
                it.scores = jnp.where(it.lvl == 5 - n, _dot_nt(it.qb * wgt, it.kb * wgt), it.scores)
        yield
        for it in its:
            b_tot = it.b[CHUNK - 1:CHUNK] if it.fwd else it.b[0:1]
            it.intra = _dot(_bf(it.scores), it.v)
            it.upd = _dot_tn(it.v, _bf(it.k * jnp.exp2(b_tot - it.b)))
            it.qd = _bf(it.q * jnp.exp2(it.b))
            it.dec = jnp.exp2(b_tot)
        yield
        for it in its:
            out = it.intra + _dot_nt(it.qd, _bf(st[it.fwd]))
            st[it.fwd] = st[it.fwd] * it.dec + it.upd
            if it.fwd:
                of_scr[it.rows, :] = out
            else:
                o_ref[0, it.rows, :] = out.astype(o_ref.dtype)

    def finish_fn(rows_list):
        _finish_heads(rows_list, o_ref, of_scr, zg_ref, nw_ref, centre=False)

    _scan_both(group_fn, finish_fn, n_ctx_chunks, n_chunks, HGRN_GROUP, HGRN_WAVES)


def _zspec(t, base):
    return pl.BlockSpec((1, t, LANES), lambda b, h: (b, 0, base + h))


def _const_spec(shape):
    return pl.BlockSpec(shape, lambda b, h: (0,) * len(shape))


def _hgrn(z, lb_f, lb_b, norm_w, n_ctx):
    bsz, t, _ = z.shape
    lvl_f, lvl_b, sel_f, sel_b = _level_constants()
    kern = functools.partial(_hgrn_kernel, n_ctx_chunks=n_ctx // CHUNK, n_chunks=t // CHUNK)
    head_vec = pl.BlockSpec((1, LANES), lambda b, h: (0, h))
    return pl.pallas_call(
        kern,
        grid=(bsz, A_HEADS),
        in_specs=[_zspec(t, ZB_AQ), _zspec(t, ZB_AI), _zspec(t, ZB_AFF), _zspec(t, ZB_AFB), _zspec(t, ZB_AG),
                  head_vec, head_vec, _const_spec((1, LANES)),
                  _const_spec(sel_f.shape), _const_spec(sel_b.shape),
                  _const_spec(lvl_f.shape), _const_spec(lvl_b.shape)],
        out_specs=pl.BlockSpec((1, t, LANES), lambda b, h: (b, 0, h)),
        out_shape=jax.ShapeDtypeStruct((bsz, t, N_HEADS * HEAD_DIM), BF16),
        scratch_shapes=[pltpu.VMEM((t, HEAD_DIM), F32)],
        compiler_params=_cparams(2),
        name="hgrn",
    )(z, z, z, z, z, lb_f.reshape(1, A_W), lb_b.reshape(1, A_W), norm_w.reshape(1, LANES),
      jnp.asarray(sel_f, BF16), jnp.asarray(sel_b, BF16), jnp.asarray(lvl_f), jnp.asarray(lvl_b))


def _gdn_kernel(zq_ref, zk_ref, zv_ref, zg_ref, wq_ref, wk_ref, wv_ref, gc_ref, gr_ref, par_ref, nw_ref,
                lvlf_ref, lvlb_ref, merged_ref, o_ref, q_scr, k_scr, v_scr, of_scr, *, n_ctx_chunks, n_chunks):
    del merged_ref
    t_total = n_chunks * CHUNK
    ri = lax.broadcasted_iota(jnp.int32, (CHUNK, CHUNK), 0)
    ci = lax.broadcasted_iota(jnp.int32, (CHUNK, CHUNK), 1)
    row_id = lax.broadcasted_iota(jnp.int32, (CHUNK, 1), 0)
    eye = jnp.where(ri == ci, 1.0, 0.0)

    def conv(z_ref, w_ref, c):
        has_prev = jnp.logical_and(c != 0, c != n_ctx_chunks)
        has_next = jnp.logical_and(c != n_ctx_chunks - 1, c != n_chunks - 1)
        p0 = pl.multiple_of(jnp.maximum(c * CHUNK - 16, 0), 16)
        n0 = pl.multiple_of(jnp.minimum(c * CHUNK + CHUNK, t_total - 16), 16)
        x = z_ref[0, _rows(c), :].astype(F32)
        xp = z_ref[0, pl.ds(p0, 16), :].astype(F32)[15:16] * jnp.where(has_prev, 1.0, 0.0)
        xn = z_ref[0, pl.ds(n0, 16), :].astype(F32)[0:1] * jnp.where(has_next, 1.0, 0.0)
        xm1 = jnp.where(row_id == 0, xp, pltpu.roll(x, 1, 0))
        xp1 = jnp.where(row_id == CHUNK - 1, xn, pltpu.roll(x, CHUNK - 1, 0))
        w = w_ref[...]
        return _silu(xm1 * w[0:1] + x * w[1:2] + xp1 * w[2:3])

    def prep(i, carry):
        cs = [i * PREP_GROUP + j for j in range(PREP_GROUP)]
        qs = [conv(zq_ref, wq_ref, c) for c in cs]
        ks = [conv(zk_ref, wk_ref, c) for c in cs]
        q_ss = [_row_mean(a * a) * HEAD_DIM for a in qs]
        k_ss = [_row_mean(a * a) * HEAD_DIM for a in ks]
        for c, q, k, qn, kn in zip(cs, qs, ks, q_ss, k_ss):
            q_scr[_rows(c), :] = (q * (lax.rsqrt(qn + NORM_EPS) * SCALE)).astype(q_scr.dtype)
            k_scr[_rows(c), :] = (k * lax.rsqrt(kn + NORM_EPS)).astype(k_scr.dtype)
            v_scr[_rows(c), :] = conv(zv_ref, wv_ref, c).astype(v_scr.dtype)
        return carry

    lax.fori_loop(0, n_chunks // PREP_GROUP, prep, 0)

    def group_fn(items, st):
        par = par_ref[0]
        its = []
        for c, fwd in items:
            it = _Item()
            it.fwd, it.rows = fwd, _rows(c)
            ia, ib, pr = (0, 2, 0) if fwd else (1, 3, 2)
            neg_a = -jnp.exp(par[pr:pr + 1, 0:1])
            dt = par[pr + 1:pr + 2, 0:1]
            it.q = q_scr[it.rows, :]
            it.k = k_scr[it.rows, :]
            it.v = v_scr[it.rows, :].astype(F32)
            g_col = neg_a * _softplus(gc_ref[0, 0, it.rows, ia:ia + 1] + dt)
            g_row = neg_a * _softplus(gr_ref[0, 0, ia, pl.ds(c, 1), :] + dt)
            it.beta = jax.nn.sigmoid(gc_ref[0, 0, it.rows, ib:ib + 1])
            incl = (ci <= ri) if fwd else (ci >= ri)
            incl_t = (ri <= ci) if fwd else (ri >= ci)
            it.bc = jnp.sum(jnp.where(incl, g_row, 0.0), axis=1, keepdims=True)
            bc_row = jnp.sum(jnp.where(incl_t, g_col, 0.0), axis=0, keepdims=True)
            it.lmat = jnp.where(incl, jnp.exp(jnp.minimum(it.bc - bc_row, 0.0)), 0.0)
            it.lvl = (lvlf_ref if fwd else lvlb_ref)[...]
            its.append(it)
        yield
        for it in its:
            it.amat = it.beta * _dot_nt(it.k, it.k) * it.lmat
            it.qk = _bf(_dot_nt(it.q, it.k) * it.lmat)
        yield
        for it in its:
            it.x = eye - jnp.where(it.lvl == 0, it.amat, 0.0)
        for lev in range(1, 6):
            for it in its:
                it.xb = _bf(it.x)
                it.y = _dot(_bf(jnp.where(it.lvl == lev, it.amat, 0.0)), it.xb)
            yield
            for it in its:
                it.x = it.x - _dot(it.xb, _bf(it.y))
            yield
        for it in its:
            e_col = jnp.exp(it.bc)
            kf = it.k.astype(F32)
            sol = _dot(_bf(it.x), _bf(jnp.concatenate([it.v * it.beta, kf * (it.beta * e_col)], axis=1)))
            it.u = _bf(sol[:, :HEAD_DIM])
            it.w = _bf(sol[:, HEAD_DIM:])
            bc_tot = it.bc[CHUNK - 1:CHUNK] if it.fwd else it.bc[0:1]
            it.qd = it.q.astype(F32) * e_col
            it.kd = _bf(kf * jnp.exp(bc_tot - it.bc))
            it.last = jnp.exp(bc_tot)
        yield
        for it in its:
            it.kw = _bf(_dot_tn(it.kd, it.w))
            it.ku = _dot_tn(it.kd, it.u)
            it.qs = _bf(it.qd - _dot(it.qk, it.w))
            it.o0 = _dot(it.qk, it.u)
        yield
        for it in its:
            s = st[it.fwd]
            sb = _bf(s)
            out = it.o0 + _dot(it.qs, sb)
            st[it.fwd] = it.last * s + (it.ku - _dot(it.kw, sb))
            if it.fwd:
                of_scr[it.rows, :] = out
            else:
                o_ref[0, it.rows, :] = out.astype(o_ref.dtype)

    def finish_fn(rows_list):
        _finish_heads(rows_list, o_ref, of_scr, zg_ref, nw_ref, centre=False)

    _scan_both(group_fn, finish_fn, n_ctx_chunks, n_chunks, GDN_GROUP, GDN_WAVES)


def _gdn(z, zgate, merged, conv_w, a_log_f, a_log_b, dt_f, dt_b, norm_w, n_ctx):
    bsz, t, _ = z.shape
    n = t // CHUNK
    g = zgate[:, :, :N_GATES].reshape(bsz, t, 4, B_HEADS)
    gate_cols = g.transpose(0, 3, 1, 2)
    gate_rows = g.reshape(bsz, n, CHUNK, 4, B_HEADS).transpose(0, 4, 3, 1, 2)
    par = jnp.stack([a_log_f, dt_f, a_log_b, dt_b] + [jnp.zeros_like(dt_f)] * 4, axis=1)
    par = jnp.broadcast_to(par[:, :, None], (B_HEADS, 8, LANES)).astype(F32)
    lvl_f, lvl_b, _, _ = _level_constants()
    kern = functools.partial(_gdn_kernel, n_ctx_chunks=n_ctx // CHUNK, n_chunks=n)

    def wspec(base):
        return pl.BlockSpec((3, LANES), lambda b, h: (0, base + h))

    return pl.pallas_call(
        kern,
        grid=(bsz, B_HEADS),
        in_specs=[_zspec(t, ZB_BQ), _zspec(t, ZB_BK), _zspec(t, ZB_BV), _zspec(t, ZB_BG),
                  wspec(0), wspec(B_HEADS), wspec(2 * B_HEADS),
                  pl.BlockSpec((1, 1, t, 4), lambda b, h: (b, h, 0, 0)),
                  pl.BlockSpec((1, 1, 4, n, CHUNK), lambda b, h: (b, h, 0, 0, 0)),
                  pl.BlockSpec((1, 8, LANES), lambda b, h: (h, 0, 0)),
                  _const_spec((1, LANES)), _const_spec(lvl_f.shape), _const_spec(lvl_b.shape),
                  pl.BlockSpec(memory_space=pl.ANY)],
        out_specs=pl.BlockSpec((1, t, LANES), lambda b, h: (b, 0, A_HEADS + h)),
        out_shape=jax.ShapeDtypeStruct(merged.shape, BF16),
        input_output_aliases={13: 0},
        scratch_shapes=[pltpu.VMEM((t, HEAD_DIM), BF16)] * 3 + [pltpu.VMEM((t, HEAD_DIM), F32)],
        compiler_params=_cparams(2),
        name="gdn",
    )(z, z, z, z, conv_w, conv_w, conv_w, gate_cols, gate_rows, par, norm_w.reshape(1, LANES),
      jnp.asarray(lvl_f), jnp.asarray(lvl_b), merged)


def _ret_kernel(zq_ref, zk_ref, zv_ref, zg_ref, base_ref, loc_ref, dec_ref, dmat_ref, cdec_ref, nw_ref,
                merged_ref, o_ref, of_scr, *, n_ctx_chunks, n_chunks):
    del merged_ref
    def group_fn(items, st):
        its = []
        for c, fwd in items:
            it = _Item()
            it.fwd, it.rows, it.d = fwd, _rows(c), 0 if fwd else 1
            d = it.d
            cb = base_ref[2 * d, pl.ds(c, 1), :]
            sb = base_ref[2 * d + 1, pl.ds(c, 1), :]
            cl, sl, cls, sls = (loc_ref[4 * d + n] for n in range(4))
            cos = cb * cl - sb * sl
            sin = sb * cls + cb * sls

            def rot(a):
                return a * cos + pltpu.roll(a, HEAD_DIM // 2, 1) * sin

            q = rot(zq_ref[0, it.rows, :].astype(F32) * SCALE)
            k = rot(zk_ref[0, it.rows, :].astype(F32))
            it.q, it.k = _bf(q), _bf(k)
            it.qd = _bf(q * dec_ref[0, 2 * d])
            it.kd = _bf(k * dec_ref[0, 2 * d + 1])
            it.v = zv_ref[0, it.rows, :]
            its.append(it)
        yield
        for it in its:
            it.scores = _bf(_dot_nt(it.q, it.k) * dmat_ref[0, it.d])
        yield
        for it in its:
            it.intra = _dot(it.scores, it.v)
            it.upd = _dot_tn(it.v, it.kd)
        yield
        for it in its:
            out = it.intra + _dot_nt(it.qd, _bf(st[it.fwd]))
            st[it.fwd] = st[it.fwd] * cdec_ref[0, it.d:it.d + 1, :] + it.upd
            if it.fwd:
                of_scr[it.rows, :] = out
            else:
                o_ref[0, it.rows, :] = out.astype(o_ref.dtype)

    def finish_fn(rows_list):
        _finish_heads(rows_list, o_ref, of_scr, zg_ref, nw_ref, centre=True)

    _scan_both(group_fn, finish_fn, n_ctx_chunks, n_chunks, RET_GROUP, RET_WAVES)


def _retention_tables(n_ctx, t):
    half = HEAD_DIM // 2
    n = t // CHUNK
    nc = n_ctx // CHUNK
    inv = ROPE_BASE ** (-jnp.arange(half, dtype=F32) / half)
    inv = jnp.concatenate([inv, inv])
    sgn = jnp.concatenate([-jnp.ones((half,), F32), jnp.ones((half,), F32)])
    c_idx = np.arange(n)
    base_f = (c_idx * CHUNK).astype(np.float32)
    base_b = np.where(c_idx < nc, n_ctx - CHUNK * (c_idx + 1), t - CHUNK * (c_idx - nc + 1)).astype(np.float32)
    loc_f = np.arange(CHUNK, dtype=np.float32)
    loc_b = loc_f[::-1].copy()

    def ang(p):
        return jnp.asarray(p)[:, None] * inv[None, :]

    base = jnp.stack([jnp.cos(ang(base_f)), jnp.sin(ang(base_f)), jnp.cos(ang(base_b)), jnp.sin(ang(base_b))])
    loc = jnp.stack([jnp.cos(ang(loc_f)), jnp.sin(ang(loc_f)), jnp.cos(ang(loc_f)) * sgn, jnp.sin(ang(loc_f)) * sgn,
                     jnp.cos(ang(loc_b)), jnp.sin(ang(loc_b)), jnp.cos(ang(loc_b)) * sgn, jnp.sin(ang(loc_b)) * sgn])
    lg_f = jnp.log1p(-jnp.exp2(-5.0 - jnp.arange(C_HEADS, dtype=F32)))
    lg_b = lg_f[::-1]
    idx = jnp.arange(CHUNK, dtype=F32)
    rel = idx[:, None] - idx[None, :]

    def masked_exp(a, mask):
        return jnp.where(mask, jnp.exp(jnp.where(mask, a, 0.0)), 0.0)

    dmat_f = masked_exp(lg_f[:, None, None] * rel, (rel >= 0)[None])
    dmat_b = masked_exp(-lg_b[:, None, None] * rel, (rel <= 0)[None])
    dq_f = jnp.exp(lg_f[:, None] * (idx + 1.0))
    dk_f = jnp.exp(lg_f[:, None] * (CHUNK - 1.0 - idx))
    dq_b = jnp.exp(lg_b[:, None] * (CHUNK - idx))
    dk_b = jnp.exp(lg_b[:, None] * idx)
    dec = jnp.stack([dq_f, dk_f, dq_b, dk_b], axis=1)
    dec = jnp.broadcast_to(dec[..., None], (C_HEADS, 4, CHUNK, LANES))
    cdec = jnp.stack([jnp.exp(lg_f * CHUNK), jnp.exp(lg_b * CHUNK)] + [jnp.zeros((C_HEADS,), F32)] * 6, axis=1)
    cdec = jnp.broadcast_to(cdec[..., None], (C_HEADS, 8, LANES))
    return base, loc, dec, jnp.stack([dmat_f, dmat_b], axis=1), cdec


def _retention(z, merged, norm_w, n_ctx):
    bsz, t, _ = z.shape
    n = t // CHUNK
    base, loc, dec, dmat, cdec = _retention_tables(n_ctx, t)
    kern = functools.partial(_ret_kernel, n_ctx_chunks=n_ctx // CHUNK, n_chunks=n)
    return pl.pallas_call(
        kern,
        grid=(bsz, C_HEADS),
        in_specs=[_zspec(t, ZB_CQ), _zspec(t, ZB_CK), _zspec(t, ZB_CV), _zspec(t, ZB_CG),
                  _const_spec(base.shape), _const_spec(loc.shape),
                  pl.BlockSpec((1, 4, CHUNK, LANES), lambda b, h: (h, 0, 0, 0)),
                  pl.BlockSpec((1, 2, CHUNK, CHUNK), lambda b, h: (h, 0, 0, 0)),
                  pl.BlockSpec((1, 8, LANES), lambda b, h: (h, 0, 0)),
                  pl.BlockSpec((1, LANES), lambda b, h: (0, h)),
                  pl.BlockSpec(memory_space=pl.ANY)],
        out_specs=pl.BlockSpec((1, t, LANES), lambda b, h: (b, 0, A_HEADS + B_HEADS + h)),
        out_shape=jax.ShapeDtypeStruct(merged.shape, BF16),
        input_output_aliases={10: 0},
        scratch_shapes=[pltpu.VMEM((t, HEAD_DIM), F32)],
        compiler_params=_cparams(2),
        name="retention",
    )(z, z, z, z, base, loc, dec, dmat, cdec, norm_w.reshape(1, C_W), merged)


def _reorder_in_proj(w):
    g0 = 5 * A_W + 4 * B_W
    main = jnp.concatenate([w[:, :g0], w[:, g0 + N_GATES:]], axis=1).astype(BF16)
    gates = jnp.pad(w[:, g0:g0 + N_GATES], ((0, 0), (0, LANES - N_GATES))).astype(BF16)
    return main, gates


def _lower_bounds(logits):
    p = jax.nn.softmax(logits.astype(F32), axis=0)
    return jnp.cumsum(p, axis=0) - p[0]


def kernel(x, c, ctx, c_ctx, w_mod, b_mod, norm1_w, norm2_w, w_in, qkv_conv_w, hgrn_lb_logits_fwd, hgrn_lb_logits_bwd, hgrn_norm_w, gdn_a_log_fwd, gdn_a_log_bwd, gdn_dt_bias_fwd, gdn_dt_bias_bwd, gdn_norm_w, ret_norm_w, w_out, ffn_w_up, ffn_conv_w, ffn_conv_b, ffn_w_down, final_norm_w):
    bsz, n_lat, d = x.shape
    n_ctx = ctx.shape[1]
    depth = w_mod.shape[0]
    t = n_ctx + n_lat
    assert n_ctx % ROW_TILE == 0 and n_lat % ROW_TILE == 0 and bsz < MOD_ROWS
    lb_fwd = _lower_bounds(hgrn_lb_logits_fwd)
    lb_bwd = _lower_bounds(hgrn_lb_logits_bwd)

    rows = jnp.concatenate([c, c_ctx[None, :], jnp.zeros((MOD_ROWS - bsz - 1, d), F32)], axis=0)
    mods = _modulation(rows, w_mod, b_mod)

    tm = 512
    m_rows = bsz * t
    assert m_rows % tm == 0
    f2 = ffn_w_up.shape[2]
    mod = [mods[l].reshape(MOD_ROWS * N_MOD, 1, d) for l in range(depth)]
    h = _norm_mod(ctx, x, norm1_w[0], mod[0], 1, 0)
    xs = x
    for l in range(depth):
        w_main, w_gate = _reorder_in_proj(w_in[l])
        h = h.reshape(m_rows, d)
        z = _matmul(h, w_main, tm, 23 * LANES, BF16, "in_proj").reshape(bsz, t, Z_BLOCKS * LANES)
        zgate = _matmul(h, w_gate, tm, LANES, F32, "in_proj_gates").reshape(bsz, t, LANES)
        merged = _hgrn(z, lb_fwd[l], lb_bwd[l], hgrn_norm_w[l], n_ctx)
        merged = _gdn(z, zgate, merged, qkv_conv_w[l], gdn_a_log_fwd[l], gdn_a_log_bwd[l],
                      gdn_dt_bias_fwd[l], gdn_dt_bias_bwd[l], gdn_norm_w[l], n_ctx)
        merged = _retention(z, merged, ret_norm_w[l], n_ctx)
        xs, h = _out_proj(merged, w_out[l].astype(BF16), ctx if l == 0 else None, xs, mod[l], 2,
                          norm2_w[l], 4, 3, n_ctx)

        uv = _matmul(h.reshape(m_rows, d), ffn_w_up[l].astype(BF16), tm, f2 // 4, BF16, "ffn_up").reshape(bsz, t, f2)
        w_down = ffn_w_down[l].astype(BF16)
        if l + 1 < depth:
            xs, h = _ffn_tail(uv, ffn_conv_w[l], ffn_conv_b[l], w_down, xs, mod[l], 5,
                              norm1_w[l + 1], mod[l + 1], 1, 0, n_ctx, 512)
    return _ffn_tail(uv, ffn_conv_w[-1], ffn_conv_b[-1], w_down, xs, mod[-1], 5,
                     final_norm_w, None, None, None, n_ctx, 512)
```

```python
import functools
import math

import numpy as np
import jax
import jax.numpy as jnp
from jax import lax
from jax.experimental import pallas as pl
from jax.experimental.pallas import tpu as pltpu

F32 = jnp.float32
BF16 = jnp.bfloat16

HEAD_DIM = 128
N_HEADS = 16
A_HEADS = 5
C_HEADS = 5
B_HEADS = 6
A_W = A_HEADS * HEAD_DIM
B_W = B_HEADS * HEAD_DIM
C_W = C_HEADS * HEAD_DIM
N_GATES = 4 * B_HEADS
CHUNK = 64
GRID_W = 64
ROPE_BASE = 10000.0
NORM_EPS = 1e-6
N_MOD = 6
GATE_CLAMP = 60.0
SCALE = HEAD_DIM ** -0.5
LOG2_E = 1.4426950408889634
LANES = 128
MOD_ROWS = 8
ROW_TILE = 256
VMEM_LIMIT = 56 * 1024 * 1024
HGRN_GROUP, HGRN_WAVES = 11, 1
GDN_GROUP, GDN_WAVES = 22, 22
RET_GROUP, RET_WAVES = 22, 22
PREP_GROUP = 4

ZB_AQ, ZB_AI, ZB_AFF, ZB_AFB, ZB_AG = 0, 5, 10, 15, 20
ZB_BQ, ZB_BK, ZB_BV, ZB_BG = 25, 31, 37, 43
ZB_CQ, ZB_CK, ZB_CV, ZB_CG = 49, 54, 59, 64
Z_BLOCKS = 69


def _cparams(n_axes):
    return pltpu.CompilerParams(dimension_semantics=("arbitrary",) * n_axes,
                                vmem_limit_bytes=VMEM_LIMIT)


def _dot(a, b):
    return jnp.dot(a, b, preferred_element_type=F32)


def _dot_nt(a, b):
    return lax.dot_general(a, b, (((1,), (1,)), ((), ())), preferred_element_type=F32)


def _dot_tn(a, b):
    return lax.dot_general(a, b, (((0,), (0,)), ((), ())), preferred_element_type=F32)


def _bf(a):
    return a.astype(BF16)


def _split2(a):
    hi = a.astype(BF16)
    lo = (a - hi.astype(F32)).astype(BF16)
    return hi, lo


def _dot3(a, b):
    ah, al = _split2(a)
    bh, bl = _split2(b)
    return _dot(ah, bh) + (_dot(al, bh) + _dot(ah, bl))


def _row_mean(a):
    hi, lo = _split2(a)
    ones = jnp.ones((a.shape[1], a.shape[1]), BF16)
    return (_dot(hi, ones) + _dot(lo, ones)) * (1.0 / a.shape[1])


def _silu(a):
    return a * jax.nn.sigmoid(a)


def _softplus(a):
    return jnp.maximum(a, 0.0) + jnp.log1p(jnp.exp(-jnp.abs(a)))


def _log_sigmoid(a):
    return jnp.minimum(a, 0.0) - jnp.log1p(jnp.exp(-jnp.abs(a)))


def _mod_kernel(a_ref, w_ref, b_ref, o_ref):
    a = _silu(a_ref[...])
    o_ref[0] = _dot3(a, w_ref[0]) + b_ref[0]


def _modulation(rows, w_mod, b_mod):
    depth, d, n = w_mod.shape
    tn = 1024
    return pl.pallas_call(
        _mod_kernel,
        grid=(depth, n // tn),
        in_specs=[pl.BlockSpec((MOD_ROWS, d), lambda l, j: (0, 0)),
                  pl.BlockSpec((1, d, tn), lambda l, j: (l, 0, j)),
                  pl.BlockSpec((1, 1, tn), lambda l, j: (l, 0, j))],
        out_specs=pl.BlockSpec((1, MOD_ROWS, tn), lambda l, j: (l, 0, j)),
        out_shape=jax.ShapeDtypeStruct((depth, MOD_ROWS, n), F32),
        compiler_params=_cparams(2),
        name="modulation",
    )(rows, w_mod, b_mod.reshape(depth, 1, n))


def _mod_row(b, i, n_ctx_tiles, n_batch, k):
    return (jnp.where(i < n_ctx_tiles, n_batch, b) * N_MOD + k, 0, 0)


def _split_stream_specs(n_ctx_tiles, d):
    return [pl.BlockSpec((1, ROW_TILE, d), lambda b, i: (b, jnp.minimum(i, n_ctx_tiles - 1), 0)),
            pl.BlockSpec((1, ROW_TILE, d), lambda b, i: (b, jnp.maximum(i - n_ctx_tiles, 0), 0))]


def _for_stream_tile(n_ctx_tiles, ctx_ref, lat_ref, body):
    i = pl.program_id(1)

    @pl.when(i < n_ctx_tiles)
    def _():
        body(ctx_ref[0])

    @pl.when(i >= n_ctx_tiles)
    def _():
        body(lat_ref[0])


def _norm_mod_kernel(ctx_ref, lat_ref, w_ref, sc_ref, sh_ref, o_ref, *, n_ctx_tiles):
    def body(x):
        o_ref[0] = _rms_mod(x, w_ref[...], sc_ref[0], sh_ref[0]).astype(o_ref.dtype)

    _for_stream_tile(n_ctx_tiles, ctx_ref, lat_ref, body)


def _norm_mod(ctx, x, w, mod, k_scale, k_shift):
    bsz, n_ctx, d = ctx.shape
    t = n_ctx + x.shape[1]
    nct = n_ctx // ROW_TILE
    return pl.pallas_call(
        functools.partial(_norm_mod_kernel, n_ctx_tiles=nct),
        grid=(bsz, t // ROW_TILE),
        in_specs=_split_stream_specs(nct, d) + [
            pl.BlockSpec((1, d), lambda b, i: (0, 0)),
            pl.BlockSpec((1, 1, d), lambda b, i: _mod_row(b, i, nct, bsz, k_scale)),
            pl.BlockSpec((1, 1, d), lambda b, i: _mod_row(b, i, nct, bsz, k_shift))],
        out_specs=pl.BlockSpec((1, ROW_TILE, d), lambda b, i: (b, i, 0)),
        out_shape=jax.ShapeDtypeStruct((bsz, t, d), BF16),
        compiler_params=_cparams(2),
        name="norm_mod",
    )(ctx, x, w.reshape(1, d), mod, mod)


def _mm_kernel(a_ref, w_ref, o_ref):
    o_ref[...] = _dot(a_ref[...], w_ref[...]).astype(o_ref.dtype)


def _matmul(a, w, tm, tn, out_dtype, name):
    m, k = a.shape
    n = w.shape[1]
    return pl.pallas_call(
        _mm_kernel,
        grid=(n // tn, m // tm),
        in_specs=[pl.BlockSpec((tm, k), lambda j, i: (i, 0)),
                  pl.BlockSpec((k, tn), lambda j, i: (0, j))],
        out_specs=pl.BlockSpec((tm, tn), lambda j, i: (i, j)),
        out_shape=jax.ShapeDtypeStruct((m, n), out_dtype),
        compiler_params=_cparams(2),
        name=name,
    )(a, w)


def _rms_mod(x, w, scale, shift):
    return (x * lax.rsqrt(jnp.mean(x * x, axis=-1, keepdims=True) + NORM_EPS) * w) * (1.0 + scale) + shift


def _out_proj_kernel(a_ref, w_ref, g_ref, nw_ref, sc_ref, sh_ref, *rest, n_ctx_tiles, split):
    def body(x):
        x = x + g_ref[0] * _dot(a_ref[0], w_ref[...])
        o_ref[0] = x
        h_ref[0] = _rms_mod(x, nw_ref[...], sc_ref[0], sh_ref[0]).astype(h_ref.dtype)

    if split:
        ctx_ref, lat_ref, o_ref, h_ref = rest
        _for_stream_tile(n_ctx_tiles, ctx_ref, lat_ref, body)
    else:
        x_ref, o_ref, h_ref = rest
        body(x_ref[0])


def _out_proj(a, w, ctx, x, mod, k_gate, norm_w, k_scale, k_shift, n_ctx):
    bsz, t, k = a.shape
    d = w.shape[1]
    nct = n_ctx // ROW_TILE
    split = ctx is not None

    def mod_spec(kk):
        return pl.BlockSpec((1, 1, d), lambda b, i: _mod_row(b, i, nct, bsz, kk))

    row_spec = pl.BlockSpec((1, ROW_TILE, d), lambda b, i: (b, i, 0))
    in_specs = [pl.BlockSpec((1, ROW_TILE, k), lambda b, i: (b, i, 0)),
                pl.BlockSpec((k, d), lambda b, i: (0, 0)),
                mod_spec(k_gate), pl.BlockSpec((1, d), lambda b, i: (0, 0)), mod_spec(k_scale), mod_spec(k_shift)]
    in_specs += _split_stream_specs(nct, d) if split else [row_spec]
    return pl.pallas_call(
        functools.partial(_out_proj_kernel, n_ctx_tiles=nct, split=split),
        grid=(bsz, t // ROW_TILE),
        in_specs=in_specs,
        out_specs=[row_spec, row_spec],
        out_shape=[jax.ShapeDtypeStruct((bsz, t, d), F32), jax.ShapeDtypeStruct((bsz, t, d), BF16)],
        input_output_aliases={} if split else {6: 0},
        compiler_params=_cparams(2),
        name="out_proj",
    )(a, w, mod, norm_w.reshape(1, d), mod, mod, *((ctx, x) if split else (x,)))


def _ffn_tail_kernel(up_ref, um_ref, un_ref, v_ref, cw_ref, cb_ref, shift_ref, wd_ref, x_ref, g_ref, nw_ref, *rest,
                     n_ctx_tiles, n_tiles, tile_offset, tc, last):
    if last:
        (o_ref,) = rest
    else:
        sc_ref, sh_ref, o_ref, h_ref = rest
    i = pl.program_id(1) + tile_offset
    is_ctx = i < n_ctx_tiles
    has_up = jnp.logical_and(jnp.logical_not(is_ctx), i > n_ctx_tiles)
    has_dn = jnp.logical_and(jnp.logical_not(is_ctx), i < n_tiles - 1)
    row_on = jnp.where(is_ctx, 0.0, 1.0)
    up_on = jnp.where(has_up, 1.0, 0.0)
    dn_on = jnp.where(has_dn, 1.0, 0.0)
    shift = shift_ref[jnp.where(is_ctx, 0, 1)]
    acc = None
    for j in range(um_ref.shape[2] // tc):
        cols = slice(j * tc, (j + 1) * tc)
        main = um_ref[0, :, cols].astype(F32)
        prev = up_ref[0, :, cols].astype(F32) * up_on
        nxt = un_ref[0, :, cols].astype(F32) * dn_on
        up = jnp.concatenate([prev, main[:ROW_TILE - GRID_W]], axis=0)
        dn = jnp.concatenate([main[GRID_W:], nxt], axis=0)
        w = cw_ref[:, cols]
        wu = w[0:3] * row_on
        wc = w[3:6]
        wd = w[6:9] * row_on

        def col(kw):
            return up * wu[kw:kw + 1] + main * wc[kw:kw + 1] + dn * wd[kw:kw + 1]

        sides = _dot(shift, jnp.concatenate([_bf(col(0)), _bf(col(2))], axis=0))
        u = sides + col(1) + cb_ref[:, cols]
        act = _bf(_silu(u)) * v_ref[0, :, cols]
        part = _dot(act, wd_ref[cols, :])
        acc = part if acc is None else acc + part
    x = x_ref[0] + g_ref[0] * acc
    if last:
        o_ref[0] = x * lax.rsqrt(jnp.mean(x * x, axis=-1, keepdims=True) + NORM_EPS) * nw_ref[...]
    else:
        o_ref[0] = x
        h_ref[0] = _rms_mod(x, nw_ref[...], sc_ref[0], sh_ref[0]).astype(h_ref.dtype)


def _shift_constants(n_ctx):
    r = np.arange(ROW_TILE)
    out = np.zeros((2, ROW_TILE, 2 * ROW_TILE), np.float32)
    for kind, width in enumerate((n_ctx, GRID_W)):
        has_left = r % width != 0
        has_right = r % width != width - 1
        out[kind, r[has_left], r[has_left] - 1] = 1.0
        out[kind, r[has_right], ROW_TILE + r[has_right] + 1] = 1.0
    return out


def _ffn_tail(uv, conv_w, conv_b, w_down, x, mod, k_gate, norm_w, mod_next, k_scale, k_shift, n_ctx, tc):
    bsz, t, f2 = uv.shape
    f = f2 // 2
    d = w_down.shape[1]
    assert n_ctx == ROW_TILE and ROW_TILE % GRID_W == 0 and f % tc == 0
    last = mod_next is None
    nct = n_ctx // ROW_TILE
    nt = t // ROW_TILE
    off = nct if last else 0
    per = ROW_TILE // GRID_W
    nrow = t // GRID_W
    kern = functools.partial(_ffn_tail_kernel, n_ctx_tiles=nct, n_tiles=nt, tile_offset=off, tc=tc, last=last)

    def mod_spec(kk):
        return pl.BlockSpec((1, 1, d), lambda b, i: _mod_row(b, i + off, nct, bsz, kk))

    row_spec = pl.BlockSpec((1, ROW_TILE, d), lambda b, i: (b, i, 0))
    in_specs = [pl.BlockSpec((1, GRID_W, f), lambda b, i: (b, jnp.maximum((i + off) * per - 1, 0), 0)),
                pl.BlockSpec((1, ROW_TILE, f), lambda b, i: (b, i + off, 0)),
                pl.BlockSpec((1, GRID_W, f), lambda b, i: (b, jnp.minimum((i + off) * per + per, nrow - 1), 0)),
                pl.BlockSpec((1, ROW_TILE, f), lambda b, i: (b, i + off, 1)),
                pl.BlockSpec((9, f), lambda b, i: (0, 0)),
                pl.BlockSpec((1, f), lambda b, i: (0, 0)),
                pl.BlockSpec((2, ROW_TILE, 2 * ROW_TILE), lambda b, i: (0, 0, 0)),
                pl.BlockSpec((f, d), lambda b, i: (0, 0), pipeline_mode=pl.Buffered(1)),
                pl.BlockSpec((1, ROW_TILE, d), lambda b, i: (b, i + off, 0)),
                mod_spec(k_gate),
                pl.BlockSpec((1, d), lambda b, i: (0, 0))]
    args = [uv, uv, uv, uv, conv_w.reshape(9, f), conv_b.reshape(1, f), jnp.asarray(_shift_constants(n_ctx), BF16),
            w_down, x, mod, norm_w.reshape(1, d)]
    if last:
        return pl.pallas_call(
            kern, grid=(bsz, nt - off), in_specs=in_specs, out_specs=row_spec,
            out_shape=jax.ShapeDtypeStruct((bsz, t - n_ctx, d), F32),
            compiler_params=_cparams(2), name="ffn_tail_final",
        )(*args)
    return pl.pallas_call(
        kern, grid=(bsz, nt), in_specs=in_specs + [mod_spec(k_scale), mod_spec(k_shift)],
        out_specs=[row_spec, row_spec],
        out_shape=[jax.ShapeDtypeStruct(x.shape, F32), jax.ShapeDtypeStruct(x.shape, BF16)],
        input_output_aliases={8: 0},
        compiler_params=_cparams(2), name="ffn_tail",
    )(*args, mod_next, mod_next)


def _scan_both(group_fn, finish_fn, n_ctx_chunks, n_chunks, group, waves):
    assert n_chunks % group == 0 and group % waves == 0
    per_wave = group // waves
    init = jnp.zeros((HEAD_DIM, HEAD_DIM), F32)

    def body(i, carry):
        st = {True: carry[0], False: carry[1]}
        gens = []
        for w in range(waves):
            items = []
            for j in range(per_wave):
                p = i * group + w * per_wave + j
                items.append((p, True))
                items.append((jnp.where(p < n_ctx_chunks, n_ctx_chunks - 1 - p, n_chunks + n_ctx_chunks - 1 - p), False))
            gens.append(group_fn(items, st))
        live = set(range(waves))
        step = 0
        while live:
            for w in sorted(live):
                if step >= w:
                    try:
                        next(gens[w])
                    except StopIteration:
                        live.discard(w)
            step += 1
        return st[True], st[False]

    lax.fori_loop(0, n_chunks // group, body, (init, init))

    def finish(i, carry):
        finish_fn([_rows(i * group + j) for j in range(group)])
        return carry

    lax.fori_loop(0, n_chunks // group, finish, 0)


def _rows(c):
    return pl.ds(pl.multiple_of(c * CHUNK, CHUNK), CHUNK)


def _finish_heads(rows_list, o_ref, of_scr, zg_ref, nw_ref, centre):
    outs = [of_scr[r, :] + o_ref[0, r, :].astype(F32) for r in rows_list]
    if centre:
        mus = [_row_mean(o) for o in outs]
        outs = [o - mu for o, mu in zip(outs, mus)]
    mss = [_row_mean(o * o) for o in outs]
    for r, o, ms in zip(rows_list, outs, mss):
        o = o * lax.rsqrt(ms + NORM_EPS) * nw_ref[...]
        o_ref[0, r, :] = (o * _silu(zg_ref[0, r, :].astype(F32))).astype(o_ref.dtype)


def _level_constants():
    idx = np.arange(CHUNK)
    x = idx[:, None] ^ idx[None, :]
    lvl = np.where(x > 0, np.floor(np.log2(np.maximum(x, 1))), 6).astype(np.int32)
    lvl_f = np.where(idx[:, None] >= idx[None, :], lvl, 7).astype(np.int32)
    lvl_b = lvl_f.T.copy()
    def stack(fwd):
        tri = ((idx[None, :] <= idx[:, None]) if fwd else (idx[None, :] >= idx[:, None])).astype(np.float32)
        mats = [tri]
        for m in (32, 16, 8, 4, 2, 1):
            ref = (idx // (2 * m)) * (2 * m) + (m - 1 if fwd else m)
            mats.append(tri - tri[ref])
        return np.concatenate(mats, axis=0)
    return lvl_f, lvl_b, stack(True), stack(False)


class _Item:
    pass


def _hgrn_kernel(zq_ref, zi_ref, zff_ref, zfb_ref, zg_ref, lbf_ref, lbb_ref, nw_ref,
                 self_ref, selb_ref, lvlf_ref, lvlb_ref, merged_ref, o_ref, of_scr, *, n_ctx_chunks, n_chunks):
    del merged_ref
    def group_fn(items, st):
        its = []
        for c, fwd in items:
            it = _Item()
            it.fwd, it.rows = fwd, _rows(c)
            zf_ref, lb_ref = (zff_ref, lbf_ref) if fwd else (zfb_ref, lbb_ref)
            it.q = _silu(zq_ref[0, it.rows, :].astype(F32)) * SCALE
            it.v = zi_ref[0, it.rows, :]
            zf = zf_ref[0, it.rows, :].astype(F32)
            lb = lb_ref[...]
            logf = _log_sigmoid(zf) + jnp.log1p(lb * jnp.exp(-jnp.maximum(zf, -GATE_CLAMP)))
            it.k = (1.0 - lb) * jax.nn.sigmoid(-zf)
            it.qb, it.kb = _bf(it.q), _bf(it.k)
            it.cat = jnp.concatenate(_split2(logf * LOG2_E), axis=1)
            its.append(it)
        yield
        for it in its:
            cs = _dot((self_ref if it.fwd else selb_ref)[...], it.cat)
            it.cs = cs[:, :LANES] + cs[:, LANES:]
            it.b = it.cs[:CHUNK]
            it.lvl = (lvlf_ref if it.fwd else lvlb_ref)[...]
        yield
        for it in its:
            it.scores = jnp.where(it.lvl == 6, _dot_nt(it.qb, it.kb), 0.0)
        for n in range(6):
            yield
            for it in its:
                wgt = _bf(jnp.exp2(-jnp.abs(it.cs[(n + 1) * CHUNK:(n + 2) * CHUNK])))
                it.scores = jnp.where(it.lvl == 5 - n, _dot_nt(it.qb * wgt, it.kb * wgt), it.scores)
        yield
        for it in its:
            b_tot = it.b[CHUNK - 1:CHUNK] if it.fwd else it.b[0:1]
            it.intra = _dot(_bf(it.scores), it.v)
            it.upd = _dot_tn(it.v, _bf(it.k * jnp.exp2(b_tot - it.b)))
            it.qd = _bf(it.q * jnp.exp2(it.b))
            it.dec = jnp.exp2(b_tot)
        yield
        for it in its:
            out = it.intra + _dot_nt(it.qd, _bf(st[it.fwd]))
            st[it.fwd] = st[it.fwd] * it.dec + it.upd
            if it.fwd:
                of_scr[it.rows, :] = out
            else:
                o_ref[0, it.rows, :] = out.astype(o_ref.dtype)

    def finish_fn(rows_list):
        _finish_heads(rows_list, o_ref, of_scr, zg_ref, nw_ref, centre=False)

    _scan_both(group_fn, finish_fn, n_ctx_chunks, n_chunks, HGRN_GROUP, HGRN_WAVES)


def _zspec(t, base):
    return pl.BlockSpec((1, t, LANES), lambda b, h: (b, 0, base + h))


def _const_spec(shape):
    return pl.BlockSpec(shape, lambda b, h: (0,) * len(shape))


def _hgrn(z, merged, lb_f, lb_b, norm_w, n_ctx):
    bsz, t, _ = z.shape
    lvl_f, lvl_b, sel_f, sel_b = _level_constants()
    kern = functools.partial(_hgrn_kernel, n_ctx_chunks=n_ctx // CHUNK, n_chunks=t // CHUNK)
    head_vec = pl.BlockSpec((1, LANES), lambda b, h: (0, h))
    return pl.pallas_call(
        kern,
        grid=(bsz, A_HEADS),
        in_specs=[_zspec(t, ZB_AQ), _zspec(t, ZB_AI), _zspec(t, ZB_AFF), _zspec(t, ZB_AFB), _zspec(t, ZB_AG),
                  head_vec, head_vec, _const_spec((1, LANES)),
                  _const_spec(sel_f.shape), _const_spec(sel_b.shape),
                  _const_spec(lvl_f.shape), _const_spec(lvl_b.shape),
                  pl.BlockSpec(memory_space=pl.ANY)],
        out_specs=pl.BlockSpec((1, t, LANES), lambda b, h: (b, 0, h)),
        out_shape=jax.ShapeDtypeStruct(merged.shape, BF16),
        input_output_aliases={12: 0},
        scratch_shapes=[pltpu.VMEM((t, HEAD_DIM), F32)],
        compiler_params=_cparams(2),
        name="hgrn",
    )(z, z, z, z, z, lb_f.reshape(1, A_W), lb_b.reshape(1, A_W), norm_w.reshape(1, LANES),
      jnp.asarray(sel_f, BF16), jnp.asarray(sel_b, BF16), jnp.asarray(lvl_f), jnp.asarray(lvl_b), merged)


def _gdn_kernel(zq_ref, zk_ref, zv_ref, zg_ref, wq_ref, wk_ref, wv_ref, gc_ref, gr_ref, par_ref, nw_ref,
                lvlf_ref, lvlb_ref, merged_ref, o_ref, q_scr, k_scr, v_scr, of_scr, *, n_ctx_chunks, n_chunks):
    del merged_ref
    t_total = n_chunks * CHUNK
    ri = lax.broadcasted_iota(jnp.int32, (CHUNK, CHUNK), 0)
    ci = lax.broadcasted_iota(jnp.int32, (CHUNK, CHUNK), 1)
    row_id = lax.broadcasted_iota(jnp.int32, (CHUNK, 1), 0)
    eye = jnp.where(ri == ci, 1.0, 0.0)

    def conv(z_ref, w_ref, c):
        has_prev = jnp.logical_and(c != 0, c != n_ctx_chunks)
        has_next = jnp.logical_and(c != n_ctx_chunks - 1, c != n_chunks - 1)
        p0 = pl.multiple_of(jnp.maximum(c * CHUNK - 16, 0), 16)
        n0 = pl.multiple_of(jnp.minimum(c * CHUNK + CHUNK, t_total - 16), 16)
        x = z_ref[0, _rows(c), :].astype(F32)
        xp = z_ref[0, pl.ds(p0, 16), :].astype(F32)[15:16] * jnp.where(has_prev, 1.0, 0.0)
        xn = z_ref[0, pl.ds(n0, 16), :].astype(F32)[0:1] * jnp.where(has_next, 1.0, 0.0)
        xm1 = jnp.where(row_id == 0, xp, pltpu.roll(x, 1, 0))
        xp1 = jnp.where(row_id == CHUNK - 1, xn, pltpu.roll(x, CHUNK - 1, 0))
        w = w_ref[...]
        return _silu(xm1 * w[0:1] + x * w[1:2] + xp1 * w[2:3])

    def prep(i, carry):
        cs = [i * PREP_GROUP + j for j in range(PREP_GROUP)]
        qs = [conv(zq_ref, wq_ref, c) for c in cs]
        ks = [conv(zk_ref, wk_ref, c) for c in cs]
        q_ss = [_row_mean(a * a) * HEAD_DIM for a in qs]
        k_ss = [_row_mean(a * a) * HEAD_DIM for a in ks]
        for c, q, k, qn, kn in zip(cs, qs, ks, q_ss, k_ss):
            q_scr[_rows(c), :] = (q * (lax.rsqrt(qn + NORM_EPS) * SCALE)).astype(q_scr.dtype)
            k_scr[_rows(c), :] = (k * lax.rsqrt(kn + NORM_EPS)).astype(k_scr.dtype)
            v_scr[_rows(c), :] = conv(zv_ref, wv_ref, c).astype(v_scr.dtype)
        return carry

    lax.fori_loop(0, n_chunks // PREP_GROUP, prep, 0)

    def group_fn(items, st):
        par = par_ref[0]
        its = []
        for c, fwd in items:
            it = _Item()
            it.fwd, it.rows = fwd, _rows(c)
            ia, ib, pr = (0, 2, 0) if fwd else (1, 3, 2)
            neg_a = -jnp.exp(par[pr:pr + 1, 0:1])
            dt = par[pr + 1:pr + 2, 0:1]
            it.q = q_scr[it.rows, :]
            it.k = k_scr[it.rows, :]
            it.v = v_scr[it.rows, :].astype(F32)
            g_col = neg_a * _softplus(gc_ref[0, 0, it.rows, ia:ia + 1] + dt)
            g_row = neg_a * _softplus(gr_ref[0, 0, ia, pl.ds(c, 1), :] + dt)
            it.beta = jax.nn.sigmoid(gc_ref[0, 0, it.rows, ib:ib + 1])
            incl = (ci <= ri) if fwd else (ci >= ri)
            incl_t = (ri <= ci) if fwd else (ri >= ci)
            it.bc = jnp.sum(jnp.where(incl, g_row, 0.0), axis=1, keepdims=True)
            bc_row = jnp.sum(jnp.where(incl_t, g_col, 0.0), axis=0, keepdims=True)
            it.lmat = jnp.where(incl, jnp.exp(jnp.minimum(it.bc - bc_row, 0.0)), 0.0)
            it.lvl = (lvlf_ref if fwd else lvlb_ref)[...]
            its.append(it)
        yield
        for it in its:
            it.amat = it.beta * _dot_nt(it.k, it.k) * it.lmat
            it.qk = _bf(_dot_nt(it.q, it.k) * it.lmat)
        yield
        for it in its:
            it.x = eye - jnp.where(it.lvl == 0, it.amat, 0.0)
        for lev in range(1, 6):
            for it in its:
                it.xb = _bf(it.x)
                it.y = _dot(_bf(jnp.where(it.lvl == lev, it.amat, 0.0)), it.xb)
            yield
            for it in its:
                it.x = it.x - _dot(it.xb, _bf(it.y))
            yield
        for it in its:
            e_col = jnp.exp(it.bc)
            kf = it.k.astype(F32)
            sol = _dot(_bf(it.x), _bf(jnp.concatenate([it.v * it.beta, kf * (it.beta * e_col)], axis=1)))
            it.u = _bf(sol[:, :HEAD_DIM])
            it.w = _bf(sol[:, HEAD_DIM:])
            bc_tot = it.bc[CHUNK - 1:CHUNK] if it.fwd else it.bc[0:1]
            it.qd = it.q.astype(F32) * e_col
            it.kd = _bf(kf * jnp.exp(bc_tot - it.bc))
            it.last = jnp.exp(bc_tot)
        yield
        for it in its:
            it.kw = _bf(_dot_tn(it.kd, it.w))
            it.ku = _dot_tn(it.kd, it.u)
            it.qs = _bf(it.qd - _dot(it.qk, it.w))
            it.o0 = _dot(it.qk, it.u)
        yield
        for it in its:
            s = st[it.fwd]
            sb = _bf(s)
            out = it.o0 + _dot(it.qs, sb)
            st[it.fwd] = it.last * s + (it.ku - _dot(it.kw, sb))
            if it.fwd:
                of_scr[it.rows, :] = out
            else:
                o_ref[0, it.rows, :] = out.astype(o_ref.dtype)

    def finish_fn(rows_list):
        _finish_heads(rows_list, o_ref, of_scr, zg_ref, nw_ref, centre=False)

    _scan_both(group_fn, finish_fn, n_ctx_chunks, n_chunks, GDN_GROUP, GDN_WAVES)


def _gdn(z, zgate, merged, conv_w, a_log_f, a_log_b, dt_f, dt_b, norm_w, n_ctx):
    bsz, t, _ = z.shape
    n = t // CHUNK
    g = zgate[:, :, :N_GATES].reshape(bsz, t, 4, B_HEADS)
    gate_cols = g.transpose(0, 3, 1, 2)
    gate_rows = g.reshape(bsz, n, CHUNK, 4, B_HEADS).transpose(0, 4, 3, 1, 2)
    par = jnp.stack([a_log_f, dt_f, a_log_b, dt_b] + [jnp.zeros_like(dt_f)] * 4, axis=1)
    par = jnp.broadcast_to(par[:, :, None], (B_HEADS, 8, LANES)).astype(F32)
    lvl_f, lvl_b, _, _ = _level_constants()
    kern = functools.partial(_gdn_kernel, n_ctx_chunks=n_ctx // CHUNK, n_chunks=n)

    def wspec(base):
        return pl.BlockSpec((3, LANES), lambda b, h: (0, base + h))

    return pl.pallas_call(
        kern,
        grid=(bsz, B_HEADS),
        in_specs=[_zspec(t, ZB_BQ), _zspec(t, ZB_BK), _zspec(t, ZB_BV), _zspec(t, ZB_BG),
                  wspec(0), wspec(B_HEADS), wspec(2 * B_HEADS),
                  pl.BlockSpec((1, 1, t, 4), lambda b, h: (b, h, 0, 0)),
                  pl.BlockSpec((1, 1, 4, n, CHUNK), lambda b, h: (b, h, 0, 0, 0)),
                  pl.BlockSpec((1, 8, LANES), lambda b, h: (h, 0, 0)),
                  _const_spec((1, LANES)), _const_spec(lvl_f.shape), _const_spec(lvl_b.shape),
                  pl.BlockSpec(memory_space=pl.ANY)],
        out_specs=pl.BlockSpec((1, t, LANES), lambda b, h: (b, 0, A_HEADS + h)),
        out_shape=jax.ShapeDtypeStruct(merged.shape, BF16),
        input_output_aliases={13: 0},
        scratch_shapes=[pltpu.VMEM((t, HEAD_DIM), BF16)] * 3 + [pltpu.VMEM((t, HEAD_DIM), F32)],
        compiler_params=_cparams(2),
        name="gdn",
    )(z, z, z, z, conv_w, conv_w, conv_w, gate_cols, gate_rows, par, norm_w.reshape(1, LANES),
      jnp.asarray(lvl_f), jnp.asarray(lvl_b), merged)


def _ret_kernel(zq_ref, zk_ref, zv_ref, zg_ref, base_ref, loc_ref, dec_ref, dmat_ref, cdec_ref, nw_ref,
                merged_ref, o_ref, of_scr, *, n_ctx_chunks, n_chunks):
    del merged_ref
    def group_fn(items, st):
        its = []
        for c, fwd in items:
            it = _Item()
            it.fwd, it.rows, it.d = fwd, _rows(c), 0 if fwd else 1
            d = it.d
            cb = base_ref[2 * d, pl.ds(c, 1), :]
            sb = base_ref[2 * d + 1, pl.ds(c, 1), :]
            cl, sl, cls, sls = (loc_ref[4 * d + n] for n in range(4))
            cos = cb * cl - sb * sl
            sin = sb * cls + cb * sls

            def rot(a):
                return a * cos + pltpu.roll(a, HEAD_DIM // 2, 1) * sin

            q = rot(zq_ref[0, it.rows, :].astype(F32) * SCALE)
            k = rot(zk_ref[0, it.rows, :].astype(F32))
            it.q, it.k = _bf(q), _bf(k)
            it.qd = _bf(q * dec_ref[0, 2 * d])
            it.kd = _bf(k * dec_ref[0, 2 * d + 1])
            it.v = zv_ref[0, it.rows, :]
            its.append(it)
        yield
        for it in its:
            it.scores = _bf(_dot_nt(it.q, it.k) * dmat_ref[0, it.d])
        yield
        for it in its:
            it.intra = _dot(it.scores, it.v)
            it.upd = _dot_tn(it.v, it.kd)
        yield
        for it in its:
            out = it.intra + _dot_nt(it.qd, _bf(st[it.fwd]))
            st[it.fwd] = st[it.fwd] * cdec_ref[0, it.d:it.d + 1, :] + it.upd
            if it.fwd:
                of_scr[it.rows, :] = out
            else:
                o_ref[0, it.rows, :] = out.astype(o_ref.dtype)

    def finish_fn(rows_list):
        _finish_heads(rows_list, o_ref, of_scr, zg_ref, nw_ref, centre=True)

    _scan_both(group_fn, finish_fn, n_ctx_chunks, n_chunks, RET_GROUP, RET_WAVES)


def _retention_tables(n_ctx, t):
    half = HEAD_DIM // 2
    n = t // CHUNK
    nc = n_ctx // CHUNK
    inv = ROPE_BASE ** (-jnp.arange(half, dtype=F32) / half)
    inv = jnp.concatenate([inv, inv])
    sgn = jnp.concatenate([-jnp.ones((half,), F32), jnp.ones((half,), F32)])
    c_idx = np.arange(n)
    base_f = (c_idx * CHUNK).astype(np.float32)
    base_b = np.where(c_idx < nc, n_ctx - CHUNK * (c_idx + 1), t - CHUNK * (c_idx - nc + 1)).astype(np.float32)
    loc_f = np.arange(CHUNK, dtype=np.float32)
    loc_b = loc_f[::-1].copy()

    def ang(p):
        return jnp.asarray(p)[:, None] * inv[None, :]

    base = jnp.stack([jnp.cos(ang(base_f)), jnp.sin(ang(base_f)), jnp.cos(ang(base_b)), jnp.sin(ang(base_b))])
    loc = jnp.stack([jnp.cos(ang(loc_f)), jnp.sin(ang(loc_f)), jnp.cos(ang(loc_f)) * sgn, jnp.sin(ang(loc_f)) * sgn,
                     jnp.cos(ang(loc_b)), jnp.sin(ang(loc_b)), jnp.cos(ang(loc_b)) * sgn, jnp.sin(ang(loc_b)) * sgn])
    lg_f = jnp.log1p(-jnp.exp2(-5.0 - jnp.arange(C_HEADS, dtype=F32)))
    lg_b = lg_f[::-1]
    idx = jnp.arange(CHUNK, dtype=F32)
    rel = idx[:, None] - idx[None, :]

    def masked_exp(a, mask):
        return jnp.where(mask, jnp.exp(jnp.where(mask, a, 0.0)), 0.0)

    dmat_f = masked_exp(lg_f[:, None, None] * rel, (rel >= 0)[None])
    dmat_b = masked_exp(-lg_b[:, None, None] * rel, (rel <= 0)[None])
    dq_f = jnp.exp(lg_f[:, None] * (idx + 1.0))
    dk_f = jnp.exp(lg_f[:, None] * (CHUNK - 1.0 - idx))
    dq_b = jnp.exp(lg_b[:, None] * (CHUNK - idx))
    dk_b = jnp.exp(lg_b[:, None] * idx)
    dec = jnp.stack([dq_f, dk_f, dq_b, dk_b], axis=1)
    dec = jnp.broadcast_to(dec[..., None], (C_HEADS, 4, CHUNK, LANES))
    cdec = jnp.stack([jnp.exp(lg_f * CHUNK), jnp.exp(lg_b * CHUNK)] + [jnp.zeros((C_HEADS,), F32)] * 6, axis=1)
    cdec = jnp.broadcast_to(cdec[..., None], (C_HEADS, 8, LANES))
    return base, loc, dec, jnp.stack([dmat_f, dmat_b], axis=1), cdec


def _retention(z, merged, norm_w, n_ctx):
    bsz, t, _ = z.shape
    n = t // CHUNK
    base, loc, dec, dmat, cdec = _retention_tables(n_ctx, t)
    kern = functools.partial(_ret_kernel, n_ctx_chunks=n_ctx // CHUNK, n_chunks=n)
    return pl.pallas_call(
        kern,
        grid=(bsz, C_HEADS),
        in_specs=[_zspec(t, ZB_CQ), _zspec(t, ZB_CK), _zspec(t, ZB_CV), _zspec(t, ZB_CG),
                  _const_spec(base.shape), _const_spec(loc.shape),
                  pl.BlockSpec((1, 4, CHUNK, LANES), lambda b, h: (h, 0, 0, 0)),
                  pl.BlockSpec((1, 2, CHUNK, CHUNK), lambda b, h: (h, 0, 0, 0)),
                  pl.BlockSpec((1, 8, LANES), lambda b, h: (h, 0, 0)),
                  pl.BlockSpec((1, LANES), lambda b, h: (0, h)),
                  pl.BlockSpec(memory_space=pl.ANY)],
        out_specs=pl.BlockSpec((1, t, LANES), lambda b, h: (b, 0, A_HEADS + B_HEADS + h)),
        out_shape=jax.ShapeDtypeStruct(merged.shape, BF16),
        input_output_aliases={10: 0},
        scratch_shapes=[pltpu.VMEM((t, HEAD_DIM), F32)],
        compiler_params=_cparams(2),
        name="retention",
    )(z, z, z, z, base, loc, dec, dmat, cdec, norm_w.reshape(1, C_W), merged)


def _reorder_in_proj(w):
    g0 = 5 * A_W + 4 * B_W
    main = jnp.concatenate([w[:, :g0], w[:, g0 + N_GATES:]], axis=1).astype(BF16)
    gates = jnp.pad(w[:, g0:g0 + N_GATES], ((0, 0), (0, LANES - N_GATES))).astype(BF16)
    return main, gates


def _lower_bounds(logits):
    p = jax.nn.softmax(logits.astype(F32), axis=0)
    return jnp.cumsum(p, axis=0) - p[0]


def kernel(x, c, ctx, c_ctx, w_mod, b_mod, norm1_w, norm2_w, w_in, qkv_conv_w, hgrn_lb_logits_fwd, hgrn_lb_logits_bwd, hgrn_norm_w, gdn_a_log_fwd, gdn_a_log_bwd, gdn_dt_bias_fwd, gdn_dt_bias_bwd, gdn_norm_w, ret_norm_w, w_out, ffn_w_up, ffn_conv_w, ffn_conv_b, ffn_w_down, final_norm_w):
    bsz, n_lat, d = x.shape
    n_ctx = ctx.shape[1]
    depth = w_mod.shape[0]
    t = n_ctx + n_lat
    assert n_ctx % ROW_TILE == 0 and n_lat % ROW_TILE == 0 and bsz < MOD_ROWS
    lb_fwd = _lower_bounds(hgrn_lb_logits_fwd)
    lb_bwd = _lower_bounds(hgrn_lb_logits_bwd)

    rows = jnp.concatenate([c, c_ctx[None, :], jnp.zeros((MOD_ROWS - bsz - 1, d), F32)], axis=0)
    mods = _modulation(rows, w_mod, b_mod)

    tm = 512
    m_rows = bsz * t
    assert m_rows % tm == 0
    f2 = ffn_w_up.shape[2]
    mod = [mods[l].reshape(MOD_ROWS * N_MOD, 1, d) for l in range(depth)]
    h = _norm_mod(ctx, x, norm1_w[0], mod[0], 1, 0)
    xs = x
    for l in range(depth):
        w_main, w_gate = _reorder_in_proj(w_in[l])
        h = h.reshape(m_rows, d)
        z = _matmul(h, w_main, tm, 23 * LANES, BF16, "in_proj").reshape(bsz, t, Z_BLOCKS * LANES)
        zgate = _matmul(h, w_gate, tm, LANES, F32, "in_proj_gates").reshape(bsz, t, LANES)
        merged = jnp.zeros((bsz, t, N_HEADS * HEAD_DIM), BF16)
        merged = _hgrn(z, merged, lb_fwd[l], lb_bwd[l], hgrn_norm_w[l], n_ctx)
        merged = _gdn(z, zgate, merged, qkv_conv_w[l], gdn_a_log_fwd[l], gdn_a_log_bwd[l],
                      gdn_dt_bias_fwd[l], gdn_dt_bias_bwd[l], gdn_norm_w[l], n_ctx)
        merged = _retention(z, merged, ret_norm_w[l], n_ctx)
        xs, h = _out_proj(merged, w_out[l].astype(BF16), ctx if l == 0 else None, xs, mod[l], 2,
                          norm2_w[l], 4, 3, n_ctx)

        uv = _matmul(h.reshape(m_rows, d), ffn_w_up[l].astype(BF16), tm, f2 // 4, BF16, "ffn_up").reshape(bsz, t, f2)
        w_down = ffn_w_down[l].astype(BF16)
        if l + 1 < depth:
            xs, h = _ffn_tail(uv, ffn_conv_w[l], ffn_conv_b[l], w_down, xs, mod[l], 5,
                              norm1_w[l + 1], mod[l + 1], 1, 0, n_ctx, 512)
    return _ffn_tail(uv, ffn_conv_w[-1], ffn_conv_b[-1], w_down, xs, mod[-1], 5,
                     final_norm_w, None, None, None, n_ctx, 512)
```

```python
import functools
import math

import numpy as np
import jax
import jax.numpy as jnp
from jax import lax
from jax.experimental import pallas as pl
from jax.experimental.pallas import tpu as pltpu

F32 = jnp.float32
BF16 = jnp.bfloat16

HEAD_DIM = 128
N_HEADS = 16
A_HEADS = 5
C_HEADS = 5
B_HEADS = 6
A_W = A_HEADS * HEAD_DIM
B_W = B_HEADS * HEAD_DIM
C_W = C_HEADS * HEAD_DIM
N_GATES = 4 * B_HEADS
CHUNK = 64
GRID_W = 64
ROPE_BASE = 10000.0
NORM_EPS = 1e-6
N_MOD = 6
GATE_CLAMP = 60.0
SCALE = HEAD_DIM ** -0.5
LOG2_E = 1.4426950408889634
LANES = 128
MOD_ROWS = 8
ROW_TILE = 256
VMEM_LIMIT = 56 * 1024 * 1024
HGRN_GROUP, HGRN_WAVES = 11, 1
GDN_GROUP, GDN_WAVES = 22, 22
RET_GROUP, RET_WAVES = 22, 22
PREP_GROUP = 4
CONV_AHEAD = 1
FFN_SLAB = 512
MM_ROWS = 512
IN_PROJ_COL_TILES = 3
FFN_UP_COL_TILES = 4

ZB_AQ, ZB_AI, ZB_AFF, ZB_AFB, ZB_AG = 0, 5, 10, 15, 20
ZB_BQ, ZB_BK, ZB_BV, ZB_BG = 25, 31, 37, 43
ZB_CQ, ZB_CK, ZB_CV, ZB_CG = 49, 54, 59, 64
Z_BLOCKS = 69


def _cparams(n_axes):
    return pltpu.CompilerParams(dimension_semantics=("arbitrary",) * n_axes,
                                vmem_limit_bytes=VMEM_LIMIT)


def _dot(a, b):
    return jnp.dot(a, b, preferred_element_type=F32)


def _dot_nt(a, b):
    return lax.dot_general(a, b, (((1,), (1,)), ((), ())), preferred_element_type=F32)


def _dot_tn(a, b):
    return lax.dot_general(a, b, (((0,), (0,)), ((), ())), preferred_element_type=F32)


def _bf(a):
    return a.astype(BF16)


def _split2(a):
    hi = a.astype(BF16)
    lo = (a - hi.astype(F32)).astype(BF16)
    return hi, lo


def _dot3(a, b):
    ah, al = _split2(a)
    bh, bl = _split2(b)
    return _dot(ah, bh) + (_dot(al, bh) + _dot(ah, bl))


def _row_mean(a):
    hi, lo = _split2(a)
    ones = jnp.ones((a.shape[1], a.shape[1]), BF16)
    return (_dot(hi, ones) + _dot(lo, ones)) * (1.0 / a.shape[1])


def _silu(a):
    return a * jax.nn.sigmoid(a)


def _softplus(a):
    return jnp.maximum(a, 0.0) + jnp.log1p(jnp.exp(-jnp.abs(a)))


def _log_sigmoid(a):
    return jnp.minimum(a, 0.0) - jnp.log1p(jnp.exp(-jnp.abs(a)))


def _mod_kernel(a_ref, w_ref, b_ref, o_ref):
    a = _silu(a_ref[...])
    o_ref[0] = _dot3(a, w_ref[0]) + b_ref[0]


def _modulation(rows, w_mod, b_mod):
    depth, d, n = w_mod.shape
    tn = 1024
    return pl.pallas_call(
        _mod_kernel,
        grid=(depth, n // tn),
        in_specs=[pl.BlockSpec((MOD_ROWS, d), lambda l, j: (0, 0)),
                  pl.BlockSpec((1, d, tn), lambda l, j: (l, 0, j)),
                  pl.BlockSpec((1, 1, tn), lambda l, j: (l, 0, j))],
        out_specs=pl.BlockSpec((1, MOD_ROWS, tn), lambda l, j: (l, 0, j)),
        out_shape=jax.ShapeDtypeStruct((depth, MOD_ROWS, n), F32),
        compiler_params=_cparams(2),
        name="modulation",
    )(rows, w_mod, b_mod.reshape(depth, 1, n))


def _mod_row(b, i, n_ctx_tiles, n_batch, k):
    return (jnp.where(i < n_ctx_tiles, n_batch, b) * N_MOD + k, 0, 0)


def _split_stream_specs(n_ctx_tiles, d):
    return [pl.BlockSpec((1, ROW_TILE, d), lambda b, i: (b, jnp.minimum(i, n_ctx_tiles - 1), 0)),
            pl.BlockSpec((1, ROW_TILE, d), lambda b, i: (b, jnp.maximum(i - n_ctx_tiles, 0), 0))]


def _for_stream_tile(n_ctx_tiles, ctx_ref, lat_ref, body):
    i = pl.program_id(1)

    @pl.when(i < n_ctx_tiles)
    def _():
        body(ctx_ref[0])

    @pl.when(i >= n_ctx_tiles)
    def _():
        body(lat_ref[0])


def _norm_mod_kernel(ctx_ref, lat_ref, w_ref, sc_ref, sh_ref, o_ref, *, n_ctx_tiles):
    def body(x):
        o_ref[0] = _rms_mod(x, w_ref[...], sc_ref[0], sh_ref[0]).astype(o_ref.dtype)

    _for_stream_tile(n_ctx_tiles, ctx_ref, lat_ref, body)


def _norm_mod(ctx, x, w, mod, k_scale, k_shift):
    bsz, n_ctx, d = ctx.shape
    t = n_ctx + x.shape[1]
    nct = n_ctx // ROW_TILE
    return pl.pallas_call(
        functools.partial(_norm_mod_kernel, n_ctx_tiles=nct),
        grid=(bsz, t // ROW_TILE),
        in_specs=_split_stream_specs(nct, d) + [
            pl.BlockSpec((1, d), lambda b, i: (0, 0)),
            pl.BlockSpec((1, 1, d), lambda b, i: _mod_row(b, i, nct, bsz, k_scale)),
            pl.BlockSpec((1, 1, d), lambda b, i: _mod_row(b, i, nct, bsz, k_shift))],
        out_specs=pl.BlockSpec((1, ROW_TILE, d), lambda b, i: (b, i, 0)),
        out_shape=jax.ShapeDtypeStruct((bsz, t, d), BF16),
        compiler_params=_cparams(2),
        name="norm_mod",
    )(ctx, x, w.reshape(1, d), mod, mod)


def _mm_kernel(a_ref, w_ref, o_ref):
    o_ref[...] = _dot(a_ref[...], w_ref[...]).astype(o_ref.dtype)


def _matmul(a, w, tm, tn, out_dtype, name):
    m, k = a.shape
    n = w.shape[1]
    return pl.pallas_call(
        _mm_kernel,
        grid=(n // tn, m // tm),
        in_specs=[pl.BlockSpec((tm, k), lambda j, i: (i, 0)),
                  pl.BlockSpec((k, tn), lambda j, i: (0, j))],
        out_specs=pl.BlockSpec((tm, tn), lambda j, i: (i, j)),
        out_shape=jax.ShapeDtypeStruct((m, n), out_dtype),
        compiler_params=_cparams(2),
        name=name,
    )(a, w)


def _rms_mod(x, w, scale, shift):
    return (x * lax.rsqrt(jnp.mean(x * x, axis=-1, keepdims=True) + NORM_EPS) * w) * (1.0 + scale) + shift


def _out_proj_kernel(a_ref, w_ref, g_ref, nw_ref, sc_ref, sh_ref, *rest, n_ctx_tiles, split):
    def body(x):
        x = x + g_ref[0] * _dot(a_ref[0], w_ref[...])
        o_ref[0] = x
        h_ref[0] = _rms_mod(x, nw_ref[...], sc_ref[0], sh_ref[0]).astype(h_ref.dtype)

    if split:
        ctx_ref, lat_ref, o_ref, h_ref = rest
        _for_stream_tile(n_ctx_tiles, ctx_ref, lat_ref, body)
    else:
        x_ref, o_ref, h_ref = rest
        body(x_ref[0])


def _out_proj(a, w, ctx, x, mod, k_gate, norm_w, k_scale, k_shift, n_ctx):
    bsz, t, k = a.shape
    d = w.shape[1]
    nct = n_ctx // ROW_TILE
    split = ctx is not None

    def mod_spec(kk):
        return pl.BlockSpec((1, 1, d), lambda b, i: _mod_row(b, i, nct, bsz, kk))

    row_spec = pl.BlockSpec((1, ROW_TILE, d), lambda b, i: (b, i, 0))
    in_specs = [pl.BlockSpec((1, ROW_TILE, k), lambda b, i: (b, i, 0)),
                pl.BlockSpec((k, d), lambda b, i: (0, 0)),
                mod_spec(k_gate), pl.BlockSpec((1, d), lambda b, i: (0, 0)), mod_spec(k_scale), mod_spec(k_shift)]
    in_specs += _split_stream_specs(nct, d) if split else [row_spec]
    return pl.pallas_call(
        functools.partial(_out_proj_kernel, n_ctx_tiles=nct, split=split),
        grid=(bsz, t // ROW_TILE),
        in_specs=in_specs,
        out_specs=[row_spec, row_spec],
        out_shape=[jax.ShapeDtypeStruct((bsz, t, d), F32), jax.ShapeDtypeStruct((bsz, t, d), BF16)],
        input_output_aliases={} if split else {6: 0},
        compiler_params=_cparams(2),
        name="out_proj",
    )(a, w, mod, norm_w.reshape(1, d), mod, mod, *((ctx, x) if split else (x,)))


def _ffn_tail_kernel(up_ref, um_ref, un_ref, v_ref, cw_ref, cb_ref, shift_ref, wd_ref, x_ref, g_ref, nw_ref, *rest,
                     n_ctx_tiles, n_tiles, tile_offset, tc, last):
    if last:
        (o_ref,) = rest
    else:
        sc_ref, sh_ref, o_ref, h_ref = rest
    i = pl.program_id(1) + tile_offset
    is_ctx = i < n_ctx_tiles
    has_up = jnp.logical_and(jnp.logical_not(is_ctx), i > n_ctx_tiles)
    has_dn = jnp.logical_and(jnp.logical_not(is_ctx), i < n_tiles - 1)
    row_on = jnp.where(is_ctx, 0.0, 1.0)
    up_on = jnp.where(has_up, 1.0, 0.0)
    dn_on = jnp.where(has_dn, 1.0, 0.0)
    shift = shift_ref[jnp.where(is_ctx, 0, 1)]
    def conv_slab(j):
        cols = slice(j * tc, (j + 1) * tc)
        main = um_ref[0, :, cols].astype(F32)
        prev = up_ref[0, :, cols].astype(F32) * up_on
        nxt = un_ref[0, :, cols].astype(F32) * dn_on
        up = jnp.concatenate([prev, main[:ROW_TILE - GRID_W]], axis=0)
        dn = jnp.concatenate([main[GRID_W:], nxt], axis=0)
        w = cw_ref[:, cols]
        wu = w[0:3] * row_on
        wc = w[3:6]
        wd = w[6:9] * row_on

        def col(kw):
            return up * wu[kw:kw + 1] + main * wc[kw:kw + 1] + dn * wd[kw:kw + 1]

        sides = _dot(shift, jnp.concatenate([_bf(col(0)), _bf(col(2))], axis=0))
        return cols, sides + col(1) + cb_ref[:, cols]

    n_slabs = um_ref.shape[2] // tc
    acc = None
    pending = [conv_slab(j) for j in range(min(CONV_AHEAD, n_slabs))]
    for j in range(n_slabs):
        cols, u = pending.pop(0)
        if j + CONV_AHEAD < n_slabs:
            pending.append(conv_slab(j + CONV_AHEAD))
        act = _bf(_silu(u)) * v_ref[0, :, cols]
        part = _dot(act, wd_ref[cols, :])
        acc = part if acc is None else acc + part
    x = x_ref[0] + g_ref[0] * acc
    if last:
        o_ref[0] = x * lax.rsqrt(jnp.mean(x * x, axis=-1, keepdims=True) + NORM_EPS) * nw_ref[...]
    else:
        o_ref[0] = x
        h_ref[0] = _rms_mod(x, nw_ref[...], sc_ref[0], sh_ref[0]).astype(h_ref.dtype)


def _shift_constants(n_ctx):
    r = np.arange(ROW_TILE)
    out = np.zeros((2, ROW_TILE, 2 * ROW_TILE), np.float32)
    for kind, width in enumerate((n_ctx, GRID_W)):
        has_left = r % width != 0
        has_right = r % width != width - 1
        out[kind, r[has_left], r[has_left] - 1] = 1.0
        out[kind, r[has_right], ROW_TILE + r[has_right] + 1] = 1.0
    return out


def _ffn_tail(uv, conv_w, conv_b, w_down, x, mod, k_gate, norm_w, mod_next, k_scale, k_shift, n_ctx, tc):
    bsz, t, f2 = uv.shape
    f = f2 // 2
    d = w_down.shape[1]
    assert n_ctx == ROW_TILE and ROW_TILE % GRID_W == 0 and f % tc == 0
    last = mod_next is None
    nct = n_ctx // ROW_TILE
    nt = t // ROW_TILE
    off = nct if last else 0
    per = ROW_TILE // GRID_W
    nrow = t // GRID_W
    kern = functools.partial(_ffn_tail_kernel, n_ctx_tiles=nct, n_tiles=nt, tile_offset=off, tc=tc, last=last)

    def mod_spec(kk):
        return pl.BlockSpec((1, 1, d), lambda b, i: _mod_row(b, i + off, nct, bsz, kk))

    row_spec = pl.BlockSpec((1, ROW_TILE, d), lambda b, i: (b, i, 0))
    in_specs = [pl.BlockSpec((1, GRID_W, f), lambda b, i: (b, jnp.maximum((i + off) * per - 1, 0), 0)),
                pl.BlockSpec((1, ROW_TILE, f), lambda b, i: (b, i + off, 0)),
                pl.BlockSpec((1, GRID_W, f), lambda b, i: (b, jnp.minimum((i + off) * per + per, nrow - 1), 0)),
                pl.BlockSpec((1, ROW_TILE, f), lambda b, i: (b, i + off, 1)),
                pl.BlockSpec((9, f), lambda b, i: (0, 0)),
                pl.BlockSpec((1, f), lambda b, i: (0, 0)),
                pl.BlockSpec((2, ROW_TILE, 2 * ROW_TILE), lambda b, i: (0, 0, 0)),
                pl.BlockSpec((f, d), lambda b, i: (0, 0), pipeline_mode=pl.Buffered(1)),
                pl.BlockSpec((1, ROW_TILE, d), lambda b, i: (b, i + off, 0)),
                mod_spec(k_gate),
                pl.BlockSpec((1, d), lambda b, i: (0, 0))]
    args = [uv, uv, uv, uv, conv_w.reshape(9, f), conv_b.reshape(1, f), jnp.asarray(_shift_constants(n_ctx), BF16),
            w_down, x, mod, norm_w.reshape(1, d)]
    if last:
        return pl.pallas_call(
            kern, grid=(bsz, nt - off), in_specs=in_specs, out_specs=row_spec,
            out_shape=jax.ShapeDtypeStruct((bsz, t - n_ctx, d), F32),
            compiler_params=_cparams(2), name="ffn_tail_final",
        )(*args)
    return pl.pallas_call(
        kern, grid=(bsz, nt), in_specs=in_specs + [mod_spec(k_scale), mod_spec(k_shift)],
        out_specs=[row_spec, row_spec],
        out_shape=[jax.ShapeDtypeStruct(x.shape, F32), jax.ShapeDtypeStruct(x.shape, BF16)],
        input_output_aliases={8: 0},
        compiler_params=_cparams(2), name="ffn_tail",
    )(*args, mod_next, mod_next)


def _scan_both(group_fn, finish_fn, n_ctx_chunks, n_chunks, group, waves):
    assert n_chunks % group == 0 and group % waves == 0
    per_wave = group // waves
    init = jnp.zeros((HEAD_DIM, HEAD_DIM), F32)

    def body(i, carry):
        st = {True: carry[0], False: carry[1]}
        gens = []
        for w in range(waves):
            items = []
            for j in range(per_wave):
                p = i * group + w * per_wave + j
                items.append((p, True))
                items.append((jnp.where(p < n_ctx_chunks, n_ctx_chunks - 1 - p, n_chunks + n_ctx_chunks - 1 - p), False))
            gens.append(group_fn(items, st))
        live = set(range(waves))
        step = 0
        while live:
            for w in sorted(live):
                if step >= w:
                    try:
                        next(gens[w])
                    except StopIteration:
                        live.discard(w)
            step += 1
        return st[True], st[False]

    lax.fori_loop(0, n_chunks // group, body, (init, init))

    def finish(i, carry):
        finish_fn([_rows(i * group + j) for j in range(group)])
        return carry

    lax.fori_loop(0, n_chunks // group, finish, 0)


def _rows(c):
    return pl.ds(pl.multiple_of(c * CHUNK, CHUNK), CHUNK)


def _finish_heads(rows_list, o_ref, of_scr, zg_ref, nw_ref, centre):
    outs = [of_scr[r, :] + o_ref[0, r, :].astype(F32) for r in rows_list]
    if centre:
        mus = [_row_mean(o) for o in outs]
        outs = [o - mu for o, mu in zip(outs, mus)]
    mss = [_row_mean(o * o) for o in outs]
    for r, o, ms in zip(rows_list, outs, mss):
        o = o * lax.rsqrt(ms + NORM_EPS) * nw_ref[...]
        o_ref[0, r, :] = (o * _silu(zg_ref[0, r, :].astype(F32))).astype(o_ref.dtype)


def _level_constants():
    idx = np.arange(CHUNK)
    x = idx[:, None] ^ idx[None, :]
    lvl = np.where(x > 0, np.floor(np.log2(np.maximum(x, 1))), 6).astype(np.int32)
    lvl_f = np.where(idx[:, None] >= idx[None, :], lvl, 7).astype(np.int32)
    lvl_b = lvl_f.T.copy()
    def stack(fwd):
        tri = ((idx[None, :] <= idx[:, None]) if fwd else (idx[None, :] >= idx[:, None])).astype(np.float32)
        mats = [tri]
        for m in (32, 16, 8, 4, 2, 1):
            ref = (idx // (2 * m)) * (2 * m) + (m - 1 if fwd else m)
            mats.append(tri - tri[ref])
        return np.concatenate(mats, axis=0)
    return lvl_f, lvl_b, stack(True), stack(False)


class _Item:
    pass


def _hgrn_kernel(zq_ref, zi_ref, zff_ref, zfb_ref, zg_ref, lbf_ref, lbb_ref, nw_ref,
                 self_ref, selb_ref, lvlf_ref, lvlb_ref, merged_ref, o_ref, of_scr, *, n_ctx_chunks, n_chunks):
    del merged_ref
    def group_fn(items, st):
        its = []
        for c, fwd in items:
            it = _Item()
            it.fwd, it.rows = fwd, _rows(c)
            zf_ref, lb_ref = (zff_ref, lbf_ref) if fwd else (zfb_ref, lbb_ref)
            it.q = _silu(zq_ref[0, it.rows, :].astype(F32)) * SCALE
            it.v = zi_ref[0, it.rows, :]
            zf = zf_ref[0, it.rows, :].astype(F32)
            lb = lb_ref[...]
            logf = _log_sigmoid(zf) + jnp.log1p(lb * jnp.exp(-jnp.maximum(zf, -GATE_CLAMP)))
            it.k = (1.0 - lb) * jax.nn.sigmoid(-zf)
            it.qb, it.kb = _bf(it.q), _bf(it.k)
            it.cat = jnp.concatenate(_split2(logf * LOG2_E), axis=1)
            its.append(it)
        yield
        for it in its:
            cs = _dot((self_ref if it.fwd else selb_ref)[...], it.cat)
            it.cs = cs[:, :LANES] + cs[:, LANES:]
            it.b = it.cs[:CHUNK]
            it.lvl = (lvlf_ref if it.fwd else lvlb_ref)[...]
        yield
        for it in its:
            it.scores = jnp.where(it.lvl == 6, _dot_nt(it.qb, it.kb), 0.0)
        for n in range(6):
            yield
            for it in its:
                wgt = _bf(jnp.exp2(-jnp.abs(it.cs[(n + 1) * CHUNK:(n + 2) * CHUNK])))
                it.scores = jnp.where(it.lvl == 5 - n, _dot_nt(it.qb * wgt, it.kb * wgt), it.scores)
        yield
        for it in its:
            b_tot = it.b[CHUNK - 1:CHUNK] if it.fwd else it.b[0:1]
            it.intra = _dot(_bf(it.scores), it.v)
            it.upd = _dot_tn(it.v, _bf(it.k * jnp.exp2(b_tot - it.b)))
            it.qd = _bf(it.q * jnp.exp2(it.b))
            it.dec = jnp.exp2(b_tot)
        yield
        for it in its:
            out = it.intra + _dot_nt(it.qd, _bf(st[it.fwd]))
            st[it.fwd] = st[it.fwd] * it.dec + it.upd
            if it.fwd:
                of_scr[it.rows, :] = out
            else:
                o_ref[0, it.rows, :] = out.astype(o_ref.dtype)

    def finish_fn(rows_list):
        _finish_heads(rows_list, o_ref, of_scr, zg_ref, nw_ref, centre=False)

    _scan_both(group_fn, finish_fn, n_ctx_chunks, n_chunks, HGRN_GROUP, HGRN_WAVES)


def _zspec(t, base):
    return pl.BlockSpec((1, t, LANES), lambda b, h: (b, 0, base + h))


def _const_spec(shape):
    return pl.BlockSpec(shape, lambda b, h: (0,) * len(shape))


def _hgrn(z, merged, lb_f, lb_b, norm_w, n_ctx):
    bsz, t, _ = z.shape
    lvl_f, lvl_b, sel_f, sel_b = _level_constants()
    kern = functools.partial(_hgrn_kernel, n_ctx_chunks=n_ctx // CHUNK, n_chunks=t // CHUNK)
    head_vec = pl.BlockSpec((1, LANES), lambda b, h: (0, h))
    return pl.pallas_call(
        kern,
        grid=(bsz, A_HEADS),
        in_specs=[_zspec(t, ZB_AQ), _zspec(t, ZB_AI), _zspec(t, ZB_AFF), _zspec(t, ZB_AFB), _zspec(t, ZB_AG),
                  head_vec, head_vec, _const_spec((1, LANES)),
                  _const_spec(sel_f.shape), _const_spec(sel_b.shape),
                  _const_spec(lvl_f.shape), _const_spec(lvl_b.shape),
                  pl.BlockSpec(memory_space=pl.ANY)],
        out_specs=pl.BlockSpec((1, t, LANES), lambda b, h: (b, 0, h)),
        out_shape=jax.ShapeDtypeStruct(merged.shape, BF16),
        input_output_aliases={12: 0},
        scratch_shapes=[pltpu.VMEM((t, HEAD_DIM), F32)],
        compiler_params=_cparams(2),
        name="hgrn",
    )(z, z, z, z, z, lb_f.reshape(1, A_W), lb_b.reshape(1, A_W), norm_w.reshape(1, LANES),
      jnp.asarray(sel_f, BF16), jnp.asarray(sel_b, BF16), jnp.asarray(lvl_f), jnp.asarray(lvl_b), merged)


def _gdn_kernel(zq_ref, zk_ref, zv_ref, zg_ref, wq_ref, wk_ref, wv_ref, gc_ref, gr_ref, par_ref, nw_ref,
                lvlf_ref, lvlb_ref, merged_ref, o_ref, q_scr, k_scr, v_scr, of_scr, *, n_ctx_chunks, n_chunks):
    del merged_ref
    t_total = n_chunks * CHUNK
    ri = lax.broadcasted_iota(jnp.int32, (CHUNK, CHUNK), 0)
    ci = lax.broadcasted_iota(jnp.int32, (CHUNK, CHUNK), 1)
    row_id = lax.broadcasted_iota(jnp.int32, (CHUNK, 1), 0)
    eye = jnp.where(ri == ci, 1.0, 0.0)

    def conv(z_ref, w_ref, c):
        has_prev = jnp.logical_and(c != 0, c != n_ctx_chunks)
        has_next = jnp.logical_and(c != n_ctx_chunks - 1, c != n_chunks - 1)
        p0 = pl.multiple_of(jnp.maximum(c * CHUNK - 16, 0), 16)
        n0 = pl.multiple_of(jnp.minimum(c * CHUNK + CHUNK, t_total - 16), 16)
        x = z_ref[0, _rows(c), :].astype(F32)
        xp = z_ref[0, pl.ds(p0, 16), :].astype(F32)[15:16] * jnp.where(has_prev, 1.0, 0.0)
        xn = z_ref[0, pl.ds(n0, 16), :].astype(F32)[0:1] * jnp.where(has_next, 1.0, 0.0)
        xm1 = jnp.where(row_id == 0, xp, pltpu.roll(x, 1, 0))
        xp1 = jnp.where(row_id == CHUNK - 1, xn, pltpu.roll(x, CHUNK - 1, 0))
        w = w_ref[...]
        return _silu(xm1 * w[0:1] + x * w[1:2] + xp1 * w[2:3])

    def prep(i, carry):
        cs = [i * PREP_GROUP + j for j in range(PREP_GROUP)]
        qs = [conv(zq_ref, wq_ref, c) for c in cs]
        ks = [conv(zk_ref, wk_ref, c) for c in cs]
        q_ss = [_row_mean(a * a) * HEAD_DIM for a in qs]
        k_ss = [_row_mean(a * a) * HEAD_DIM for a in ks]
        for c, q, k, qn, kn in zip(cs, qs, ks, q_ss, k_ss):
            q_scr[_rows(c), :] = (q * (lax.rsqrt(qn + NORM_EPS) * SCALE)).astype(q_scr.dtype)
            k_scr[_rows(c), :] = (k * lax.rsqrt(kn + NORM_EPS)).astype(k_scr.dtype)
            v_scr[_rows(c), :] = conv(zv_ref, wv_ref, c).astype(v_scr.dtype)
        return carry

    lax.fori_loop(0, n_chunks // PREP_GROUP, prep, 0)

    def group_fn(items, st):
        par = par_ref[0]
        its = []
        for c, fwd in items:
            it = _Item()
            it.fwd, it.rows = fwd, _rows(c)
            ia, ib, pr = (0, 2, 0) if fwd else (1, 3, 2)
            neg_a = -jnp.exp(par[pr:pr + 1, 0:1])
            dt = par[pr + 1:pr + 2, 0:1]
            it.q = q_scr[it.rows, :]
            it.k = k_scr[it.rows, :]
            it.v = v_scr[it.rows, :].astype(F32)
            g_col = neg_a * _softplus(gc_ref[0, 0, it.rows, ia:ia + 1] + dt)
            g_row = neg_a * _softplus(gr_ref[0, 0, ia, pl.ds(c, 1), :] + dt)
            it.beta = jax.nn.sigmoid(gc_ref[0, 0, it.rows, ib:ib + 1])
            incl = (ci <= ri) if fwd else (ci >= ri)
            incl_t = (ri <= ci) if fwd else (ri >= ci)
            it.bc = jnp.sum(jnp.where(incl, g_row, 0.0), axis=1, keepdims=True)
            bc_row = jnp.sum(jnp.where(incl_t, g_col, 0.0), axis=0, keepdims=True)
            it.lmat = jnp.where(incl, jnp.exp(jnp.minimum(it.bc - bc_row, 0.0)), 0.0)
            it.lvl = (lvlf_ref if fwd else lvlb_ref)[...]
            its.append(it)
        yield
        for it in its:
            it.amat = it.beta * _dot_nt(it.k, it.k) * it.lmat
            it.qk = _bf(_dot_nt(it.q, it.k) * it.lmat)
        yield
        for it in its:
            it.x = eye - jnp.where(it.lvl == 0, it.amat, 0.0)
        for lev in range(1, 6):
            for it in its:
                it.xb = _bf(it.x)
                it.y = _dot(_bf(jnp.where(it.lvl == lev, it.amat, 0.0)), it.xb)
            yield
            for it in its:
                it.x = it.x - _dot(it.xb, _bf(it.y))
            yield
        for it in its:
            e_col = jnp.exp(it.bc)
            kf = it.k.astype(F32)
            sol = _dot(_bf(it.x), _bf(jnp.concatenate([it.v * it.beta, kf * (it.beta * e_col)], axis=1)))
            it.u = _bf(sol[:, :HEAD_DIM])
            it.w = _bf(sol[:, HEAD_DIM:])
            bc_tot = it.bc[CHUNK - 1:CHUNK] if it.fwd else it.bc[0:1]
            it.qd = it.q.astype(F32) * e_col
            it.kd = _bf(kf * jnp.exp(bc_tot - it.bc))
            it.last = jnp.exp(bc_tot)
        yield
        for it in its:
            it.kw = _bf(_dot_tn(it.kd, it.w))
            it.ku = _dot_tn(it.kd, it.u)
            it.qs = _bf(it.qd - _dot(it.qk, it.w))
            it.o0 = _dot(it.qk, it.u)
        yield
        for it in its:
            s = st[it.fwd]
            sb = _bf(s)
            out = it.o0 + _dot(it.qs, sb)
            st[it.fwd] = it.last * s + (it.ku - _dot(it.kw, sb))
            if it.fwd:
                of_scr[it.rows, :] = out
            else:
                o_ref[0, it.rows, :] = out.astype(o_ref.dtype)

    def finish_fn(rows_list):
        _finish_heads(rows_list, o_ref, of_scr, zg_ref, nw_ref, centre=False)

    _scan_both(group_fn, finish_fn, n_ctx_chunks, n_chunks, GDN_GROUP, GDN_WAVES)


def _gdn(z, zgate, merged, conv_w, a_log_f, a_log_b, dt_f, dt_b, norm_w, n_ctx):
    bsz, t, _ = z.shape
    n = t // CHUNK
    g = zgate[:, :, :N_GATES].reshape(bsz, t, 4, B_HEADS)
    gate_cols = g.transpose(0, 3, 1, 2)
    gate_rows = g.reshape(bsz, n, CHUNK, 4, B_HEADS).transpose(0, 4, 3, 1, 2)
    par = jnp.stack([a_log_f, dt_f, a_log_b, dt_b] + [jnp.zeros_like(dt_f)] * 4, axis=1)
    par = jnp.broadcast_to(par[:, :, None], (B_HEADS, 8, LANES)).astype(F32)
    lvl_f, lvl_b, _, _ = _level_constants()
    kern = functools.partial(_gdn_kernel, n_ctx_chunks=n_ctx // CHUNK, n_chunks=n)

    def wspec(base):
        return pl.BlockSpec((3, LANES), lambda b, h: (0, base + h))

    return pl.pallas_call(
        kern,
        grid=(bsz, B_HEADS),
        in_specs=[_zspec(t, ZB_BQ), _zspec(t, ZB_BK), _zspec(t, ZB_BV), _zspec(t, ZB_BG),
                  wspec(0), wspec(B_HEADS), wspec(2 * B_HEADS),
                  pl.BlockSpec((1, 1, t, 4), lambda b, h: (b, h, 0, 0)),
                  pl.BlockSpec((1, 1, 4, n, CHUNK), lambda b, h: (b, h, 0, 0, 0)),
                  pl.BlockSpec((1, 8, LANES), lambda b, h: (h, 0, 0)),
                  _const_spec((1, LANES)), _const_spec(lvl_f.shape), _const_spec(lvl_b.shape),
                  pl.BlockSpec(memory_space=pl.ANY)],
        out_specs=pl.BlockSpec((1, t, LANES), lambda b, h: (b, 0, A_HEADS + h)),
        out_shape=jax.ShapeDtypeStruct(merged.shape, BF16),
        input_output_aliases={13: 0},
        scratch_shapes=[pltpu.VMEM((t, HEAD_DIM), BF16)] * 3 + [pltpu.VMEM((t, HEAD_DIM), F32)],
        compiler_params=_cparams(2),
        name="gdn",
    )(z, z, z, z, conv_w, conv_w, conv_w, gate_cols, gate_rows, par, norm_w.reshape(1, LANES),
      jnp.asarray(lvl_f), jnp.asarray(lvl_b), merged)


def _ret_kernel(zq_ref, zk_ref, zv_ref, zg_ref, base_ref, loc_ref, dec_ref, dmat_ref, cdec_ref, nw_ref,
                merged_ref, o_ref, of_scr, *, n_ctx_chunks, n_chunks):
    del merged_ref
    def group_fn(items, st):
        its = []
        for c, fwd in items:
            it = _Item()
            it.fwd, it.rows, it.d = fwd, _rows(c), 0 if fwd else 1
            d = it.d
            cb = base_ref[2 * d, pl.ds(c, 1), :]
            sb = base_ref[2 * d + 1, pl.ds(c, 1), :]
            cl, sl, cls, sls = (loc_ref[4 * d + n] for n in range(4))
            cos = cb * cl - sb * sl
            sin = sb * cls + cb * sls

            def rot(a):
                return a * cos + pltpu.roll(a, HEAD_DIM // 2, 1) * sin

            q = rot(zq_ref[0, it.rows, :].astype(F32) * SCALE)
            k = rot(zk_ref[0, it.rows, :].astype(F32))
            it.q, it.k = _bf(q), _bf(k)
            it.qd = _bf(q * dec_ref[0, 2 * d])
            it.kd = _bf(k * dec_ref[0, 2 * d + 1])
            it.v = zv_ref[0, it.rows, :]
            its.append(it)
        yield
        for it in its:
            it.scores = _bf(_dot_nt(it.q, it.k) * dmat_ref[0, it.d])
        yield
        for it in its:
            it.intra = _dot(it.scores, it.v)
            it.upd = _dot_tn(it.v, it.kd)
        yield
        for it in its:
            out = it.intra + _dot_nt(it.qd, _bf(st[it.fwd]))
            st[it.fwd] = st[it.fwd] * cdec_ref[0, it.d:it.d + 1, :] + it.upd
            if it.fwd:
                of_scr[it.rows, :] = out
            else:
                o_ref[0, it.rows, :] = out.astype(o_ref.dtype)

    def finish_fn(rows_list):
        _finish_heads(rows_list, o_ref, of_scr, zg_ref, nw_ref, centre=True)

    _scan_both(group_fn, finish_fn, n_ctx_chunks, n_chunks, RET_GROUP, RET_WAVES)


def _retention_tables(n_ctx, t):
    half = HEAD_DIM // 2
    n = t // CHUNK
    nc = n_ctx // CHUNK
    inv = ROPE_BASE ** (-jnp.arange(half, dtype=F32) / half)
    inv = jnp.concatenate([inv, inv])
    sgn = jnp.concatenate([-jnp.ones((half,), F32), jnp.ones((half,), F32)])
    c_idx = np.arange(n)
    base_f = (c_idx * CHUNK).astype(np.float32)
    base_b = np.where(c_idx < nc, n_ctx - CHUNK * (c_idx + 1), t - CHUNK * (c_idx - nc + 1)).astype(np.float32)
    loc_f = np.arange(CHUNK, dtype=np.float32)
    loc_b = loc_f[::-1].copy()

    def ang(p):
        return jnp.asarray(p)[:, None] * inv[None, :]

    base = jnp.stack([jnp.cos(ang(base_f)), jnp.sin(ang(base_f)), jnp.cos(ang(base_b)), jnp.sin(ang(base_b))])
    loc = jnp.stack([jnp.cos(ang(loc_f)), jnp.sin(ang(loc_f)), jnp.cos(ang(loc_f)) * sgn, jnp.sin(ang(loc_f)) * sgn,
                     jnp.cos(ang(loc_b)), jnp.sin(ang(loc_b)), jnp.cos(ang(loc_b)) * sgn, jnp.sin(ang(loc_b)) * sgn])
    lg_f = jnp.log1p(-jnp.exp2(-5.0 - jnp.arange(C_HEADS, dtype=F32)))
    lg_b = lg_f[::-1]
    idx = jnp.arange(CHUNK, dtype=F32)
    rel = idx[:, None] - idx[None, :]

    def masked_exp(a, mask):
        return jnp.where(mask, jnp.exp(jnp.where(mask, a, 0.0)), 0.0)

    dmat_f = masked_exp(lg_f[:, None, None] * rel, (rel >= 0)[None])
    dmat_b = masked_exp(-lg_b[:, None, None] * rel, (rel <= 0)[None])
    dq_f = jnp.exp(lg_f[:, None] * (idx + 1.0))
    dk_f = jnp.exp(lg_f[:, None] * (CHUNK - 1.0 - idx))
    dq_b = jnp.exp(lg_b[:, None] * (CHUNK - idx))
    dk_b = jnp.exp(lg_b[:, None] * idx)
    dec = jnp.stack([dq_f, dk_f, dq_b, dk_b], axis=1)
    dec = jnp.broadcast_to(dec[..., None], (C_HEADS, 4, CHUNK, LANES))
    cdec = jnp.stack([jnp.exp(lg_f * CHUNK), jnp.exp(lg_b * CHUNK)] + [jnp.zeros((C_HEADS,), F32)] * 6, axis=1)
    cdec = jnp.broadcast_to(cdec[..., None], (C_HEADS, 8, LANES))
    return base, loc, dec, jnp.stack([dmat_f, dmat_b], axis=1), cdec


def _retention(z, merged, norm_w, n_ctx):
    bsz, t, _ = z.shape
    n = t // CHUNK
    base, loc, dec, dmat, cdec = _retention_tables(n_ctx, t)
    kern = functools.partial(_ret_kernel, n_ctx_chunks=n_ctx // CHUNK, n_chunks=n)
    return pl.pallas_call(
        kern,
        grid=(bsz, C_HEADS),
        in_specs=[_zspec(t, ZB_CQ), _zspec(t, ZB_CK), _zspec(t, ZB_CV), _zspec(t, ZB_CG),
                  _const_spec(base.shape), _const_spec(loc.shape),
                  pl.BlockSpec((1, 4, CHUNK, LANES), lambda b, h: (h, 0, 0, 0)),
                  pl.BlockSpec((1, 2, CHUNK, CHUNK), lambda b, h: (h, 0, 0, 0)),
                  pl.BlockSpec((1, 8, LANES), lambda b, h: (h, 0, 0)),
                  pl.BlockSpec((1, LANES), lambda b, h: (0, h)),
                  pl.BlockSpec(memory_space=pl.ANY)],
        out_specs=pl.BlockSpec((1, t, LANES), lambda b, h: (b, 0, A_HEADS + B_HEADS + h)),
        out_shape=jax.ShapeDtypeStruct(merged.shape, BF16),
        input_output_aliases={10: 0},
        scratch_shapes=[pltpu.VMEM((t, HEAD_DIM), F32)],
        compiler_params=_cparams(2),
        name="retention",
    )(z, z, z, z, base, loc, dec, dmat, cdec, norm_w.reshape(1, C_W), merged)


def _reorder_in_proj(w):
    g0 = 5 * A_W + 4 * B_W
    main = jnp.concatenate([w[:, :g0], w[:, g0 + N_GATES:]], axis=1).astype(BF16)
    gates = jnp.pad(w[:, g0:g0 + N_GATES], ((0, 0), (0, LANES - N_GATES))).astype(BF16)
    return main, gates


def _lower_bounds(logits):
    p = jax.nn.softmax(logits.astype(F32), axis=0)
    return jnp.cumsum(p, axis=0) - p[0]


def kernel(x, c, ctx, c_ctx, w_mod, b_mod, norm1_w, norm2_w, w_in, qkv_conv_w, hgrn_lb_logits_fwd, hgrn_lb_logits_bwd, hgrn_norm_w, gdn_a_log_fwd, gdn_a_log_bwd, gdn_dt_bias_fwd, gdn_dt_bias_bwd, gdn_norm_w, ret_norm_w, w_out, ffn_w_up, ffn_conv_w, ffn_conv_b, ffn_w_down, final_norm_w):
    bsz, n_lat, d = x.shape
    n_ctx = ctx.shape[1]
    depth = w_mod.shape[0]
    t = n_ctx + n_lat
    assert n_ctx % ROW_TILE == 0 and n_lat % ROW_TILE == 0 and bsz < MOD_ROWS
    lb_fwd = _lower_bounds(hgrn_lb_logits_fwd)
    lb_bwd = _lower_bounds(hgrn_lb_logits_bwd)

    rows = jnp.concatenate([c, c_ctx[None, :], jnp.zeros((MOD_ROWS - bsz - 1, d), F32)], axis=0)
    mods = _modulation(rows, w_mod, b_mod)

    tm = MM_ROWS
    m_rows = bsz * t
    f2 = ffn_w_up.shape[2]
    zw = Z_BLOCKS * LANES
    assert m_rows % tm == 0 and zw % (IN_PROJ_COL_TILES * LANES) == 0 and f2 % (FFN_UP_COL_TILES * LANES) == 0
    mod = [mods[l].reshape(MOD_ROWS * N_MOD, 1, d) for l in range(depth)]
    h = _norm_mod(ctx, x, norm1_w[0], mod[0], 1, 0)
    xs = x
    for l in range(depth):
        w_main, w_gate = _reorder_in_proj(w_in[l])
        h = h.reshape(m_rows, d)
        z = _matmul(h, w_main, tm, zw // IN_PROJ_COL_TILES, BF16, "in_proj").reshape(bsz, t, zw)
        zgate = _matmul(h, w_gate, tm, LANES, F32, "in_proj_gates").reshape(bsz, t, LANES)
        merged = jnp.zeros((bsz, t, N_HEADS * HEAD_DIM), BF16)
        merged = _hgrn(z, merged, lb_fwd[l], lb_bwd[l], hgrn_norm_w[l], n_ctx)
        merged = _gdn(z, zgate, merged, qkv_conv_w[l], gdn_a_log_fwd[l], gdn_a_log_bwd[l],
                      gdn_dt_bias_fwd[l], gdn_dt_bias_bwd[l], gdn_norm_w[l], n_ctx)
        merged = _retention(z, merged, ret_norm_w[l], n_ctx)
        xs, h = _out_proj(merged, w_out[l].astype(BF16), ctx if l == 0 else None, xs, mod[l], 2,
                          norm2_w[l], 4, 3, n_ctx)

        uv = _matmul(h.reshape(m_rows, d), ffn_w_up[l].astype(BF16), tm, f2 // FFN_UP_COL_TILES, BF16,
                     "ffn_up").reshape(bsz, t, f2)
        w_down = ffn_w_down[l].astype(BF16)
        if l + 1 < depth:
            xs, h = _ffn_tail(uv, ffn_conv_w[l], ffn_conv_b[l], w_down, xs, mod[l], 5,
                              norm1_w[l + 1], mod[l + 1], 1, 0, n_ctx, FFN_SLAB)
    return _ffn_tail(uv, ffn_conv_w[-1], ffn_conv_b[-1], w_down, xs, mod[-1], 5,
                     final_norm_w, None, None, None, n_ctx, FFN_SLAB)
```

```python
import functools
import math

import numpy as np
import jax
import jax.numpy as jnp
from jax import lax
from jax.experimental import pallas as pl
from jax.experimental.pallas import tpu as pltpu

F32 = jnp.float32
BF16 = jnp.bfloat16

HEAD_DIM = 128
N_HEADS = 16
A_HEADS = 5
C_HEADS = 5
B_HEADS = 6
A_W = A_HEADS * HEAD_DIM
B_W = B_HEADS * HEAD_DIM
C_W = C_HEADS * HEAD_DIM
N_GATES = 4 * B_HEADS
CHUNK = 64
GRID_W = 64
ROPE_BASE = 10000.0
NORM_EPS = 1e-6
N_MOD = 6
GATE_CLAMP = 60.0
SCALE = HEAD_DIM ** -0.5
LOG2_E = 1.4426950408889634
LANES = 128
MOD_ROWS = 8
ROW_TILE = 256
VMEM_LIMIT = 56 * 1024 * 1024
HGRN_GROUP, HGRN_WAVES = 11, 1
GDN_GROUP, GDN_WAVES = 33, 33
RET_GROUP, RET_WAVES = 33, 33
PREP_GROUP = 4
CONV_AHEAD = 1
FFN_SLAB = 512
MM_ROWS = 512
IN_PROJ_COL_TILES = 3
FFN_UP_COL_TILES = 4

ZB_AQ, ZB_AI, ZB_AFF, ZB_AFB, ZB_AG = 0, 5, 10, 15, 20
ZB_BQ, ZB_BK, ZB_BV, ZB_BG = 25, 31, 37, 43
ZB_CQ, ZB_CK, ZB_CV, ZB_CG = 49, 54, 59, 64
Z_BLOCKS = 69


def _cparams(n_axes):
    return pltpu.CompilerParams(dimension_semantics=("arbitrary",) * n_axes,
                                vmem_limit_bytes=VMEM_LIMIT)


def _dot(a, b):
    return jnp.dot(a, b, preferred_element_type=F32)


def _dot_nt(a, b):
    return lax.dot_general(a, b, (((1,), (1,)), ((), ())), preferred_element_type=F32)


def _dot_tn(a, b):
    return lax.dot_general(a, b, (((0,), (0,)), ((), ())), preferred_element_type=F32)


def _bf(a):
    return a.astype(BF16)


def _split2(a):
    hi = a.astype(BF16)
    lo = (a - hi.astype(F32)).astype(BF16)
    return hi, lo


def _dot3(a, b):
    ah, al = _split2(a)
    bh, bl = _split2(b)
    return _dot(ah, bh) + (_dot(al, bh) + _dot(ah, bl))


def _row_mean(a):
    hi, lo = _split2(a)
    ones = jnp.ones((a.shape[1], a.shape[1]), BF16)
    return (_dot(hi, ones) + _dot(lo, ones)) * (1.0 / a.shape[1])


def _silu(a):
    return a * jax.nn.sigmoid(a)


def _softplus(a):
    return jnp.maximum(a, 0.0) + jnp.log1p(jnp.exp(-jnp.abs(a)))


def _log_sigmoid(a):
    return jnp.minimum(a, 0.0) - jnp.log1p(jnp.exp(-jnp.abs(a)))


def _mod_kernel(a_ref, w_ref, b_ref, o_ref):
    a = _silu(a_ref[...])
    o_ref[0] = _dot3(a, w_ref[0]) + b_ref[0]


def _modulation(rows, w_mod, b_mod):
    depth, d, n = w_mod.shape
    tn = 1024
    return pl.pallas_call(
        _mod_kernel,
        grid=(depth, n // tn),
        in_specs=[pl.BlockSpec((MOD_ROWS, d), lambda l, j: (0, 0)),
                  pl.BlockSpec((1, d, tn), lambda l, j: (l, 0, j)),
                  pl.BlockSpec((1, 1, tn), lambda l, j: (l, 0, j))],
        out_specs=pl.BlockSpec((1, MOD_ROWS, tn), lambda l, j: (l, 0, j)),
        out_shape=jax.ShapeDtypeStruct((depth, MOD_ROWS, n), F32),
        compiler_params=_cparams(2),
        name="modulation",
    )(rows, w_mod, b_mod.reshape(depth, 1, n))


def _mod_row(b, i, n_ctx_tiles, n_batch, k):
    return (jnp.where(i < n_ctx_tiles, n_batch, b) * N_MOD + k, 0, 0)


def _split_stream_specs(n_ctx_tiles, d):
    return [pl.BlockSpec((1, ROW_TILE, d), lambda b, i: (b, jnp.minimum(i, n_ctx_tiles - 1), 0)),
            pl.BlockSpec((1, ROW_TILE, d), lambda b, i: (b, jnp.maximum(i - n_ctx_tiles, 0), 0))]


def _for_stream_tile(n_ctx_tiles, ctx_ref, lat_ref, body):
    i = pl.program_id(1)

    @pl.when(i < n_ctx_tiles)
    def _():
        body(ctx_ref[0])

    @pl.when(i >= n_ctx_tiles)
    def _():
        body(lat_ref[0])


def _norm_mod_kernel(ctx_ref, lat_ref, w_ref, sc_ref, sh_ref, o_ref, *, n_ctx_tiles):
    def body(x):
        o_ref[0] = _rms_mod(x, w_ref[...], sc_ref[0], sh_ref[0]).astype(o_ref.dtype)

    _for_stream_tile(n_ctx_tiles, ctx_ref, lat_ref, body)


def _norm_mod(ctx, x, w, mod, k_scale, k_shift):
    bsz, n_ctx, d = ctx.shape
    t = n_ctx + x.shape[1]
    nct = n_ctx // ROW_TILE
    return pl.pallas_call(
        functools.partial(_norm_mod_kernel, n_ctx_tiles=nct),
        grid=(bsz, t // ROW_TILE),
        in_specs=_split_stream_specs(nct, d) + [
            pl.BlockSpec((1, d), lambda b, i: (0, 0)),
            pl.BlockSpec((1, 1, d), lambda b, i: _mod_row(b, i, nct, bsz, k_scale)),
            pl.BlockSpec((1, 1, d), lambda b, i: _mod_row(b, i, nct, bsz, k_shift))],
        out_specs=pl.BlockSpec((1, ROW_TILE, d), lambda b, i: (b, i, 0)),
        out_shape=jax.ShapeDtypeStruct((bsz, t, d), BF16),
        compiler_params=_cparams(2),
        name="norm_mod",
    )(ctx, x, w.reshape(1, d), mod, mod)


def _mm_kernel(a_ref, w_ref, o_ref):
    o_ref[...] = _dot(a_ref[...], w_ref[...]).astype(o_ref.dtype)


def _matmul(a, w, tm, tn, out_dtype, name):
    m, k = a.shape
    n = w.shape[1]
    return pl.pallas_call(
        _mm_kernel,
        grid=(n // tn, m // tm),
        in_specs=[pl.BlockSpec((tm, k), lambda j, i: (i, 0)),
                  pl.BlockSpec((k, tn), lambda j, i: (0, j))],
        out_specs=pl.BlockSpec((tm, tn), lambda j, i: (i, j)),
        out_shape=jax.ShapeDtypeStruct((m, n), out_dtype),
        compiler_params=_cparams(2),
        name=name,
    )(a, w)


def _rms_mod(x, w, scale, shift):
    return (x * lax.rsqrt(jnp.mean(x * x, axis=-1, keepdims=True) + NORM_EPS) * w) * (1.0 + scale) + shift


def _out_proj_kernel(a_ref, w_ref, g_ref, nw_ref, sc_ref, sh_ref, *rest, n_ctx_tiles, split):
    def body(x):
        x = x + g_ref[0] * _dot(a_ref[0], w_ref[...])
        o_ref[0] = x
        h_ref[0] = _rms_mod(x, nw_ref[...], sc_ref[0], sh_ref[0]).astype(h_ref.dtype)

    if split:
        ctx_ref, lat_ref, o_ref, h_ref = rest
        _for_stream_tile(n_ctx_tiles, ctx_ref, lat_ref, body)
    else:
        x_ref, o_ref, h_ref = rest
        body(x_ref[0])


def _out_proj(a, w, ctx, x, mod, k_gate, norm_w, k_scale, k_shift, n_ctx):
    bsz, t, k = a.shape
    d = w.shape[1]
    nct = n_ctx // ROW_TILE
    split = ctx is not None

    def mod_spec(kk):
        return pl.BlockSpec((1, 1, d), lambda b, i: _mod_row(b, i, nct, bsz, kk))

    row_spec = pl.BlockSpec((1, ROW_TILE, d), lambda b, i: (b, i, 0))
    in_specs = [pl.BlockSpec((1, ROW_TILE, k), lambda b, i: (b, i, 0)),
                pl.BlockSpec((k, d), lambda b, i: (0, 0)),
                mod_spec(k_gate), pl.BlockSpec((1, d), lambda b, i: (0, 0)), mod_spec(k_scale), mod_spec(k_shift)]
    in_specs += _split_stream_specs(nct, d) if split else [row_spec]
    return pl.pallas_call(
        functools.partial(_out_proj_kernel, n_ctx_tiles=nct, split=split),
        grid=(bsz, t // ROW_TILE),
        in_specs=in_specs,
        out_specs=[row_spec, row_spec],
        out_shape=[jax.ShapeDtypeStruct((bsz, t, d), F32), jax.ShapeDtypeStruct((bsz, t, d), BF16)],
        input_output_aliases={} if split else {6: 0},
        compiler_params=_cparams(2),
        name="out_proj",
    )(a, w, mod, norm_w.reshape(1, d), mod, mod, *((ctx, x) if split else (x,)))


def _ffn_tail_kernel(up_ref, um_ref, un_ref, v_ref, cw_ref, cb_ref, shift_ref, wd_ref, x_ref, g_ref, nw_ref, *rest,
                     n_ctx_tiles, n_tiles, tile_offset, tc, last):
    if last:
        (o_ref,) = rest
    else:
        sc_ref, sh_ref, o_ref, h_ref = rest
    i = pl.program_id(1) + tile_offset
    is_ctx = i < n_ctx_tiles
    has_up = jnp.logical_and(jnp.logical_not(is_ctx), i > n_ctx_tiles)
    has_dn = jnp.logical_and(jnp.logical_not(is_ctx), i < n_tiles - 1)
    row_on = jnp.where(is_ctx, 0.0, 1.0)
    up_on = jnp.where(has_up, 1.0, 0.0)
    dn_on = jnp.where(has_dn, 1.0, 0.0)
    shift = shift_ref[jnp.where(is_ctx, 0, 1)]
    def conv_slab(j):
        cols = slice(j * tc, (j + 1) * tc)
        main = um_ref[0, :, cols].astype(F32)
        prev = up_ref[0, :, cols].astype(F32) * up_on
        nxt = un_ref[0, :, cols].astype(F32) * dn_on
        up = jnp.concatenate([prev, main[:ROW_TILE - GRID_W]], axis=0)
        dn = jnp.concatenate([main[GRID_W:], nxt], axis=0)
        w = cw_ref[:, cols]
        wu = w[0:3] * row_on
        wc = w[3:6]
        wd = w[6:9] * row_on

        def col(kw):
            return up * wu[kw:kw + 1] + main * wc[kw:kw + 1] + dn * wd[kw:kw + 1]

        sides = _dot(shift, jnp.concatenate([_bf(col(0)), _bf(col(2))], axis=0))
        return cols, sides + col(1) + cb_ref[:, cols]

    n_slabs = um_ref.shape[2] // tc
    acc = None
    pending = [conv_slab(j) for j in range(min(CONV_AHEAD, n_slabs))]
    for j in range(n_slabs):
        cols, u = pending.pop(0)
        if j + CONV_AHEAD < n_slabs:
            pending.append(conv_slab(j + CONV_AHEAD))
        act = _bf(_silu(u)) * v_ref[0, :, cols]
        part = _dot(act, wd_ref[cols, :])
        acc = part if acc is None else acc + part
    x = x_ref[0] + g_ref[0] * acc
    if last:
        o_ref[0] = x * lax.rsqrt(jnp.mean(x * x, axis=-1, keepdims=True) + NORM_EPS) * nw_ref[...]
    else:
        o_ref[0] = x
        h_ref[0] = _rms_mod(x, nw_ref[...], sc_ref[0], sh_ref[0]).astype(h_ref.dtype)


def _shift_constants(n_ctx):
    r = np.arange(ROW_TILE)
    out = np.zeros((2, ROW_TILE, 2 * ROW_TILE), np.float32)
    for kind, width in enumerate((n_ctx, GRID_W)):
        has_left = r % width != 0
        has_right = r % width != width - 1
        out[kind, r[has_left], r[has_left] - 1] = 1.0
        out[kind, r[has_right], ROW_TILE + r[has_right] + 1] = 1.0
    return out


def _ffn_tail(uv, conv_w, conv_b, w_down, x, mod, k_gate, norm_w, mod_next, k_scale, k_shift, n_ctx, tc):
    bsz, t, f2 = uv.shape
    f = f2 // 2
    d = w_down.shape[1]
    assert n_ctx == ROW_TILE and ROW_TILE % GRID_W == 0 and f % tc == 0
    last = mod_next is None
    nct = n_ctx // ROW_TILE
    nt = t // ROW_TILE
    off = nct if last else 0
    per = ROW_TILE // GRID_W
    nrow = t // GRID_W
    kern = functools.partial(_ffn_tail_kernel, n_ctx_tiles=nct, n_tiles=nt, tile_offset=off, tc=tc, last=last)

    def mod_spec(kk):
        return pl.BlockSpec((1, 1, d), lambda b, i: _mod_row(b, i + off, nct, bsz, kk))

    row_spec = pl.BlockSpec((1, ROW_TILE, d), lambda b, i: (b, i, 0))
    in_specs = [pl.BlockSpec((1, GRID_W, f), lambda b, i: (b, jnp.maximum((i + off) * per - 1, 0), 0)),
                pl.BlockSpec((1, ROW_TILE, f), lambda b, i: (b, i + off, 0)),
                pl.BlockSpec((1, GRID_W, f), lambda b, i: (b, jnp.minimum((i + off) * per + per, nrow - 1), 0)),
                pl.BlockSpec((1, ROW_TILE, f), lambda b, i: (b, i + off, 1)),
                pl.BlockSpec((9, f), lambda b, i: (0, 0)),
                pl.BlockSpec((1, f), lambda b, i: (0, 0)),
                pl.BlockSpec((2, ROW_TILE, 2 * ROW_TILE), lambda b, i: (0, 0, 0)),
                pl.BlockSpec((f, d), lambda b, i: (0, 0), pipeline_mode=pl.Buffered(1)),
                pl.BlockSpec((1, ROW_TILE, d), lambda b, i: (b, i + off, 0)),
                mod_spec(k_gate),
                pl.BlockSpec((1, d), lambda b, i: (0, 0))]
    args = [uv, uv, uv, uv, conv_w.reshape(9, f), conv_b.reshape(1, f), jnp.asarray(_shift_constants(n_ctx), BF16),
            w_down, x, mod, norm_w.reshape(1, d)]
    if last:
        return pl.pallas_call(
            kern, grid=(bsz, nt - off), in_specs=in_specs, out_specs=row_spec,
            out_shape=jax.ShapeDtypeStruct((bsz, t - n_ctx, d), F32),
            compiler_params=_cparams(2), name="ffn_tail_final",
        )(*args)
    return pl.pallas_call(
        kern, grid=(bsz, nt), in_specs=in_specs + [mod_spec(k_scale), mod_spec(k_shift)],
        out_specs=[row_spec, row_spec],
        out_shape=[jax.ShapeDtypeStruct(x.shape, F32), jax.ShapeDtypeStruct(x.shape, BF16)],
        input_output_aliases={8: 0},
        compiler_params=_cparams(2), name="ffn_tail",
    )(*args, mod_next, mod_next)


def _scan_both(group_fn, finish_fn, n_ctx_chunks, n_chunks, group, waves):
    assert n_chunks % group == 0 and group % waves == 0
    per_wave = group // waves
    init = jnp.zeros((HEAD_DIM, HEAD_DIM), F32)

    def body(i, carry):
        st = {True: carry[0], False: carry[1]}
        gens = []
        for w in range(waves):
            items = []
            for j in range(per_wave):
                p = i * group + w * per_wave + j
                items.append((p, True))
                items.append((jnp.where(p < n_ctx_chunks, n_ctx_chunks - 1 - p, n_chunks + n_ctx_chunks - 1 - p), False))
            gens.append(group_fn(items, st))
        live = set(range(waves))
        step = 0
        while live:
            for w in sorted(live):
                if step >= w:
                    try:
                        next(gens[w])
                    except StopIteration:
                        live.discard(w)
            step += 1
        return st[True], st[False]

    lax.fori_loop(0, n_chunks // group, body, (init, init))

    def finish(i, carry):
        finish_fn([_rows(i * group + j) for j in range(group)])
        return carry

    lax.fori_loop(0, n_chunks // group, finish, 0)


def _rows(c):
    return pl.ds(pl.multiple_of(c * CHUNK, CHUNK), CHUNK)


def _finish_heads(rows_list, o_ref, of_scr, zg_ref, nw_ref, centre):
    outs = [of_scr[r, :] + o_ref[0, r, :].astype(F32) for r in rows_list]
    if centre:
        mus = [_row_mean(o) for o in outs]
        outs = [o - mu for o, mu in zip(outs, mus)]
    mss = [_row_mean(o * o) for o in outs]
    for r, o, ms in zip(rows_list, outs, mss):
        o = o * lax.rsqrt(ms + NORM_EPS) * nw_ref[...]
        o_ref[0, r, :] = (o * _silu(zg_ref[0, r, :].astype(F32))).astype(o_ref.dtype)


def _level_constants():
    idx = np.arange(CHUNK)
    x = idx[:, None] ^ idx[None, :]
    lvl = np.where(x > 0, np.floor(np.log2(np.maximum(x, 1))), 6).astype(np.int32)
    lvl_f = np.where(idx[:, None] >= idx[None, :], lvl, 7).astype(np.int32)
    lvl_b = lvl_f.T.copy()
    def stack(fwd):
        tri = ((idx[None, :] <= idx[:, None]) if fwd else (idx[None, :] >= idx[:, None])).astype(np.float32)
        mats = [tri]
        for m in (32, 16, 8, 4, 2, 1):
            ref = (idx // (2 * m)) * (2 * m) + (m - 1 if fwd else m)
            mats.append(tri - tri[ref])
        return np.concatenate(mats, axis=0)
    return lvl_f, lvl_b, stack(True), stack(False)


class _Item:
    pass


def _hgrn_kernel(zq_ref, zi_ref, zff_ref, zfb_ref, zg_ref, lbf_ref, lbb_ref, nw_ref,
                 self_ref, selb_ref, lvlf_ref, lvlb_ref, merged_ref, o_ref, of_scr, *, n_ctx_chunks, n_chunks):
    del merged_ref
    def group_fn(items, st):
        its = []
        for c, fwd in items:
            it = _Item()
            it.fwd, it.rows = fwd, _rows(c)
            zf_ref, lb_ref = (zff_ref, lbf_ref) if fwd else (zfb_ref, lbb_ref)
            it.q = _silu(zq_ref[0, it.rows, :].astype(F32)) * SCALE
            it.v = zi_ref[0, it.rows, :]
            zf = zf_ref[0, it.rows, :].astype(F32)
            lb = lb_ref[...]
            logf = _log_sigmoid(zf) + jnp.log1p(lb * jnp.exp(-jnp.maximum(zf, -GATE_CLAMP)))
            it.k = (1.0 - lb) * jax.nn.sigmoid(-zf)
            it.qb, it.kb = _bf(it.q), _bf(it.k)
            it.cat = jnp.concatenate(_split2(logf * LOG2_E), axis=1)
            its.append(it)
        yield
        for it in its:
            cs = _dot((self_ref if it.fwd else selb_ref)[...], it.cat)
            it.cs = cs[:, :LANES] + cs[:, LANES:]
            it.b = it.cs[:CHUNK]
            it.lvl = (lvlf_ref if it.fwd else lvlb_ref)[...]
        yield
        for it in its:
            it.scores = jnp.where(it.lvl == 6, _dot_nt(it.qb, it.kb), 0.0)
        for n in range(6):
            yield
            for it in its:
                wgt = _bf(jnp.exp2(-jnp.abs(it.cs[(n + 1) * CHUNK:(n + 2) * CHUNK])))
                it.scores = jnp.where(it.lvl == 5 - n, _dot_nt(it.qb * wgt, it.kb * wgt), it.scores)
        yield
        for it in its:
            b_tot = it.b[CHUNK - 1:CHUNK] if it.fwd else it.b[0:1]
            it.intra = _dot(_bf(it.scores), it.v)
            it.upd = _dot_tn(it.v, _bf(it.k * jnp.exp2(b_tot - it.b)))
            it.qd = _bf(it.q * jnp.exp2(it.b))
            it.dec = jnp.exp2(b_tot)
        yield
        for it in its:
            out = it.intra + _dot_nt(it.qd, _bf(st[it.fwd]))
            st[it.fwd] = st[it.fwd] * it.dec + it.upd
            if it.fwd:
                of_scr[it.rows, :] = out
            else:
                o_ref[0, it.rows, :] = out.astype(o_ref.dtype)

    def finish_fn(rows_list):
        _finish_heads(rows_list, o_ref, of_scr, zg_ref, nw_ref, centre=False)

    _scan_both(group_fn, finish_fn, n_ctx_chunks, n_chunks, HGRN_GROUP, HGRN_WAVES)


def _zspec(t, base):
    return pl.BlockSpec((1, t, LANES), lambda b, h: (b, 0, base + h))


def _const_spec(shape):
    return pl.BlockSpec(shape, lambda b, h: (0,) * len(shape))


def _hgrn(z, merged, lb_f, lb_b, norm_w, n_ctx):
    bsz, t, _ = z.shape
    lvl_f, lvl_b, sel_f, sel_b = _level_constants()
    kern = functools.partial(_hgrn_kernel, n_ctx_chunks=n_ctx // CHUNK, n_chunks=t // CHUNK)
    head_vec = pl.BlockSpec((1, LANES), lambda b, h: (0, h))
    return pl.pallas_call(
        kern,
        grid=(bsz, A_HEADS),
        in_specs=[_zspec(t, ZB_AQ), _zspec(t, ZB_AI), _zspec(t, ZB_AFF), _zspec(t, ZB_AFB), _zspec(t, ZB_AG),
                  head_vec, head_vec, _const_spec((1, LANES)),
                  _const_spec(sel_f.shape), _const_spec(sel_b.shape),
                  _const_spec(lvl_f.shape), _const_spec(lvl_b.shape),
                  pl.BlockSpec(memory_space=pl.ANY)],
        out_specs=pl.BlockSpec((1, t, LANES), lambda b, h: (b, 0, h)),
        out_shape=jax.ShapeDtypeStruct(merged.shape, BF16),
        input_output_aliases={12: 0},
        scratch_shapes=[pltpu.VMEM((t, HEAD_DIM), F32)],
        compiler_params=_cparams(2),
        name="hgrn",
    )(z, z, z, z, z, lb_f.reshape(1, A_W), lb_b.reshape(1, A_W), norm_w.reshape(1, LANES),
      jnp.asarray(sel_f, BF16), jnp.asarray(sel_b, BF16), jnp.asarray(lvl_f), jnp.asarray(lvl_b), merged)


def _gdn_kernel(zq_ref, zk_ref, zv_ref, zg_ref, wq_ref, wk_ref, wv_ref, gc_ref, gr_ref, par_ref, nw_ref,
                lvlf_ref, lvlb_ref, merged_ref, o_ref, q_scr, k_scr, v_scr, of_scr, *, n_ctx_chunks, n_chunks):
    del merged_ref
    t_total = n_chunks * CHUNK
    ri = lax.broadcasted_iota(jnp.int32, (CHUNK, CHUNK), 0)
    ci = lax.broadcasted_iota(jnp.int32, (CHUNK, CHUNK), 1)
    row_id = lax.broadcasted_iota(jnp.int32, (CHUNK, 1), 0)
    eye = jnp.where(ri == ci, 1.0, 0.0)

    def conv(z_ref, w_ref, c):
        has_prev = jnp.logical_and(c != 0, c != n_ctx_chunks)
        has_next = jnp.logical_and(c != n_ctx_chunks - 1, c != n_chunks - 1)
        p0 = pl.multiple_of(jnp.maximum(c * CHUNK - 16, 0), 16)
        n0 = pl.multiple_of(jnp.minimum(c * CHUNK + CHUNK, t_total - 16), 16)
        x = z_ref[0, _rows(c), :].astype(F32)
        xp = z_ref[0, pl.ds(p0, 16), :].astype(F32)[15:16] * jnp.where(has_prev, 1.0, 0.0)
        xn = z_ref[0, pl.ds(n0, 16), :].astype(F32)[0:1] * jnp.where(has_next, 1.0, 0.0)
        xm1 = jnp.where(row_id == 0, xp, pltpu.roll(x, 1, 0))
        xp1 = jnp.where(row_id == CHUNK - 1, xn, pltpu.roll(x, CHUNK - 1, 0))
        w = w_ref[...]
        return _silu(xm1 * w[0:1] + x * w[1:2] + xp1 * w[2:3])

    def prep(i, carry):
        cs = [i * PREP_GROUP + j for j in range(PREP_GROUP)]
        qs = [conv(zq_ref, wq_ref, c) for c in cs]
        ks = [conv(zk_ref, wk_ref, c) for c in cs]
        q_ss = [_row_mean(a * a) * HEAD_DIM for a in qs]
        k_ss = [_row_mean(a * a) * HEAD_DIM for a in ks]
        for c, q, k, qn, kn in zip(cs, qs, ks, q_ss, k_ss):
            q_scr[_rows(c), :] = (q * (lax.rsqrt(qn + NORM_EPS) * SCALE)).astype(q_scr.dtype)
            k_scr[_rows(c), :] = (k * lax.rsqrt(kn + NORM_EPS)).astype(k_scr.dtype)
            v_scr[_rows(c), :] = conv(zv_ref, wv_ref, c).astype(v_scr.dtype)
        return carry

    lax.fori_loop(0, n_chunks // PREP_GROUP, prep, 0)

    def group_fn(items, st):
        par = par_ref[0]
        its = []
        for c, fwd in items:
            it = _Item()
            it.fwd, it.rows = fwd, _rows(c)
            ia, ib, pr = (0, 2, 0) if fwd else (1, 3, 2)
            neg_a = -jnp.exp(par[pr:pr + 1, 0:1])
            dt = par[pr + 1:pr + 2, 0:1]
            it.q = q_scr[it.rows, :]
            it.k = k_scr[it.rows, :]
            it.v = v_scr[it.rows, :].astype(F32)
            g_col = neg_a * _softplus(gc_ref[0, 0, it.rows, ia:ia + 1] + dt)
            g_row = neg_a * _softplus(gr_ref[0, 0, ia, pl.ds(c, 1), :] + dt)
            it.beta = jax.nn.sigmoid(gc_ref[0, 0, it.rows, ib:ib + 1])
            incl = (ci <= ri) if fwd else (ci >= ri)
            incl_t = (ri <= ci) if fwd else (ri >= ci)
            it.bc = jnp.sum(jnp.where(incl, g_row, 0.0), axis=1, keepdims=True)
            bc_row = jnp.sum(jnp.where(incl_t, g_col, 0.0), axis=0, keepdims=True)
            it.lmat = jnp.where(incl, jnp.exp(jnp.minimum(it.bc - bc_row, 0.0)), 0.0)
            it.lvl = (lvlf_ref if fwd else lvlb_ref)[...]
            its.append(it)
        yield
        for it in its:
            it.amat = it.beta * _dot_nt(it.k, it.k) * it.lmat
            it.qk = _bf(_dot_nt(it.q, it.k) * it.lmat)
        yield
        for it in its:
            it.x = eye - jnp.where(it.lvl == 0, it.amat, 0.0)
        for lev in range(1, 6):
            for it in its:
                it.xb = _bf(it.x)
                it.y = _dot(_bf(jnp.where(it.lvl == lev, it.amat, 0.0)), it.xb)
            yield
            for it in its:
                it.x = it.x - _dot(it.xb, _bf(it.y))
            yield
        for it in its:
            e_col = jnp.exp(it.bc)
            kf = it.k.astype(F32)
            sol = _dot(_bf(it.x), _bf(jnp.concatenate([it.v * it.beta, kf * (it.beta * e_col)], axis=1)))
            it.u = _bf(sol[:, :HEAD_DIM])
            it.w = _bf(sol[:, HEAD_DIM:])
            bc_tot = it.bc[CHUNK - 1:CHUNK] if it.fwd else it.bc[0:1]
            it.qd = it.q.astype(F32) * e_col
            it.kd = _bf(kf * jnp.exp(bc_tot - it.bc))
            it.last = jnp.exp(bc_tot)
        yield
        for it in its:
            it.kw = _bf(_dot_tn(it.kd, it.w))
            it.ku = _dot_tn(it.kd, it.u)
            it.qs = _bf(it.qd - _dot(it.qk, it.w))
            it.o0 = _dot(it.qk, it.u)
        yield
        for it in its:
            s = st[it.fwd]
            sb = _bf(s)
            out = it.o0 + _dot(it.qs, sb)
            st[it.fwd] = it.last * s + (it.ku - _dot(it.kw, sb))
            if it.fwd:
                of_scr[it.rows, :] = out
            else:
                o_ref[0, it.rows, :] = out.astype(o_ref.dtype)

    def finish_fn(rows_list):
        _finish_heads(rows_list, o_ref, of_scr, zg_ref, nw_ref, centre=False)

    _scan_both(group_fn, finish_fn, n_ctx_chunks, n_chunks, GDN_GROUP, GDN_WAVES)


def _gdn(z, zgate, merged, conv_w, a_log_f, a_log_b, dt_f, dt_b, norm_w, n_ctx):
    bsz, t, _ = z.shape
    n = t // CHUNK
    g = zgate[:, :, :N_GATES].reshape(bsz, t, 4, B_HEADS)
    gate_cols = g.transpose(0, 3, 1, 2)
    gate_rows = g.reshape(bsz, n, CHUNK, 4, B_HEADS).transpose(0, 4, 3, 1, 2)
    par = jnp.stack([a_log_f, dt_f, a_log_b, dt_b] + [jnp.zeros_like(dt_f)] * 4, axis=1)
    par = jnp.broadcast_to(par[:, :, None], (B_HEADS, 8, LANES)).astype(F32)
    lvl_f, lvl_b, _, _ = _level_constants()
    kern = functools.partial(_gdn_kernel, n_ctx_chunks=n_ctx // CHUNK, n_chunks=n)

    def wspec(base):
        return pl.BlockSpec((3, LANES), lambda b, h: (0, base + h))

    return pl.pallas_call(
        kern,
        grid=(bsz, B_HEADS),
        in_specs=[_zspec(t, ZB_BQ), _zspec(t, ZB_BK), _zspec(t, ZB_BV), _zspec(t, ZB_BG),
                  wspec(0), wspec(B_HEADS), wspec(2 * B_HEADS),
                  pl.BlockSpec((1, 1, t, 4), lambda b, h: (b, h, 0, 0)),
                  pl.BlockSpec((1, 1, 4, n, CHUNK), lambda b, h: (b, h, 0, 0, 0)),
                  pl.BlockSpec((1, 8, LANES), lambda b, h: (h, 0, 0)),
                  _const_spec((1, LANES)), _const_spec(lvl_f.shape), _const_spec(lvl_b.shape),
                  pl.BlockSpec(memory_space=pl.ANY)],
        out_specs=pl.BlockSpec((1, t, LANES), lambda b, h: (b, 0, A_HEADS + h)),
        out_shape=jax.ShapeDtypeStruct(merged.shape, BF16),
        input_output_aliases={13: 0},
        scratch_shapes=[pltpu.VMEM((t, HEAD_DIM), BF16)] * 3 + [pltpu.VMEM((t, HEAD_DIM), F32)],
        compiler_params=_cparams(2),
        name="gdn",
    )(z, z, z, z, conv_w, conv_w, conv_w, gate_cols, gate_rows, par, norm_w.reshape(1, LANES),
      jnp.asarray(lvl_f), jnp.asarray(lvl_b), merged)


def _ret_kernel(zq_ref, zk_ref, zv_ref, zg_ref, base_ref, loc_ref, dec_ref, dmat_ref, cdec_ref, nw_ref,
                merged_ref, o_ref, of_scr, *, n_ctx_chunks, n_chunks):
    del merged_ref
    def group_fn(items, st):
        its = []
        for c, fwd in items:
            it = _Item()
            it.fwd, it.rows, it.d = fwd, _rows(c), 0 if fwd else 1
            d = it.d
            cb = base_ref[2 * d, pl.ds(c, 1), :]
            sb = base_ref[2 * d + 1, pl.ds(c, 1), :]
            cl, sl, cls, sls = (loc_ref[4 * d + n] for n in range(4))
            cos = cb * cl - sb * sl
            sin = sb * cls + cb * sls

            def rot(a):
                return a * cos + pltpu.roll(a, HEAD_DIM // 2, 1) * sin

            q = rot(zq_ref[0, it.rows, :].astype(F32) * SCALE)
            k = rot(zk_ref[0, it.rows, :].astype(F32))
            it.q, it.k = _bf(q), _bf(k)
            it.qd = _bf(q * dec_ref[0, 2 * d])
            it.kd = _bf(k * dec_ref[0, 2 * d + 1])
            it.v = zv_ref[0, it.rows, :]
            its.append(it)
        yield
        for it in its:
            it.scores = _bf(_dot_nt(it.q, it.k) * dmat_ref[0, it.d])
        yield
        for it in its:
            it.intra = _dot(it.scores, it.v)
            it.upd = _dot_tn(it.v, it.kd)
        yield
        for it in its:
            out = it.intra + _dot_nt(it.qd, _bf(st[it.fwd]))
            st[it.fwd] = st[it.fwd] * cdec_ref[0, it.d:it.d + 1, :] + it.upd
            if it.fwd:
                of_scr[it.rows, :] = out
            else:
                o_ref[0, it.rows, :] = out.astype(o_ref.dtype)

    def finish_fn(rows_list):
        _finish_heads(rows_list, o_ref, of_scr, zg_ref, nw_ref, centre=True)

    _scan_both(group_fn, finish_fn, n_ctx_chunks, n_chunks, RET_GROUP, RET_WAVES)


def _retention_tables(n_ctx, t):
    half = HEAD_DIM // 2
    n = t // CHUNK
    nc = n_ctx // CHUNK
    inv = ROPE_BASE ** (-jnp.arange(half, dtype=F32) / half)
    inv = jnp.concatenate([inv, inv])
    sgn = jnp.concatenate([-jnp.ones((half,), F32), jnp.ones((half,), F32)])
    c_idx = np.arange(n)
    base_f = (c_idx * CHUNK).astype(np.float32)
    base_b = np.where(c_idx < nc, n_ctx - CHUNK * (c_idx + 1), t - CHUNK * (c_idx - nc + 1)).astype(np.float32)
    loc_f = np.arange(CHUNK, dtype=np.float32)
    loc_b = loc_f[::-1].copy()

    def ang(p):
        return jnp.asarray(p)[:, None] * inv[None, :]

    base = jnp.stack([jnp.cos(ang(base_f)), jnp.sin(ang(base_f)), jnp.cos(ang(base_b)), jnp.sin(ang(base_b))])
    loc = jnp.stack([jnp.cos(ang(loc_f)), jnp.sin(ang(loc_f)), jnp.cos(ang(loc_f)) * sgn, jnp.sin(ang(loc_f)) * sgn,
                     jnp.cos(ang(loc_b)), jnp.sin(ang(loc_b)), jnp.cos(ang(loc_b)) * sgn, jnp.sin(ang(loc_b)) * sgn])
    lg_f = jnp.log1p(-jnp.exp2(-5.0 - jnp.arange(C_HEADS, dtype=F32)))
    lg_b = lg_f[::-1]
    idx = jnp.arange(CHUNK, dtype=F32)
    rel = idx[:, None] - idx[None, :]

    def masked_exp(a, mask):
        return jnp.where(mask, jnp.exp(jnp.where(mask, a, 0.0)), 0.0)

    dmat_f = masked_exp(lg_f[:, None, None] * rel, (rel >= 0)[None])
    dmat_b = masked_exp(-lg_b[:, None, None] * rel, (rel <= 0)[None])
    dq_f = jnp.exp(lg_f[:, None] * (idx + 1.0))
    dk_f = jnp.exp(lg_f[:, None] * (CHUNK - 1.0 - idx))
    dq_b = jnp.exp(lg_b[:, None] * (CHUNK - idx))
    dk_b = jnp.exp(lg_b[:, None] * idx)
    dec = jnp.stack([dq_f, dk_f, dq_b, dk_b], axis=1)
    dec = jnp.broadcast_to(dec[..., None], (C_HEADS, 4, CHUNK, LANES))
    cdec = jnp.stack([jnp.exp(lg_f * CHUNK), jnp.exp(lg_b * CHUNK)] + [jnp.zeros((C_HEADS,), F32)] * 6, axis=1)
    cdec = jnp.broadcast_to(cdec[..., None], (C_HEADS, 8, LANES))
    return base, loc, dec, jnp.stack([dmat_f, dmat_b], axis=1), cdec


def _retention(z, merged, norm_w, n_ctx):
    bsz, t, _ = z.shape
    n = t // CHUNK
    base, loc, dec, dmat, cdec = _retention_tables(n_ctx, t)
    kern = functools.partial(_ret_kernel, n_ctx_chunks=n_ctx // CHUNK, n_chunks=n)
    return pl.pallas_call(
        kern,
        grid=(bsz, C_HEADS),
        in_specs=[_zspec(t, ZB_CQ), _zspec(t, ZB_CK), _zspec(t, ZB_CV), _zspec(t, ZB_CG),
                  _const_spec(base.shape), _const_spec(loc.shape),
                  pl.BlockSpec((1, 4, CHUNK, LANES), lambda b, h: (h, 0, 0, 0)),
                  pl.BlockSpec((1, 2, CHUNK, CHUNK), lambda b, h: (h, 0, 0, 0)),
                  pl.BlockSpec((1, 8, LANES), lambda b, h: (h, 0, 0)),
                  pl.BlockSpec((1, LANES), lambda b, h: (0, h)),
                  pl.BlockSpec(memory_space=pl.ANY)],
        out_specs=pl.BlockSpec((1, t, LANES), lambda b, h: (b, 0, A_HEADS + B_HEADS + h)),
        out_shape=jax.ShapeDtypeStruct(merged.shape, BF16),
        input_output_aliases={10: 0},
        scratch_shapes=[pltpu.VMEM((t, HEAD_DIM), F32)],
        compiler_params=_cparams(2),
        name="retention",
    )(z, z, z, z, base, loc, dec, dmat, cdec, norm_w.reshape(1, C_W), merged)


def _reorder_in_proj(w):
    g0 = 5 * A_W + 4 * B_W
    main = jnp.concatenate([w[:, :g0], w[:, g0 + N_GATES:]], axis=1).astype(BF16)
    gates = jnp.pad(w[:, g0:g0 + N_GATES], ((0, 0), (0, LANES - N_GATES))).astype(BF16)
    return main, gates


def _lower_bounds(logits):
    p = jax.nn.softmax(logits.astype(F32), axis=0)
    return jnp.cumsum(p, axis=0) - p[0]


def kernel(x, c, ctx, c_ctx, w_mod, b_mod, norm1_w, norm2_w, w_in, qkv_conv_w, hgrn_lb_logits_fwd, hgrn_lb_logits_bwd, hgrn_norm_w, gdn_a_log_fwd, gdn_a_log_bwd, gdn_dt_bias_fwd, gdn_dt_bias_bwd, gdn_norm_w, ret_norm_w, w_out, ffn_w_up, ffn_conv_w, ffn_conv_b, ffn_w_down, final_norm_w):
    bsz, n_lat, d = x.shape
    n_ctx = ctx.shape[1]
    depth = w_mod.shape[0]
    t = n_ctx + n_lat
    assert n_ctx % ROW_TILE == 0 and n_lat % ROW_TILE == 0 and bsz < MOD_ROWS
    lb_fwd = _lower_bounds(hgrn_lb_logits_fwd)
    lb_bwd = _lower_bounds(hgrn_lb_logits_bwd)

    rows = jnp.concatenate([c, c_ctx[None, :], jnp.zeros((MOD_ROWS - bsz - 1, d), F32)], axis=0)
    mods = _modulation(rows, w_mod, b_mod)

    tm = MM_ROWS
    m_rows = bsz * t
    f2 = ffn_w_up.shape[2]
    zw = Z_BLOCKS * LANES
    assert m_rows % tm == 0 and zw % (IN_PROJ_COL_TILES * LANES) == 0 and f2 % (FFN_UP_COL_TILES * LANES) == 0
    mod = [mods[l].reshape(MOD_ROWS * N_MOD, 1, d) for l in range(depth)]
    h = _norm_mod(ctx, x, norm1_w[0], mod[0], 1, 0)
    xs = x
    for l in range(depth):
        w_main, w_gate = _reorder_in_proj(w_in[l])
        h = h.reshape(m_rows, d)
        z = _matmul(h, w_main, tm, zw // IN_PROJ_COL_TILES, BF16, "in_proj").reshape(bsz, t, zw)
        zgate = _matmul(h, w_gate, tm, LANES, F32, "in_proj_gates").reshape(bsz, t, LANES)
        merged = jnp.zeros((bsz, t, N_HEADS * HEAD_DIM), BF16)
        merged = _hgrn(z, merged, lb_fwd[l], lb_bwd[l], hgrn_norm_w[l], n_ctx)
        merged = _gdn(z, zgate, merged, qkv_conv_w[l], gdn_a_log_fwd[l], gdn_a_log_bwd[l],
                      gdn_dt_bias_fwd[l], gdn_dt_bias_bwd[l], gdn_norm_w[l], n_ctx)
        merged = _retention(z, merged, ret_norm_w[l], n_ctx)
        xs, h = _out_proj(merged, w_out[l].astype(BF16), ctx if l == 0 else None, xs, mod[l], 2,
                          norm2_w[l], 4, 3, n_ctx)

        uv = _matmul(h.reshape(m_rows, d), ffn_w_up[l].astype(BF16), tm, f2 // FFN_UP_COL_TILES, BF16,
                     "ffn_up").reshape(bsz, t, f2)
        w_down = ffn_w_down[l].astype(BF16)
        if l + 1 < depth:
            xs, h = _ffn_tail(uv, ffn_conv_w[l], ffn_conv_b[l], w_down, xs, mod[l], 5,
                              norm1_w[l + 1], mod[l + 1], 1, 0, n_ctx, FFN_SLAB)
    return _ffn_tail(uv, ffn_conv_w[-1], ffn_conv_b[-1], w_down, xs, mod[-1], 5,
                     final_norm_w, None, None, None, n_ctx, FFN_SLAB)
```

```python
import functools
import math

import numpy as np
import jax
import jax.numpy as jnp
from jax import lax
from jax.experimental import pallas as pl
from jax.experimental.pallas import tpu as pltpu

F32 = jnp.float32
BF16 = jnp.bfloat16

HEAD_DIM = 128
N_HEADS = 16
A_HEADS = 5
C_HEADS = 5
B_HEADS = 6
A_W = A_HEADS * HEAD_DIM
B_W = B_HEADS * HEAD_DIM
C_W = C_HEADS * HEAD_DIM
N_GATES = 4 * B_HEADS
CHUNK = 64
GRID_W = 64
ROPE_BASE = 10000.0
NORM_EPS = 1e-6
N_MOD = 6
GATE_CLAMP = 60.0
SCALE = HEAD_DIM ** -0.5
LOG2_E = 1.4426950408889634
LANES = 128
MOD_ROWS = 8
ROW_TILE = 256
VMEM_LIMIT = 56 * 1024 * 1024
HGRN_GROUP, HGRN_WAVES = 11, 1
GDN_GROUP, GDN_WAVES = 44, 44
RET_GROUP, RET_WAVES = 44, 44
PREP_GROUP = 4
CONV_AHEAD = 1
FFN_SLAB = 512
MM_ROWS = 512
IN_PROJ_COL_TILES = 3
FFN_UP_COL_TILES = 4

ZB_AQ, ZB_AI, ZB_AFF, ZB_AFB, ZB_AG = 0, 5, 10, 15, 20
ZB_BQ, ZB_BK, ZB_BV, ZB_BG = 25, 31, 37, 43
ZB_CQ, ZB_CK, ZB_CV, ZB_CG = 49, 54, 59, 64
Z_BLOCKS = 69


def _cparams(n_axes):
    return pltpu.CompilerParams(dimension_semantics=("arbitrary",) * n_axes,
                                vmem_limit_bytes=VMEM_LIMIT)


def _dot(a, b):
    return jnp.dot(a, b, preferred_element_type=F32)


def _dot_nt(a, b):
    return lax.dot_general(a, b, (((1,), (1,)), ((), ())), preferred_element_type=F32)


def _dot_tn(a, b):
    return lax.dot_general(a, b, (((0,), (0,)), ((), ())), preferred_element_type=F32)


def _bf(a):
    return a.astype(BF16)


def _split2(a):
    hi = a.astype(BF16)
    lo = (a - hi.astype(F32)).astype(BF16)
    return hi, lo


def _dot3(a, b):
    ah, al = _split2(a)
    bh, bl = _split2(b)
    return _dot(ah, bh) + (_dot(al, bh) + _dot(ah, bl))


def _row_mean(a):
    hi, lo = _split2(a)
    ones = jnp.ones((a.shape[1], a.shape[1]), BF16)
    return (_dot(hi, ones) + _dot(lo, ones)) * (1.0 / a.shape[1])


def _silu(a):
    return a * jax.nn.sigmoid(a)


def _softplus(a):
    return jnp.maximum(a, 0.0) + jnp.log1p(jnp.exp(-jnp.abs(a)))


def _log_sigmoid(a):
    return jnp.minimum(a, 0.0) - jnp.log1p(jnp.exp(-jnp.abs(a)))


def _mod_kernel(a_ref, w_ref, b_ref, o_ref):
    a = _silu(a_ref[...])
    o_ref[0] = _dot3(a, w_ref[0]) + b_ref[0]


def _modulation(rows, w_mod, b_mod):
    depth, d, n = w_mod.shape
    tn = 1024
    return pl.pallas_call(
        _mod_kernel,
        grid=(depth, n // tn),
        in_specs=[pl.BlockSpec((MOD_ROWS, d), lambda l, j: (0, 0)),
                  pl.BlockSpec((1, d, tn), lambda l, j: (l, 0, j)),
                  pl.BlockSpec((1, 1, tn), lambda l, j: (l, 0, j))],
        out_specs=pl.BlockSpec((1, MOD_ROWS, tn), lambda l, j: (l, 0, j)),
        out_shape=jax.ShapeDtypeStruct((depth, MOD_ROWS, n), F32),
        compiler_params=_cparams(2),
        name="modulation",
    )(rows, w_mod, b_mod.reshape(depth, 1, n))


def _mod_row(b, i, n_ctx_tiles, n_batch, k):
    return (jnp.where(i < n_ctx_tiles, n_batch, b) * N_MOD + k, 0, 0)


def _split_stream_specs(n_ctx_tiles, d):
    return [pl.BlockSpec((1, ROW_TILE, d), lambda b, i: (b, jnp.minimum(i, n_ctx_tiles - 1), 0)),
            pl.BlockSpec((1, ROW_TILE, d), lambda b, i: (b, jnp.maximum(i - n_ctx_tiles, 0), 0))]


def _for_stream_tile(n_ctx_tiles, ctx_ref, lat_ref, body):
    i = pl.program_id(1)

    @pl.when(i < n_ctx_tiles)
    def _():
        body(ctx_ref[0])

    @pl.when(i >= n_ctx_tiles)
    def _():
        body(lat_ref[0])


def _norm_mod_kernel(ctx_ref, lat_ref, w_ref, sc_ref, sh_ref, o_ref, *, n_ctx_tiles):
    def body(x):
        o_ref[0] = _rms_mod(x, w_ref[...], sc_ref[0], sh_ref[0]).astype(o_ref.dtype)

    _for_stream_tile(n_ctx_tiles, ctx_ref, lat_ref, body)


def _norm_mod(ctx, x, w, mod, k_scale, k_shift):
    bsz, n_ctx, d = ctx.shape
    t = n_ctx + x.shape[1]
    nct = n_ctx // ROW_TILE
    return pl.pallas_call(
        functools.partial(_norm_mod_kernel, n_ctx_tiles=nct),
        grid=(bsz, t // ROW_TILE),
        in_specs=_split_stream_specs(nct, d) + [
            pl.BlockSpec((1, d), lambda b, i: (0, 0)),
            pl.BlockSpec((1, 1, d), lambda b, i: _mod_row(b, i, nct, bsz, k_scale)),
            pl.BlockSpec((1, 1, d), lambda b, i: _mod_row(b, i, nct, bsz, k_shift))],
        out_specs=pl.BlockSpec((1, ROW_TILE, d), lambda b, i: (b, i, 0)),
        out_shape=jax.ShapeDtypeStruct((bsz, t, d), BF16),
        compiler_params=_cparams(2),
        name="norm_mod",
    )(ctx, x, w.reshape(1, d), mod, mod)


def _mm_kernel(a_ref, w_ref, o_ref):
    o_ref[...] = _dot(a_ref[...], w_ref[...]).astype(o_ref.dtype)


def _matmul(a, w, tm, tn, out_dtype, name):
    m, k = a.shape
    n = w.shape[1]
    return pl.pallas_call(
        _mm_kernel,
        grid=(n // tn, m // tm),
        in_specs=[pl.BlockSpec((tm, k), lambda j, i: (i, 0)),
                  pl.BlockSpec((k, tn), lambda j, i: (0, j))],
        out_specs=pl.BlockSpec((tm, tn), lambda j, i: (i, j)),
        out_shape=jax.ShapeDtypeStruct((m, n), out_dtype),
        compiler_params=_cparams(2),
        name=name,
    )(a, w)


def _rms_mod(x, w, scale, shift):
    return (x * lax.rsqrt(jnp.mean(x * x, axis=-1, keepdims=True) + NORM_EPS) * w) * (1.0 + scale) + shift


def _out_proj_kernel(a_ref, w_ref, g_ref, nw_ref, sc_ref, sh_ref, *rest, n_ctx_tiles, split):
    def body(x):
        x = x + g_ref[0] * _dot(a_ref[0], w_ref[...])
        o_ref[0] = x
        h_ref[0] = _rms_mod(x, nw_ref[...], sc_ref[0], sh_ref[0]).astype(h_ref.dtype)

    if split:
        ctx_ref, lat_ref, o_ref, h_ref = rest
        _for_stream_tile(n_ctx_tiles, ctx_ref, lat_ref, body)
    else:
        x_ref, o_ref, h_ref = rest
        body(x_ref[0])


def _out_proj(a, w, ctx, x, mod, k_gate, norm_w, k_scale, k_shift, n_ctx):
    bsz, t, k = a.shape
    d = w.shape[1]
    nct = n_ctx // ROW_TILE
    split = ctx is not None

    def mod_spec(kk):
        return pl.BlockSpec((1, 1, d), lambda b, i: _mod_row(b, i, nct, bsz, kk))

    row_spec = pl.BlockSpec((1, ROW_TILE, d), lambda b, i: (b, i, 0))
    in_specs = [pl.BlockSpec((1, ROW_TILE, k), lambda b, i: (b, i, 0)),
                pl.BlockSpec((k, d), lambda b, i: (0, 0)),
                mod_spec(k_gate), pl.BlockSpec((1, d), lambda b, i: (0, 0)), mod_spec(k_scale), mod_spec(k_shift)]
    in_specs += _split_stream_specs(nct, d) if split else [row_spec]
    return pl.pallas_call(
        functools.partial(_out_proj_kernel, n_ctx_tiles=nct, split=split),
        grid=(bsz, t // ROW_TILE),
        in_specs=in_specs,
        out_specs=[row_spec, row_spec],
        out_shape=[jax.ShapeDtypeStruct((bsz, t, d), F32), jax.ShapeDtypeStruct((bsz, t, d), BF16)],
        input_output_aliases={} if split else {6: 0},
        compiler_params=_cparams(2),
        name="out_proj",
    )(a, w, mod, norm_w.reshape(1, d), mod, mod, *((ctx, x) if split else (x,)))


def _ffn_tail_kernel(up_ref, um_ref, un_ref, v_ref, cw_ref, cb_ref, shift_ref, wd_ref, x_ref, g_ref, nw_ref, *rest,
                     n_ctx_tiles, n_tiles, tile_offset, tc, last):
    if last:
        (o_ref,) = rest
    else:
        sc_ref, sh_ref, o_ref, h_ref = rest
    i = pl.program_id(1) + tile_offset
    is_ctx = i < n_ctx_tiles
    has_up = jnp.logical_and(jnp.logical_not(is_ctx), i > n_ctx_tiles)
    has_dn = jnp.logical_and(jnp.logical_not(is_ctx), i < n_tiles - 1)
    row_on = jnp.where(is_ctx, 0.0, 1.0)
    up_on = jnp.where(has_up, 1.0, 0.0)
    dn_on = jnp.where(has_dn, 1.0, 0.0)
    shift = shift_ref[jnp.where(is_ctx, 0, 1)]
    def conv_slab(j):
        cols = slice(j * tc, (j + 1) * tc)
        main = um_ref[0, :, cols].astype(F32)
        prev = up_ref[0, :, cols].astype(F32) * up_on
        nxt = un_ref[0, :, cols].astype(F32) * dn_on
        up = jnp.concatenate([prev, main[:ROW_TILE - GRID_W]], axis=0)
        dn = jnp.concatenate([main[GRID_W:], nxt], axis=0)
        w = cw_ref[:, cols]
        wu = w[0:3] * row_on
        wc = w[3:6]
        wd = w[6:9] * row_on

        def col(kw):
            return up * wu[kw:kw + 1] + main * wc[kw:kw + 1] + dn * wd[kw:kw + 1]

        sides = _dot(shift, jnp.concatenate([_bf(col(0)), _bf(col(2))], axis=0))
        return cols, sides + col(1) + cb_ref[:, cols]

    n_slabs = um_ref.shape[2] // tc
    acc = None
    pending = [conv_slab(j) for j in range(min(CONV_AHEAD, n_slabs))]
    for j in range(n_slabs):
        cols, u = pending.pop(0)
        if j + CONV_AHEAD < n_slabs:
            pending.append(conv_slab(j + CONV_AHEAD))
        act = _bf(_silu(u)) * v_ref[0, :, cols]
        part = _dot(act, wd_ref[cols, :])
        acc = part if acc is None else acc + part
    x = x_ref[0] + g_ref[0] * acc
    if last:
        o_ref[0] = x * lax.rsqrt(jnp.mean(x * x, axis=-1, keepdims=True) + NORM_EPS) * nw_ref[...]
    else:
        o_ref[0] = x
        h_ref[0] = _rms_mod(x, nw_ref[...], sc_ref[0], sh_ref[0]).astype(h_ref.dtype)


def _shift_constants(n_ctx):
    r = np.arange(ROW_TILE)
    out = np.zeros((2, ROW_TILE, 2 * ROW_TILE), np.float32)
    for kind, width in enumerate((n_ctx, GRID_W)):
        has_left = r % width != 0
        has_right = r % width != width - 1
        out[kind, r[has_left], r[has_left] - 1] = 1.0
        out[kind, r[has_right], ROW_TILE + r[has_right] + 1] = 1.0
    return out


def _ffn_tail(uv, conv_w, conv_b, w_down, x, mod, k_gate, norm_w, mod_next, k_scale, k_shift, n_ctx, tc):
    bsz, t, f2 = uv.shape
    f = f2 // 2
    d = w_down.shape[1]
    assert n_ctx == ROW_TILE and ROW_TILE % GRID_W == 0 and f % tc == 0
    last = mod_next is None
    nct = n_ctx // ROW_TILE
    nt = t // ROW_TILE
    off = nct if last else 0
    per = ROW_TILE // GRID_W
    nrow = t // GRID_W
    kern = functools.partial(_ffn_tail_kernel, n_ctx_tiles=nct, n_tiles=nt, tile_offset=off, tc=tc, last=last)

    def mod_spec(kk):
        return pl.BlockSpec((1, 1, d), lambda b, i: _mod_row(b, i + off, nct, bsz, kk))

    row_spec = pl.BlockSpec((1, ROW_TILE, d), lambda b, i: (b, i, 0))
    in_specs = [pl.BlockSpec((1, GRID_W, f), lambda b, i: (b, jnp.maximum((i + off) * per - 1, 0), 0)),
                pl.BlockSpec((1, ROW_TILE, f), lambda b, i: (b, i + off, 0)),
                pl.BlockSpec((1, GRID_W, f), lambda b, i: (b, jnp.minimum((i + off) * per + per, nrow - 1), 0)),
                pl.BlockSpec((1, ROW_TILE, f), lambda b, i: (b, i + off, 1)),
                pl.BlockSpec((9, f), lambda b, i: (0, 0)),
                pl.BlockSpec((1, f), lambda b, i: (0, 0)),
                pl.BlockSpec((2, ROW_TILE, 2 * ROW_TILE), lambda b, i: (0, 0, 0)),
                pl.BlockSpec((f, d), lambda b, i: (0, 0), pipeline_mode=pl.Buffered(1)),
                pl.BlockSpec((1, ROW_TILE, d), lambda b, i: (b, i + off, 0)),
                mod_spec(k_gate),
                pl.BlockSpec((1, d), lambda b, i: (0, 0))]
    args = [uv, uv, uv, uv, conv_w.reshape(9, f), conv_b.reshape(1, f), jnp.asarray(_shift_constants(n_ctx), BF16),
            w_down, x, mod, norm_w.reshape(1, d)]
    if last:
        return pl.pallas_call(
            kern, grid=(bsz, nt - off), in_specs=in_specs, out_specs=row_spec,
            out_shape=jax.ShapeDtypeStruct((bsz, t - n_ctx, d), F32),
            compiler_params=_cparams(2), name="ffn_tail_final",
        )(*args)
    return pl.pallas_call(
        kern, grid=(bsz, nt), in_specs=in_specs + [mod_spec(k_scale), mod_spec(k_shift)],
        out_specs=[row_spec, row_spec],
        out_shape=[jax.ShapeDtypeStruct(x.shape, F32), jax.ShapeDtypeStruct(x.shape, BF16)],
        input_output_aliases={8: 0},
        compiler_params=_cparams(2), name="ffn_tail",
    )(*args, mod_next, mod_next)


def _scan_both(group_fn, finish_fn, n_ctx_chunks, n_chunks, group, waves):
    assert n_chunks % group == 0 and group % waves == 0
    per_wave = group // waves
    init = jnp.zeros((HEAD_DIM, HEAD_DIM), F32)

    def body(i, carry):
        st = {True: carry[0], False: carry[1]}
        gens = []
        for w in range(waves):
            items = []
            for j in range(per_wave):
                p = i * group + w * per_wave + j
                items.append((p, True))
                items.append((jnp.where(p < n_ctx_chunks, n_ctx_chunks - 1 - p, n_chunks + n_ctx_chunks - 1 - p), False))
            gens.append(group_fn(items, st))
        live = set(range(waves))
        step = 0
        while live:
            for w in sorted(live):
                if step >= w:
                    try:
                        next(gens[w])
                    except StopIteration:
                        live.discard(w)
            step += 1
        return st[True], st[False]

    lax.fori_loop(0, n_chunks // group, body, (init, init))

    def finish(i, carry):
        finish_fn([_rows(i * group + j) for j in range(group)])
        return carry

    lax.fori_loop(0, n_chunks // group, finish, 0)


def _rows(c):
    return pl.ds(pl.multiple_of(c * CHUNK, CHUNK), CHUNK)


def _finish_heads(rows_list, o_ref, of_scr, zg_ref, nw_ref, centre):
    outs = [of_scr[r, :] + o_ref[0, r, :].astype(F32) for r in rows_list]
    if centre:
        mus = [_row_mean(o) for o in outs]
        outs = [o - mu for o, mu in zip(outs, mus)]
    mss = [_row_mean(o * o) for o in outs]
    for r, o, ms in zip(rows_list, outs, mss):
        o = o * lax.rsqrt(ms + NORM_EPS) * nw_ref[...]
        o_ref[0, r, :] = (o * _silu(zg_ref[0, r, :].astype(F32))).astype(o_ref.dtype)


def _level_constants():
    idx = np.arange(CHUNK)
    x = idx[:, None] ^ idx[None, :]
    lvl = np.where(x > 0, np.floor(np.log2(np.maximum(x, 1))), 6).astype(np.int32)
    lvl_f = np.where(idx[:, None] >= idx[None, :], lvl, 7).astype(np.int32)
    lvl_b = lvl_f.T.copy()
    def stack(fwd):
        tri = ((idx[None, :] <= idx[:, None]) if fwd else (idx[None, :] >= idx[:, None])).astype(np.float32)
        mats = [tri]
        for m in (32, 16, 8, 4, 2, 1):
            ref = (idx // (2 * m)) * (2 * m) + (m - 1 if fwd else m)
            mats.append(tri - tri[ref])
        return np.concatenate(mats, axis=0)
    return lvl_f, lvl_b, stack(True), stack(False)


class _Item:
    pass


def _hgrn_kernel(zq_ref, zi_ref, zff_ref, zfb_ref, zg_ref, lbf_ref, lbb_ref, nw_ref,
                 self_ref, selb_ref, lvlf_ref, lvlb_ref, merged_ref, o_ref, of_scr, *, n_ctx_chunks, n_chunks):
    del merged_ref
    def group_fn(items, st):
        its = []
        for c, fwd in items:
            it = _Item()
            it.fwd, it.rows = fwd, _rows(c)
            zf_ref, lb_ref = (zff_ref, lbf_ref) if fwd else (zfb_ref, lbb_ref)
            it.q = _silu(zq_ref[0, it.rows, :].astype(F32)) * SCALE
            it.v = zi_ref[0, it.rows, :]
            zf = zf_ref[0, it.rows, :].astype(F32)
            lb = lb_ref[...]
            logf = _log_sigmoid(zf) + jnp.log1p(lb * jnp.exp(-jnp.maximum(zf, -GATE_CLAMP)))
            it.k = (1.0 - lb) * jax.nn.sigmoid(-zf)
            it.qb, it.kb = _bf(it.q), _bf(it.k)
            it.cat = jnp.concatenate(_split2(logf * LOG2_E), axis=1)
            its.append(it)
        yield
        for it in its:
            cs = _dot((self_ref if it.fwd else selb_ref)[...], it.cat)
            it.cs = cs[:, :LANES] + cs[:, LANES:]
            it.b = it.cs[:CHUNK]
            it.lvl = (lvlf_ref if it.fwd else lvlb_ref)[...]
        yield
        for it in its:
            it.scores = jnp.where(it.lvl == 6, _dot_nt(it.qb, it.kb), 0.0)
        for n in range(6):
            yield
            for it in its:
                wgt = _bf(jnp.exp2(-jnp.abs(it.cs[(n + 1) * CHUNK:(n + 2) * CHUNK])))
                it.scores = jnp.where(it.lvl == 5 - n, _dot_nt(it.qb * wgt, it.kb * wgt), it.scores)
        yield
        for it in its:
            b_tot = it.b[CHUNK - 1:CHUNK] if it.fwd else it.b[0:1]
            it.intra = _dot(_bf(it.scores), it.v)
            it.upd = _dot_tn(it.v, _bf(it.k * jnp.exp2(b_tot - it.b)))
            it.qd = _bf(it.q * jnp.exp2(it.b))
            it.dec = jnp.exp2(b_tot)
        yield
        for it in its:
            out = it.intra + _dot_nt(it.qd, _bf(st[it.fwd]))
            st[it.fwd] = st[it.fwd] * it.dec + it.upd
            if it.fwd:
                of_scr[it.rows, :] = out
            else:
                o_ref[0, it.rows, :] = out.astype(o_ref.dtype)

    def finish_fn(rows_list):
        _finish_heads(rows_list, o_ref, of_scr, zg_ref, nw_ref, centre=False)

    _scan_both(group_fn, finish_fn, n_ctx_chunks, n_chunks, HGRN_GROUP, HGRN_WAVES)


def _zspec(t, base):
    return pl.BlockSpec((1, t, LANES), lambda b, h: (b, 0, base + h))


def _const_spec(shape):
    return pl.BlockSpec(shape, lambda b, h: (0,) * len(shape))


def _hgrn(z, merged, lb_f, lb_b, norm_w, n_ctx):
    bsz, t, _ = z.shape
    lvl_f, lvl_b, sel_f, sel_b = _level_constants()
    kern = functools.partial(_hgrn_kernel, n_ctx_chunks=n_ctx // CHUNK, n_chunks=t // CHUNK)
    head_vec = pl.BlockSpec((1, LANES), lambda b, h: (0, h))
    return pl.pallas_call(
        kern,
        grid=(bsz, A_HEADS),
        in_specs=[_zspec(t, ZB_AQ), _zspec(t, ZB_AI), _zspec(t, ZB_AFF), _zspec(t, ZB_AFB), _zspec(t, ZB_AG),
                  head_vec, head_vec, _const_spec((1, LANES)),
                  _const_spec(sel_f.shape), _const_spec(sel_b.shape),
                  _const_spec(lvl_f.shape), _const_spec(lvl_b.shape),
                  pl.BlockSpec(memory_space=pl.ANY)],
        out_specs=pl.BlockSpec((1, t, LANES), lambda b, h: (b, 0, h)),
        out_shape=jax.ShapeDtypeStruct(merged.shape, BF16),
        input_output_aliases={12: 0},
        scratch_shapes=[pltpu.VMEM((t, HEAD_DIM), F32)],
        compiler_params=_cparams(2),
        name="hgrn",
    )(z, z, z, z, z, lb_f.reshape(1, A_W), lb_b.reshape(1, A_W), norm_w.reshape(1, LANES),
      jnp.asarray(sel_f, BF16), jnp.asarray(sel_b, BF16), jnp.asarray(lvl_f), jnp.asarray(lvl_b), merged)


def _gdn_kernel(zq_ref, zk_ref, zv_ref, zg_ref, wq_ref, wk_ref, wv_ref, gc_ref, gr_ref, par_ref, nw_ref,
                lvlf_ref, lvlb_ref, merged_ref, o_ref, q_scr, k_scr, v_scr, of_scr, *, n_ctx_chunks, n_chunks):
    del merged_ref
    t_total = n_chunks * CHUNK
    ri = lax.broadcasted_iota(jnp.int32, (CHUNK, CHUNK), 0)
    ci = lax.broadcasted_iota(jnp.int32, (CHUNK, CHUNK), 1)
    row_id = lax.broadcasted_iota(jnp.int32, (CHUNK, 1), 0)
    eye = jnp.where(ri == ci, 1.0, 0.0)

    def conv(z_ref, w_ref, c):
        has_prev = jnp.logical_and(c != 0, c != n_ctx_chunks)
        has_next = jnp.logical_and(c != n_ctx_chunks - 1, c != n_chunks - 1)
        p0 = pl.multiple_of(jnp.maximum(c * CHUNK - 16, 0), 16)
        n0 = pl.multiple_of(jnp.minimum(c * CHUNK + CHUNK, t_total - 16), 16)
        x = z_ref[0, _rows(c), :].astype(F32)
        xp = z_ref[0, pl.ds(p0, 16), :].astype(F32)[15:16] * jnp.where(has_prev, 1.0, 0.0)
        xn = z_ref[0, pl.ds(n0, 16), :].astype(F32)[0:1] * jnp.where(has_next, 1.0, 0.0)
        xm1 = jnp.where(row_id == 0, xp, pltpu.roll(x, 1, 0))
        xp1 = jnp.where(row_id == CHUNK - 1, xn, pltpu.roll(x, CHUNK - 1, 0))
        w = w_ref[...]
        return _silu(xm1 * w[0:1] + x * w[1:2] + xp1 * w[2:3])

    def prep(i, carry):
        cs = [i * PREP_GROUP + j for j in range(PREP_GROUP)]
        qs = [conv(zq_ref, wq_ref, c) for c in cs]
        ks = [conv(zk_ref, wk_ref, c) for c in cs]
        q_ss = [_row_mean(a * a) * HEAD_DIM for a in qs]
        k_ss = [_row_mean(a * a) * HEAD_DIM for a in ks]
        for c, q, k, qn, kn in zip(cs, qs, ks, q_ss, k_ss):
            q_scr[_rows(c), :] = (q * (lax.rsqrt(qn + NORM_EPS) * SCALE)).astype(q_scr.dtype)
            k_scr[_rows(c), :] = (k * lax.rsqrt(kn + NORM_EPS)).astype(k_scr.dtype)
            v_scr[_rows(c), :] = conv(zv_ref, wv_ref, c).astype(v_scr.dtype)
        return carry

    lax.fori_loop(0, n_chunks // PREP_GROUP, prep, 0)

    def group_fn(items, st):
        par = par_ref[0]
        its = []
        for c, fwd in items:
            it = _Item()
            it.fwd, it.rows = fwd, _rows(c)
            ia, ib, pr = (0, 2, 0) if fwd else (1, 3, 2)
            neg_a = -jnp.exp(par[pr:pr + 1, 0:1])
            dt = par[pr + 1:pr + 2, 0:1]
            it.q = q_scr[it.rows, :]
            it.k = k_scr[it.rows, :]
            it.v = v_scr[it.rows, :].astype(F32)
            g_col = neg_a * _softplus(gc_ref[0, 0, it.rows, ia:ia + 1] + dt)
            g_row = neg_a * _softplus(gr_ref[0, 0, ia, pl.ds(c, 1), :] + dt)
            it.beta = jax.nn.sigmoid(gc_ref[0, 0, it.rows, ib:ib + 1])
            incl = (ci <= ri) if fwd else (ci >= ri)
            incl_t = (ri <= ci) if fwd else (ri >= ci)
            it.bc = jnp.sum(jnp.where(incl, g_row, 0.0), axis=1, keepdims=True)
            bc_row = jnp.sum(jnp.where(incl_t, g_col, 0.0), axis=0, keepdims=True)
            it.lmat = jnp.where(incl, jnp.exp(jnp.minimum(it.bc - bc_row, 0.0)), 0.0)
            it.lvl = (lvlf_ref if fwd else lvlb_ref)[...]
            its.append(it)
        yield
        for it in its:
            it.amat = it.beta * _dot_nt(it.k, it.k) * it.lmat
            it.qk = _bf(_dot_nt(it.q, it.k) * it.lmat)
        yield
        for it in its:
            it.x = eye - jnp.where(it.lvl == 0, it.amat, 0.0)
        for lev in range(1, 6):
            for it in its:
                it.xb = _bf(it.x)
                it.y = _dot(_bf(jnp.where(it.lvl == lev, it.amat, 0.0)), it.xb)
            yield
            for it in its:
                it.x = it.x - _dot(it.xb, _bf(it.y))
            yield
        for it in its:
            e_col = jnp.exp(it.bc)
            kf = it.k.astype(F32)
            sol = _dot(_bf(it.x), _bf(jnp.concatenate([it.v * it.beta, kf * (it.beta * e_col)], axis=1)))
            it.u = _bf(sol[:, :HEAD_DIM])
            it.w = _bf(sol[:, HEAD_DIM:])
            bc_tot = it.bc[CHUNK - 1:CHUNK] if it.fwd else it.bc[0:1]
            it.qd = it.q.astype(F32) * e_col
            it.kd = _bf(kf * jnp.exp(bc_tot - it.bc))
            it.last = jnp.exp(bc_tot)
        yield
        for it in its:
            it.kw = _bf(_dot_tn(it.kd, it.w))
            it.ku = _dot_tn(it.kd, it.u)
            it.qs = _bf(it.qd - _dot(it.qk, it.w))
            it.o0 = _dot(it.qk, it.u)
        yield
        for it in its:
            s = st[it.fwd]
            sb = _bf(s)
            out = it.o0 + _dot(it.qs, sb)
            st[it.fwd] = it.last * s + (it.ku - _dot(it.kw, sb))
            if it.fwd:
                of_scr[it.rows, :] = out
            else:
                o_ref[0, it.rows, :] = out.astype(o_ref.dtype)

    def finish_fn(rows_list):
        _finish_heads(rows_list, o_ref, of_scr, zg_ref, nw_ref, centre=False)

    _scan_both(group_fn, finish_fn, n_ctx_chunks, n_chunks, GDN_GROUP, GDN_WAVES)


def _gdn(z, zgate, merged, conv_w, a_log_f, a_log_b, dt_f, dt_b, norm_w, n_ctx):
    bsz, t, _ = z.shape
    n = t // CHUNK
    g = zgate[:, :, :N_GATES].reshape(bsz, t, 4, B_HEADS)
    gate_cols = g.transpose(0, 3, 1, 2)
    gate_rows = g.reshape(bsz, n, CHUNK, 4, B_HEADS).transpose(0, 4, 3, 1, 2)
    par = jnp.stack([a_log_f, dt_f, a_log_b, dt_b] + [jnp.zeros_like(dt_f)] * 4, axis=1)
    par = jnp.broadcast_to(par[:, :, None], (B_HEADS, 8, LANES)).astype(F32)
    lvl_f, lvl_b, _, _ = _level_constants()
    kern = functools.partial(_gdn_kernel, n_ctx_chunks=n_ctx // CHUNK, n_chunks=n)

    def wspec(base):
        return pl.BlockSpec((3, LANES), lambda b, h: (0, base + h))

    return pl.pallas_call(
        kern,
        grid=(bsz, B_HEADS),
        in_specs=[_zspec(t, ZB_BQ), _zspec(t, ZB_BK), _zspec(t, ZB_BV), _zspec(t, ZB_BG),
                  wspec(0), wspec(B_HEADS), wspec(2 * B_HEADS),
                  pl.BlockSpec((1, 1, t, 4), lambda b, h: (b, h, 0, 0)),
                  pl.BlockSpec((1, 1, 4, n, CHUNK), lambda b, h: (b, h, 0, 0, 0)),
                  pl.BlockSpec((1, 8, LANES), lambda b, h: (h, 0, 0)),
                  _const_spec((1, LANES)), _const_spec(lvl_f.shape), _const_spec(lvl_b.shape),
                  pl.BlockSpec(memory_space=pl.ANY)],
        out_specs=pl.BlockSpec((1, t, LANES), lambda b, h: (b, 0, A_HEADS + h)),
        out_shape=jax.ShapeDtypeStruct(merged.shape, BF16),
        input_output_aliases={13: 0},
        scratch_shapes=[pltpu.VMEM((t, HEAD_DIM), BF16)] * 3 + [pltpu.VMEM((t, HEAD_DIM), F32)],
        compiler_params=_cparams(2),
        name="gdn",
    )(z, z, z, z, conv_w, conv_w, conv_w, gate_cols, gate_rows, par, norm_w.reshape(1, LANES),
      jnp.asarray(lvl_f), jnp.asarray(lvl_b), merged)


def _ret_kernel(zq_ref, zk_ref, zv_ref, zg_ref, base_ref, loc_ref, dec_ref, dmat_ref, cdec_ref, nw_ref,
                merged_ref, o_ref, of_scr, *, n_ctx_chunks, n_chunks):
    del merged_ref
    def group_fn(items, st):
        its = []
        for c, fwd in items:
            it = _Item()
            it.fwd, it.rows, it.d = fwd, _rows(c), 0 if fwd else 1
            d = it.d
            cb = base_ref[2 * d, pl.ds(c, 1), :]
            sb = base_ref[2 * d + 1, pl.ds(c, 1), :]
            cl, sl, cls, sls = (loc_ref[4 * d + n] for n in range(4))
            cos = cb * cl - sb * sl
            sin = sb * cls + cb * sls

            def rot(a):
                return a * cos + pltpu.roll(a, HEAD_DIM // 2, 1) * sin

            q = rot(zq_ref[0, it.rows, :].astype(F32) * SCALE)
            k = rot(zk_ref[0, it.rows, :].astype(F32))
            it.q, it.k = _bf(q), _bf(k)
            it.qd = _bf(q * dec_ref[0, 2 * d])
            it.kd = _bf(k * dec_ref[0, 2 * d + 1])
            it.v = zv_ref[0, it.rows, :]
            its.append(it)
        yield
        for it in its:
            it.scores = _bf(_dot_nt(it.q, it.k) * dmat_ref[0, it.d])
        yield
        for it in its:
            it.intra = _dot(it.scores, it.v)
            it.upd = _dot_tn(it.v, it.kd)
        yield
        for it in its:
            out = it.intra + _dot_nt(it.qd, _bf(st[it.fwd]))
            st[it.fwd] = st[it.fwd] * cdec_ref[0, it.d:it.d + 1, :] + it.upd
            if it.fwd:
                of_scr[it.rows, :] = out
            else:
                o_ref[0, it.rows, :] = out.astype(o_ref.dtype)

    def finish_fn(rows_list):
        _finish_heads(rows_list, o_ref, of_scr, zg_ref, nw_ref, centre=True)

    _scan_both(group_fn, finish_fn, n_ctx_chunks, n_chunks, RET_GROUP, RET_WAVES)


def _retention_tables(n_ctx, t):
    half = HEAD_DIM // 2
    n = t // CHUNK
    nc = n_ctx // CHUNK
    inv = ROPE_BASE ** (-jnp.arange(half, dtype=F32) / half)
    inv = jnp.concatenate([inv, inv])
    sgn = jnp.concatenate([-jnp.ones((half,), F32), jnp.ones((half,), F32)])
    c_idx = np.arange(n)
    base_f = (c_idx * CHUNK).astype(np.float32)
    base_b = np.where(c_idx < nc, n_ctx - CHUNK * (c_idx + 1), t - CHUNK * (c_idx - nc + 1)).astype(np.float32)
    loc_f = np.arange(CHUNK, dtype=np.float32)
    loc_b = loc_f[::-1].copy()

    def ang(p):
        return jnp.asarray(p)[:, None] * inv[None, :]

    base = jnp.stack([jnp.cos(ang(base_f)), jnp.sin(ang(base_f)), jnp.cos(ang(base_b)), jnp.sin(ang(base_b))])
    loc = jnp.stack([jnp.cos(ang(loc_f)), jnp.sin(ang(loc_f)), jnp.cos(ang(loc_f)) * sgn, jnp.sin(ang(loc_f)) * sgn,
                     jnp.cos(ang(loc_b)), jnp.sin(ang(loc_b)), jnp.cos(ang(loc_b)) * sgn, jnp.sin(ang(loc_b)) * sgn])
    lg_f = jnp.log1p(-jnp.exp2(-5.0 - jnp.arange(C_HEADS, dtype=F32)))
    lg_b = lg_f[::-1]
    idx = jnp.arange(CHUNK, dtype=F32)
    rel = idx[:, None] - idx[None, :]

    def masked_exp(a, mask):
        return jnp.where(mask, jnp.exp(jnp.where(mask, a, 0.0)), 0.0)

    dmat_f = masked_exp(lg_f[:, None, None] * rel, (rel >= 0)[None])
    dmat_b = masked_exp(-lg_b[:, None, None] * rel, (rel <= 0)[None])
    dq_f = jnp.exp(lg_f[:, None] * (idx + 1.0))
    dk_f = jnp.exp(lg_f[:, None] * (CHUNK - 1.0 - idx))
    dq_b = jnp.exp(lg_b[:, None] * (CHUNK - idx))
    dk_b = jnp.exp(lg_b[:, None] * idx)
    dec = jnp.stack([dq_f, dk_f, dq_b, dk_b], axis=1)
    dec = jnp.broadcast_to(dec[..., None], (C_HEADS, 4, CHUNK, LANES))
    cdec = jnp.stack([jnp.exp(lg_f * CHUNK), jnp.exp(lg_b * CHUNK)] + [jnp.zeros((C_HEADS,), F32)] * 6, axis=1)
    cdec = jnp.broadcast_to(cdec[..., None], (C_HEADS, 8, LANES))
    return base, loc, dec, jnp.stack([dmat_f, dmat_b], axis=1), cdec


def _retention(z, merged, norm_w, n_ctx):
    bsz, t, _ = z.shape
    n = t // CHUNK
    base, loc, dec, dmat, cdec = _retention_tables(n_ctx, t)
    kern = functools.partial(_ret_kernel, n_ctx_chunks=n_ctx // CHUNK, n_chunks=n)
    return pl.pallas_call(
        kern,
        grid=(bsz, C_HEADS),
        in_specs=[_zspec(t, ZB_CQ), _zspec(t, ZB_CK), _zspec(t, ZB_CV), _zspec(t, ZB_CG),
                  _const_spec(base.shape), _const_spec(loc.shape),
                  pl.BlockSpec((1, 4, CHUNK, LANES), lambda b, h: (h, 0, 0, 0)),
                  pl.BlockSpec((1, 2, CHUNK, CHUNK), lambda b, h: (h, 0, 0, 0)),
                  pl.BlockSpec((1, 8, LANES), lambda b, h: (h, 0, 0)),
                  pl.BlockSpec((1, LANES), lambda b, h: (0, h)),
                  pl.BlockSpec(memory_space=pl.ANY)],
        out_specs=pl.BlockSpec((1, t, LANES), lambda b, h: (b, 0, A_HEADS + B_HEADS + h)),
        out_shape=jax.ShapeDtypeStruct(merged.shape, BF16),
        input_output_aliases={10: 0},
        scratch_shapes=[pltpu.VMEM((t, HEAD_DIM), F32)],
        compiler_params=_cparams(2),
        name="retention",
    )(z, z, z, z, base, loc, dec, dmat, cdec, norm_w.reshape(1, C_W), merged)


def _reorder_in_proj(w):
    g0 = 5 * A_W + 4 * B_W
    main = jnp.concatenate([w[:, :g0], w[:, g0 + N_GATES:]], axis=1).astype(BF16)
    gates = jnp.pad(w[:, g0:g0 + N_GATES], ((0, 0), (0, LANES - N_GATES))).astype(BF16)
    return main, gates


def _lower_bounds(logits):
    p = jax.nn.softmax(logits.astype(F32), axis=0)
    return jnp.cumsum(p, axis=0) - p[0]


def kernel(x, c, ctx, c_ctx, w_mod, b_mod, norm1_w, norm2_w, w_in, qkv_conv_w, hgrn_lb_logits_fwd, hgrn_lb_logits_bwd, hgrn_norm_w, gdn_a_log_fwd, gdn_a_log_bwd, gdn_dt_bias_fwd, gdn_dt_bias_bwd, gdn_norm_w, ret_norm_w, w_out, ffn_w_up, ffn_conv_w, ffn_conv_b, ffn_w_down, final_norm_w):
    bsz, n_lat, d = x.shape
    n_ctx = ctx.shape[1]
    depth = w_mod.shape[0]
    t = n_ctx + n_lat
    assert n_ctx % ROW_TILE == 0 and n_lat % ROW_TILE == 0 and bsz < MOD_ROWS
    lb_fwd = _lower_bounds(hgrn_lb_logits_fwd)
    lb_bwd = _lower_bounds(hgrn_lb_logits_bwd)

    rows = jnp.concatenate([c, c_ctx[None, :], jnp.zeros((MOD_ROWS - bsz - 1, d), F32)], axis=0)
    mods = _modulation(rows, w_mod, b_mod)

    tm = MM_ROWS
    m_rows = bsz * t
    f2 = ffn_w_up.shape[2]
    zw = Z_BLOCKS * LANES
    assert m_rows % tm == 0 and zw % (IN_PROJ_COL_TILES * LANES) == 0 and f2 % (FFN_UP_COL_TILES * LANES) == 0
    mod = [mods[l].reshape(MOD_ROWS * N_MOD, 1, d) for l in range(depth)]
    h = _norm_mod(ctx, x, norm1_w[0], mod[0], 1, 0)
    xs = x
    for l in range(depth):
        w_main, w_gate = _reorder_in_proj(w_in[l])
        h = h.reshape(m_rows, d)
        z = _matmul(h, w_main, tm, zw // IN_PROJ_COL_TILES, BF16, "in_proj").reshape(bsz, t, zw)
        zgate = _matmul(h, w_gate, tm, LANES, F32, "in_proj_gates").reshape(bsz, t, LANES)
        merged = jnp.zeros((bsz, t, N_HEADS * HEAD_DIM), BF16)
        merged = _hgrn(z, merged, lb_fwd[l], lb_bwd[l], hgrn_norm_w[l], n_ctx)
        merged = _gdn(z, zgate, merged, qkv_conv_w[l], gdn_a_log_fwd[l], gdn_a_log_bwd[l],
                      gdn_dt_bias_fwd[l], gdn_dt_bias_bwd[l], gdn_norm_w[l], n_ctx)
        merged = _retention(z, merged, ret_norm_w[l], n_ctx)
        xs, h = _out_proj(merged, w_out[l].astype(BF16), ctx if l == 0 else None, xs, mod[l], 2,
                          norm2_w[l], 4, 3, n_ctx)

        uv = _matmul(h.reshape(m_rows, d), ffn_w_up[l].astype(BF16), tm, f2 // FFN_UP_COL_TILES, BF16,
                     "ffn_up").reshape(bsz, t, f2)
        w_down = ffn_w_down[l].astype(BF16)
        if l + 1 < depth:
            xs, h = _ffn_tail(uv, ffn_conv_w[l], ffn_conv_b[l], w_down, xs, mod[l], 5,
                              norm1_w[l + 1], mod[l + 1], 1, 0, n_ctx, FFN_SLAB)
    return _ffn_tail(uv, ffn_conv_w[-1], ffn_conv_b[-1], w_down, xs, mod[-1], 5,
                     final_norm_w, None, None, None, n_ctx, FFN_SLAB)
```
